```python
import jax
import jax.numpy as jnp
from jax import lax
import numpy as np

D_MODEL = 2048
BATCH = 4
SEQ = 8192
DEPTH = 1
DEC_BATCH = 16
DEC_SEQ = 16
PAST_LEN = 4096

CHUNK = 64
D_A = D_MODEL // 2
HEAD_A = 64
N_HEADS_A = D_A // HEAD_A
LORA_W = 64
LORA_A = 64
LORA_G = 160
SHIFT_W = 3 * D_A + LORA_W + LORA_A + LORA_G
D_B = D_MODEL // 2
CONV_W = 31
N_BRANCH = 2
D_IN = SHIFT_W + 2 * D_B + N_BRANCH * D_MODEL
N_GROUPS = 4
EXP_PER_GROUP = 8
N_EXPERTS = N_GROUPS * EXP_PER_GROUP
TOP_K = 2
D_EXPERT = 512
RMS_EPS = 1e-6
LN_EPS = 1e-5
GN_EPS = 64e-5

kernel_name = "rwkv7_conformer_gated_hmoe_stream_step"


def rms_norm(x, g):
    xf = x.astype(jnp.float32)
    y = xf * lax.rsqrt(jnp.mean(xf * xf, axis=-1, keepdims=True) + RMS_EPS)
    return (y * g.astype(jnp.float32)).astype(x.dtype)


def wkv7_recurrence(r, w, k, v, a, b, s0):
    def step(s, inp):
        r_t, w_t, k_t, v_t, a_t, b_t = inp
        sa = jnp.einsum("bhvk,bhk->bhv", s, a_t)
        s = s * w_t[:, :, None, :] + sa[..., None] * b_t[:, :, None, :] + v_t[..., None] * k_t[:, :, None, :]
        return s, jnp.einsum("bhvk,bhk->bhv", s, r_t)
    xs = tuple(jnp.swapaxes(t, 0, 1) for t in (r, w, k, v, a, b))
    s_T, ys = lax.scan(step, s0, xs)
    return jnp.swapaxes(ys, 0, 1), s_T


def rwkv7_time_mix(p, p_prev, s0, mu_shift, w0, w_lora2, a0, a_lora2, g_lora2, k_k, k_a, r_k, lnx_g, lnx_b):
    f32 = jnp.float32
    B, T, _ = p.shape
    p_shift = jnp.concatenate([p_prev[:, None, :].astype(p.dtype), p[:, :-1]], axis=1)
    xs = p + (p_shift - p) * mu_shift
    r, xw, k, v, xa, xg = jnp.split(
        xs, [D_A, D_A + LORA_W, 2 * D_A + LORA_W, 3 * D_A + LORA_W, 3 * D_A + LORA_W + LORA_A], axis=-1)
    w_log = -jax.nn.softplus(-(w0 + jnp.tanh(xw) @ w_lora2).astype(f32)) - 0.5
    decay = jnp.exp(-jnp.exp(w_log))
    a = jax.nn.sigmoid((a0 + xa @ a_lora2).astype(f32))
    g = (jax.nn.sigmoid(xg) @ g_lora2).astype(f32)
    heads = lambda t: t.astype(f32).reshape(B, T, N_HEADS_A, HEAD_A)
    kk = heads(k * k_k)
    kk = kk / jnp.maximum(jnp.linalg.norm(kk, axis=-1, keepdims=True), 1e-12)
    k = k.astype(f32) * (1.0 + (a - 1.0) * k_a.astype(f32))
    r_h, k_h, v_h, a_h = heads(r), heads(k), heads(v), heads(a)
    y, s_new = wkv7_recurrence(r_h, heads(decay), k_h, v_h, -kk, kk * a_h, s0.astype(f32))
    mu = jnp.mean(y, axis=-1, keepdims=True)
    var = jnp.mean(jnp.square(y - mu), axis=-1, keepdims=True)
    yn = ((y - mu) * lax.rsqrt(var + GN_EPS)).reshape(B, T, D_A) * lnx_g.astype(f32) + lnx_b.astype(f32)
    bonus = jnp.sum(r_h * k_h * r_k.astype(f32), axis=-1, keepdims=True) * v_h
    o = (yn + bonus.reshape(B, T, D_A)) * g
    return o.astype(p.dtype), s_new.astype(s0.dtype)


def conformer_conv(u2, buf, conv_w, conv_b, cln_g, cln_b):
    u = u2[..., :D_B] * jax.nn.sigmoid(u2[..., D_B:])
    ext = jnp.concatenate([buf.astype(u.dtype), u], axis=1)
    z = lax.conv_general_dilated(ext, conv_w[:, None, :].astype(u.dtype), window_strides=(1,), padding="VALID",
                                 dimension_numbers=("NWC", "WIO", "NWC"), feature_group_count=D_B) + conv_b
    zf = z.astype(jnp.float32)
    mu = jnp.mean(zf, axis=-1, keepdims=True)
    var = jnp.mean(jnp.square(zf - mu), axis=-1, keepdims=True)
    zn = (zf - mu) * lax.rsqrt(var + LN_EPS) * cln_g.astype(jnp.float32) + cln_b.astype(jnp.float32)
    return jax.nn.silu(zn).astype(u.dtype), ext[:, -(CONV_W - 1):]


def hier_moe(h, w_rg, b_rg, w_re, b_re, w_gate_e, w_up_e, w_down_e):
    f32 = jnp.float32
    B, T, D = h.shape
    t = h.reshape(B * T, D)
    lg = (t @ w_rg).astype(f32) + b_rg.astype(f32)
    g_sel = jnp.argmax(lg, axis=-1)
    p_grp = jnp.take_along_axis(jax.nn.softmax(lg, axis=-1), g_sel[:, None], axis=-1)
    le = ((t @ w_re).astype(f32) + b_re.astype(f32)).reshape(-1, N_GROUPS, EXP_PER_GROUP)
    le = jnp.take_along_axis(le, g_sel[:, None, None], axis=1)[:, 0]
    top_p, top_i = lax.top_k(jax.nn.softmax(le, axis=-1), TOP_K)
    top_p = top_p / jnp.sum(top_p, axis=-1, keepdims=True)
    w_grp = jnp.sum(jax.nn.one_hot(top_i, EXP_PER_GROUP, dtype=f32) * top_p[..., None], axis=1) * p_grp
    gate = (jax.nn.one_hot(g_sel, N_GROUPS, dtype=f32)[:, :, None] * w_grp[:, None, :]).astype(h.dtype)
    wg = w_gate_e.reshape(N_GROUPS, EXP_PER_GROUP, D, D_EXPERT)
    wu = w_up_e.reshape(N_GROUPS, EXP_PER_GROUP, D, D_EXPERT)
    wd = w_down_e.reshape(N_GROUPS, EXP_PER_GROUP, D_EXPERT, D)
    out = jnp.zeros_like(t)
    for gi in range(N_GROUPS):
        hid = jax.nn.silu(jnp.einsum("nd,edf->nef", t, wg[gi])) * jnp.einsum("nd,edf->nef", t, wu[gi])
        out = out + jnp.einsum("nef,efd->nd", hid * gate[:, gi, :, None], wd[gi])
    return out.reshape(B, T, D)


def trunk_layer(x, shift_prev, wkv_prev, conv_prev, norm1_g, w_in, b_conv_in, b_gate, mu_shift, w0, w_lora2,
                a0, a_lora2, g_lora2, k_k, k_a, r_k, lnx_g, lnx_b, w_out_a, conv_w, conv_b, cln_g, cln_b,
                w_out_b, w_o, norm2_g, w_rg, b_rg, w_re, b_re, w_gate_e, w_up_e, w_down_e):
    h = rms_norm(x, norm1_g)
    proj = h @ w_in
    p_rwkv, u2, gate_in = jnp.split(proj, [SHIFT_W, SHIFT_W + 2 * D_B], axis=-1)
    o_a, wkv_new = rwkv7_time_mix(p_rwkv, shift_prev, wkv_prev, mu_shift, w0, w_lora2, a0, a_lora2, g_lora2,
                                  k_k, k_a, r_k, lnx_g, lnx_b)
    o_b, conv_new = conformer_conv(u2 + b_conv_in, conv_prev, conv_w, conv_b, cln_g, cln_b)
    g_a, g_b = jnp.split(jax.nn.sigmoid(gate_in + b_gate), N_BRANCH, axis=-1)
    merged = g_a * (o_a @ w_out_a) + g_b * (o_b @ w_out_b)
    x = x + merged @ w_o
    x = x + hier_moe(rms_norm(x, norm2_g), w_rg, b_rg, w_re, b_re, w_gate_e, w_up_e, w_down_e)
    return x, p_rwkv[:, -1], wkv_new, conv_new


def setup_inputs(seed: int = 0) -> dict:
    key = jax.random.key(seed)
    ks = iter(jax.random.split(key, 48))
    f32 = jnp.float32
    nrm = lambda shape, scale: jax.random.normal(next(ks), shape, f32) * scale
    uni = lambda shape, lo, hi: jax.random.uniform(next(ks), shape, f32, lo, hi)
    L = DEPTH
    return {
        "x_prompt": nrm((BATCH, SEQ, D_MODEL), 1.0),
        "x_sample": nrm((DEC_BATCH, DEC_SEQ, D_MODEL), 1.0),
        "state_shift": nrm((L, DEC_BATCH, SHIFT_W), 1.0),
        "state_wkv": nrm((L, DEC_BATCH, N_HEADS_A, HEAD_A, HEAD_A), 0.3),
        "cache_conv": nrm((L, DEC_BATCH, CONV_W - 1, D_B), 0.5),
        "norm1_g": 1.0 + nrm((L, D_MODEL), 0.02),
        "w_in": nrm((L, D_MODEL, D_IN), D_MODEL ** -0.5),
        "b_conv_in": nrm((L, 2 * D_B), 0.02),
        "b_gate": nrm((L, N_BRANCH * D_MODEL), 0.02),
        "mu_shift": uni((L, SHIFT_W), 0.0, 1.0),
        "w0": nrm((L, D_A), 0.5) - 0.5,
        "w_lora2": nrm((L, LORA_W, D_A), 0.5 * LORA_W ** -0.5),
        "a0": nrm((L, D_A), 0.1),
        "a_lora2": nrm((L, LORA_A, D_A), 0.5 * LORA_A ** -0.5),
        "g_lora2": nrm((L, LORA_G, D_A), LORA_G ** -0.5),
        "k_k": 0.85 + nrm((L, D_A), 0.05),
        "k_a": 1.0 + nrm((L, D_A), 0.05),
        "r_k": nrm((L, N_HEADS_A, HEAD_A), 0.1),
        "lnx_g": 1.0 + nrm((L, D_A), 0.02),
        "lnx_b": nrm((L, D_A), 0.02),
        "w_out_a": nrm((L, D_A, D_MODEL), D_A ** -0.5),
        "conv_w": nrm((L, CONV_W, D_B), CONV_W ** -0.5),
        "conv_b": nrm((L, D_B), 0.02),
        "cln_g": 1.0 + nrm((L, D_B), 0.02),
        "cln_b": nrm((L, D_B), 0.02),
        "w_out_b": nrm((L, D_B, D_MODEL), D_B ** -0.5),
        "w_o": nrm((L, D_MODEL, D_MODEL), D_MODEL ** -0.5),
        "norm2_g": 1.0 + nrm((L, D_MODEL), 0.02),
        "w_rg": nrm((L, D_MODEL, N_GROUPS), D_MODEL ** -0.5),
        "b_rg": nrm((L, N_GROUPS), 0.01),
        "w_re": nrm((L, D_MODEL, N_EXPERTS), D_MODEL ** -0.5),
        "b_re": nrm((L, N_EXPERTS), 0.01),
        "w_gate_e": nrm((L, N_EXPERTS, D_MODEL, D_EXPERT), D_MODEL ** -0.5),
        "w_up_e": nrm((L, N_EXPERTS, D_MODEL, D_EXPERT), D_MODEL ** -0.5),
        "w_down_e": nrm((L, N_EXPERTS, D_EXPERT, D_MODEL), D_EXPERT ** -0.5),
        "final_g": 1.0 + nrm((D_MODEL,), 0.02),
    }


def reference(x_prompt, x_sample, state_shift, state_wkv, cache_conv, norm1_g, w_in, b_conv_in, b_gate,
              mu_shift, w0, w_lora2, a0, a_lora2, g_lora2, k_k, k_a, r_k, lnx_g, lnx_b, w_out_a, conv_w,
              conv_b, cln_g, cln_b, w_out_b, w_o, norm2_g, w_rg, b_rg, w_re, b_re, w_gate_e, w_up_e,
              w_down_e, final_g):
    per_layer = (norm1_g, w_in, b_conv_in, b_gate, mu_shift, w0, w_lora2, a0, a_lora2, g_lora2, k_k, k_a,
                 r_k, lnx_g, lnx_b, w_out_a, conv_w, conv_b, cln_g, cln_b, w_out_b, w_o, norm2_g, w_rg, b_rg,
                 w_re, b_re, w_gate_e, w_up_e, w_down_e)

    def run(x, shift0, wkv0, conv0):
        shifts, wkvs, convs = [], [], []
        for l in range(DEPTH):
            x, sh, wk, cv = trunk_layer(x, shift0[l], wkv0[l], conv0[l], *[p[l] for p in per_layer])
            shifts.append(sh)
            wkvs.append(wk)
            convs.append(cv)
        return rms_norm(x, final_g), jnp.stack(shifts), jnp.stack(wkvs), jnp.stack(convs)

    zero_shift = jnp.zeros((DEPTH, BATCH, SHIFT_W), x_prompt.dtype)
    zero_wkv = jnp.zeros((DEPTH, BATCH, N_HEADS_A, HEAD_A, HEAD_A), state_wkv.dtype)
    zero_conv = jnp.zeros((DEPTH, BATCH, CONV_W - 1, D_B), x_prompt.dtype)
    y_prompt, new_shift_prompt, new_wkv_prompt, new_conv_prompt = run(x_prompt, zero_shift, zero_wkv, zero_conv)
    y_sample, new_shift_sample, new_wkv_sample, new_conv_sample = run(x_sample, state_shift, state_wkv, cache_conv)
    return (y_prompt, y_sample, new_shift_prompt, new_wkv_prompt, new_conv_prompt,
            new_shift_sample, new_wkv_sample, new_conv_sample)
```

```python
import functools
import math

import numpy as np
import jax
import jax.numpy as jnp
from jax import lax
from jax.experimental import pallas as pl
from jax.experimental.pallas import tpu as pltpu

F32 = jnp.float32
BF16 = jnp.bfloat16

D_MODEL = 2048
D_A = 1024
HEAD = 64
N_HEADS = D_A // HEAD
LORA_W = 64
LORA_A = 64
LORA_G = 160
SHIFT_W = 3 * D_A + LORA_W + LORA_A + LORA_G
D_B = 1024
CONV_W = 31
N_GROUPS = 4
EXP_PER_GROUP = 8
N_EXPERTS = N_GROUPS * EXP_PER_GROUP
D_EXPERT = 512
RMS_EPS = 1e-6
LN_EPS = 1e-5
GN_EPS = 64e-5

LANE = 128
P_W = 27 * LANE
LORA_IN0 = 3 * D_A
XG0 = LORA_IN0 + LANE
HALO = 32
ROUTE_W = LANE
TM = 256
TMX = 256
VMEM_LIMIT = 56 * 1024 * 1024

_PERM = np.concatenate([
    np.arange(0, D_A),
    np.arange(D_A + LORA_W, 2 * D_A + LORA_W),
    np.arange(2 * D_A + LORA_W, 3 * D_A + LORA_W),
    np.arange(D_A, D_A + LORA_W),
    np.arange(3 * D_A + LORA_W, SHIFT_W),
])
_INV_PERM = np.argsort(_PERM)


def _dot(a, b):
    return jnp.dot(a, b, preferred_element_type=F32)


def _dot_nt(a, b):
    return lax.dot_general(a, b, (((1,), (1,)), ((), ())), preferred_element_type=F32)


def _dot_tn(a, b):
    return lax.dot_general(a, b, (((0,), (0,)), ((), ())), preferred_element_type=F32)


def _split(x):
    hi = x.astype(BF16)
    lo = (x - hi.astype(F32)).astype(BF16)
    return hi, lo


def _dot3(x, w_hi, w_lo):
    x_hi, x_lo = _split(x)
    return _dot(x_hi, w_hi) + (_dot(x_lo, w_hi) + _dot(x_hi, w_lo))


def _rms(x, g):
    return x * lax.rsqrt(jnp.mean(x * x, axis=-1, keepdims=True) + RMS_EPS) * g


def _head_sum(x):
    r = lax.broadcasted_iota(jnp.int32, (LANE, LANE), 0) // HEAD
    c = lax.broadcasted_iota(jnp.int32, (LANE, LANE), 1) // HEAD
    ones = jnp.where(r == c, 1.0, 0.0).astype(BF16)
    outs = []
    for j in range(x.shape[1] // LANE):
        hi, lo = _split(x[:, j * LANE:(j + 1) * LANE])
        outs.append(_dot(hi, ones) + _dot(lo, ones))
    return jnp.concatenate(outs, axis=1)


def _const_spec(shape):
    nd = len(shape)
    return pl.BlockSpec(shape, lambda *_: (0,) * nd, pipeline_mode=pl.Buffered(1))


def _params(sem):
    return pltpu.CompilerParams(dimension_semantics=sem, vmem_limit_bytes=VMEM_LIMIT)


def _inproj_kernel(x_ref, g_ref, wr_ref, wc_ref, bc_ref, p_ref, u_ref):
    h = _rms(x_ref[...], g_ref[...]).astype(BF16)
    p_ref[...] = _dot(h, wr_ref[...])
    c = _dot(h, wc_ref[...]) + bc_ref[...]
    u_ref[...] = c[:, :D_B] * jax.nn.sigmoid(c[:, D_B:])


def _inproj(x, g, wr, wc, bc):
    n = x.shape[0]
    return pl.pallas_call(
        _inproj_kernel,
        grid=(n // TM,),
        in_specs=[
            pl.BlockSpec((TM, D_MODEL), lambda i: (i, 0)),
            _const_spec((1, D_MODEL)),
            _const_spec((D_MODEL, P_W)),
            _const_spec((D_MODEL, 2 * D_B)),
            _const_spec((1, 2 * D_B)),
        ],
        out_specs=[
            pl.BlockSpec((TM, P_W), lambda i: (i, 0)),
            pl.BlockSpec((TM, D_B), lambda i: (i, 0)),
        ],
        out_shape=[
            jax.ShapeDtypeStruct((n, P_W), F32),
            jax.ShapeDtypeStruct((n, D_B), F32),
        ],
        compiler_params=_params(("parallel",)),
        name="inproj",
    )(x, g, wr, wc, bc)


def _prep_kernel(p_ref, bnd_ref, mu_ref, vp_ref, w2h_ref, w2l_ref, g2h_ref, g2l_ref,
                 rt_ref, at_ref, bt_ref, kt_ref, v_ref, ge_ref, bonus_ref, g_ref,
                 *, chunk, nb, step):
    tm = p_ref.shape[0]
    p = p_ref[...]
    row = lax.broadcasted_iota(jnp.int32, (tm, 1), 0)
    prev = pltpu.roll(p, 1, axis=0)
    for j in range(nb):
        prev = jnp.where(row == j * step, bnd_ref[0, j:j + 1, :], prev)
    xs = p + (prev - p) * mu_ref[...]
    r = xs[:, 0:D_A]
    k = xs[:, D_A:2 * D_A]
    v = xs[:, 2 * D_A:3 * D_A]
    lin = xs[:, LORA_IN0:LORA_IN0 + LANE]
    lane = lax.broadcasted_iota(jnp.int32, (1, LANE), 1)
    lin = jnp.where(lane < LORA_W, jnp.tanh(lin), lin)
    lo = _dot3(lin, w2h_ref[...], w2l_ref[...])
    w0 = vp_ref[0:1, :]
    a0 = vp_ref[1:2, :]
    k_k = vp_ref[2:3, :]
    k_a = vp_ref[3:4, :]
    r_k = vp_ref[4:5, :]
    zw = -(w0 + lo[:, :D_A])
    softplus = jnp.maximum(zw, 0.0) + jnp.log1p(jnp.exp(-jnp.abs(zw)))
    logw = -jnp.exp(-softplus - 0.5)
    a = jax.nn.sigmoid(a0 + lo[:, D_A:])
    g = _dot3(jax.nn.sigmoid(xs[:, XG0:XG0 + 2 * LANE]), g2h_ref[...], g2l_ref[...])
    kku = k * k_k
    kk = kku / jnp.maximum(jnp.sqrt(_head_sum(kku * kku)), 1e-12)
    k2 = k * (1.0 + (a - 1.0) * k_a)
    bonus_ref[...] = _head_sum(r * k2 * r_k) * v
    g_ref[...] = g
    ri = lax.broadcasted_iota(jnp.int32, (tm, tm), 0)
    ci = lax.broadcasted_iota(jnp.int32, (tm, tm), 1)
    tri = jnp.where((ri // chunk == ci // chunk) & (ci <= ri), 1.0, 0.0).astype(BF16)
    lw_hi, lw_lo = _split(logw)
    cum = _dot(tri, lw_hi) + _dot(tri, lw_lo)
    eg = jnp.exp(cum)
    egi = jnp.exp(-cum)
    rt_ref[...] = (r * eg).astype(BF16)
    at_ref[...] = (-kk * jnp.exp(cum - logw)).astype(BF16)
    bt_ref[...] = (kk * a * egi).astype(BF16)
    kt_ref[...] = (k2 * egi).astype(BF16)
    v_ref[...] = v.astype(BF16)
    for c in range(tm // chunk):
        ge_ref[c] = eg[(c + 1) * chunk - 1:(c + 1) * chunk, :]


def _prep(p, bnd, mu, vp, w2h, w2l, g2h, g2l, *, chunk, nb, step):
    n = p.shape[0]
    tok = pl.BlockSpec((TM, D_A), lambda i: (i, 0))
    return pl.pallas_call(
        functools.partial(_prep_kernel, chunk=chunk, nb=nb, step=step),
        grid=(n // TM,),
        in_specs=[
            pl.BlockSpec((TM, P_W), lambda i: (i, 0)),
            pl.BlockSpec((1, nb, P_W), lambda i: (i, 0, 0)),
            _const_spec((1, P_W)),
            _const_spec((8, D_A)),
            _const_spec((LANE, 2 * D_A)),
            _const_spec((LANE, 2 * D_A)),
            _const_spec((2 * LANE, D_A)),
            _const_spec((2 * LANE, D_A)),
        ],
        out_specs=[tok, tok, tok, tok, tok,
                   pl.BlockSpec((TM // chunk, 1, D_A), lambda i: (i, 0, 0)),
                   tok, tok],
        out_shape=[jax.ShapeDtypeStruct((n, D_A), BF16)] * 5 + [
            jax.ShapeDtypeStruct((n // chunk, 1, D_A), F32),
            jax.ShapeDtypeStruct((n, D_A), F32),
            jax.ShapeDtypeStruct((n, D_A), F32),
        ],
        compiler_params=_params(("parallel",)),
        name="rwkv_prep",
    )(p, bnd, mu, vp, w2h, w2l, g2h, g2l)


def _wkv_kernel(rt_ref, at_ref, bt_ref, kt_ref, v_ref, ge_ref, s0_ref, y_ref, st_ref, h_scr,
                *, chunk, nchunk):
    j = pl.program_id(2)

    @pl.when(j == 0)
    def _():
        for hh in range(2):
            h_scr[hh] = s0_ref[0, hh].T

    c = chunk
    row = lax.broadcasted_iota(jnp.int32, (c, c), 0)
    col = lax.broadcasted_iota(jnp.int32, (c, c), 1)
    strict = col < row
    incl = col <= row
    eye = jnp.where(row == col, 1.0, 0.0).astype(F32)
    n_sq = int(math.log2(c)) - 1

    def chunk_body(ci, carry):
        off = pl.multiple_of(ci * c, c)
        rt = rt_ref[pl.ds(off, c), :]
        at = at_ref[pl.ds(off, c), :]
        bt = bt_ref[pl.ds(off, c), :]
        kt = kt_ref[pl.ds(off, c), :]
        vv = v_ref[pl.ds(off, c), :]
        ge = ge_ref[ci]
        ys = []
        for hh in range(2):
            sl = slice(hh * HEAD, (hh + 1) * HEAD)
            h = h_scr[hh]
            ar = jnp.concatenate([at[:, sl], rt[:, sl]], axis=0)
            bk = jnp.concatenate([bt[:, sl], kt[:, sl]], axis=0)
            v = vv[:, sl]
            m = _dot_nt(ar, bk)
            a_ab = jnp.where(strict, m[:c, :c], 0.0)
            a_ak = jnp.where(strict, m[:c, c:], 0.0)
            a_rb = jnp.where(incl, m[c:, :c], 0.0)
            a_rk = jnp.where(incl, m[c:, c:], 0.0)
            pw = a_ab.astype(BF16)
            t = eye + a_ab
            for _ in range(n_sq):
                pw = _dot(pw, pw).astype(BF16)
                t = t + _dot(t.astype(BF16), pw)
            ar_h = _dot(ar, h.astype(BF16))
            av = _dot(jnp.concatenate([a_ak, a_rk], axis=0).astype(BF16), v)
            u = _dot(t.astype(BF16), (ar_h[:c] + av[:c]).astype(BF16)).astype(BF16)
            ys.append(ar_h[c:] + av[c:] + _dot(a_rb.astype(BF16), u))
            dh = _dot_tn(bk, jnp.concatenate([u, v], axis=0))
            g_col = jnp.broadcast_to(ge[:, sl], (HEAD, HEAD)).T
            h_scr[hh] = (h + dh) * g_col
        y_ref[pl.ds(off, c), :] = jnp.concatenate(ys, axis=1)
        return carry

    lax.fori_loop(0, nchunk, chunk_body, 0)

    @pl.when(j == pl.num_programs(2) - 1)
    def _():
        for hh in range(2):
            st_ref[0, hh] = h_scr[hh].T


def _wkv(rt, at, bt, kt, v, ge, s0, *, n_streams, t_len, chunk, nchunk):
    n = rt.shape[0]
    rows = chunk * nchunk
    nj = t_len // rows
    tok = pl.BlockSpec((rows, LANE), lambda b, hp, j: (b * nj + j, hp))
    st = pl.BlockSpec((1, 2, HEAD, HEAD), lambda b, hp, j: (b, hp, 0, 0))
    return pl.pallas_call(
        functools.partial(_wkv_kernel, chunk=chunk, nchunk=nchunk),
        grid=(n_streams, N_HEADS // 2, nj),
        in_specs=[tok, tok, tok, tok, tok,
                  pl.BlockSpec((nchunk, 1, LANE), lambda b, hp, j: (b * nj + j, 0, hp)),
                  st],
        out_specs=[tok, st],
        out_shape=[jax.ShapeDtypeStruct((n, D_A), F32),
                   jax.ShapeDtypeStruct((n_streams, N_HEADS, HEAD, HEAD), F32)],
        scratch_shapes=[pltpu.VMEM((2, HEAD, HEAD), F32)],
        compiler_params=_params(("parallel", "parallel", "arbitrary")),
        name="wkv",
    )(rt, at, bt, kt, v, ge, s0)


CONV_ROWS = 32


def _conv_kernel(u_ref, hown_ref, hinit_ref, cw_ref, cb_ref, lg_ref, lb_ref, o_ref, ext_scr):
    tm = u_ref.shape[0]
    first = pl.program_id(1) == 0

    @pl.when(first)
    def _():
        ext_scr[0:HALO, :] = hinit_ref[0]

    @pl.when(jnp.logical_not(first))
    def _():
        ext_scr[0:HALO, :] = hown_ref[...]

    ext_scr[HALO:HALO + tm, :] = u_ref[...]
    rows = min(CONV_ROWS, tm)
    lead = HALO - (CONV_W - 1)

    for i in range(tm // rows):
        r0 = i * rows
        acc = jnp.broadcast_to(cb_ref[...], (rows, D_B))
        for j in range(CONV_W):
            acc = acc + cw_ref[j:j + 1, :] * ext_scr[r0 + lead + j:r0 + lead + j + rows, :]
        mu = jnp.mean(acc, axis=-1, keepdims=True)
        d = acc - mu
        var = jnp.mean(d * d, axis=-1, keepdims=True)
        zn = d * lax.rsqrt(var + LN_EPS) * lg_ref[...] + lb_ref[...]
        o_ref[r0:r0 + rows, :] = (zn * jax.nn.sigmoid(zn)).astype(BF16)


def _conv(u, hinit, cw, cb, lg, lb, *, n_streams, t_len, tm):
    n = u.shape[0]
    tps = t_len // tm
    if tps > 1:
        blocks_per_tile = tm // HALO
        own_map = lambda b, i: (jnp.maximum((b * tps + i) * blocks_per_tile - 1, 0), 0)
    else:
        own_map = lambda b, i: (0, 0)
    return pl.pallas_call(
        _conv_kernel,
        grid=(n_streams, tps),
        in_specs=[
            pl.BlockSpec((tm, D_B), lambda b, i: (b * tps + i, 0)),
            pl.BlockSpec((HALO, D_B), own_map),
            pl.BlockSpec((1, HALO, D_B), lambda b, i: (b, 0, 0)),
            _const_spec((HALO, D_B)),
            _const_spec((1, D_B)),
            _const_spec((1, D_B)),
            _const_spec((1, D_B)),
        ],
        out_specs=pl.BlockSpec((tm, D_B), lambda b, i: (b * tps + i, 0)),
        out_shape=jax.ShapeDtypeStruct((n, D_B), BF16),
        scratch_shapes=[pltpu.VMEM((HALO + tm, D_B), F32)],
        compiler_params=_params(("parallel", "arbitrary")),
        name="conv",
    )(u, u, hinit, cw, cb, lg, lb)


def _merge_kernel(x_ref, y_ref, bonus_ref, g_ref, ob_ref, n1_ref, wg_ref, bg_ref,
                  lng_ref, lnb_ref, woa_ref, wob_ref, o_ref):
    h = _rms(x_ref[...], n1_ref[...]).astype(BF16)
    gate = jax.nn.sigmoid(_dot(h, wg_ref[...]) + bg_ref[...])
    y = y_ref[...]
    d = y - _head_sum(y) * (1.0 / HEAD)
    var = _head_sum(d * d) * (1.0 / HEAD)
    yn = d * lax.rsqrt(var + GN_EPS) * lng_ref[...] + lnb_ref[...]
    oa = ((yn + bonus_ref[...]) * g_ref[...]).astype(BF16)
    merged = (gate[:, :D_MODEL] * _dot(oa, woa_ref[...])
              + gate[:, D_MODEL:] * _dot(ob_ref[...], wob_ref[...]))
    o_ref[...] = merged.astype(BF16)


def _merge(x, y, bonus, g, ob, n1, wg, bg, lng, lnb, woa, wob):
    n = x.shape[0]
    tok_a = pl.BlockSpec((TM, D_A), lambda i: (i, 0))
    tok_m = pl.BlockSpec((TM, D_MODEL), lambda i: (i, 0))
    return pl.pallas_call(
        _merge_kernel,
        grid=(n // TM,),
        in_specs=[tok_m, tok_a, tok_a, tok_a, tok_a,
                  _const_spec((1, D_MODEL)),
                  _const_spec((D_MODEL, 2 * D_MODEL)),
                  _const_spec((1, 2 * D_MODEL)),
                  _const_spec((1, D_A)),
                  _const_spec((1, D_A)),
                  _const_spec((D_A, D_MODEL)),
                  _const_spec((D_B, D_MODEL))],
        out_specs=tok_m,
        out_shape=jax.ShapeDtypeStruct((n, D_MODEL), BF16),
        compiler_params=_params(("parallel",)),
        name="merge",
    )(x, y, bonus, g, ob, n1, wg, bg, lng, lnb, woa, wob)


def _outproj_kernel(m_ref, x_ref, wo_ref, n2_ref, wrt_ref, brt_ref, x1_ref, h2_ref, lg_ref):
    x1 = x_ref[...] + _dot(m_ref[...], wo_ref[...])
    x1_ref[...] = x1
    h2 = _rms(x1, n2_ref[...])
    h2_ref[...] = h2
    lg_ref[...] = jnp.dot(h2, wrt_ref[...], preferred_element_type=F32,
                          precision=lax.Precision.HIGHEST) + brt_ref[...]


def _outproj(m, x, wo, n2, wrt, brt):
    n = x.shape[0]
    tok_m = pl.BlockSpec((TM, D_MODEL), lambda i: (i, 0))
    return pl.pallas_call(
        _outproj_kernel,
        grid=(n // TM,),
        in_specs=[tok_m, tok_m,
                  _const_spec((D_MODEL, D_MODEL)),
                  _const_spec((1, D_MODEL)),
                  _const_spec((D_MODEL, ROUTE_W)),
                  _const_spec((1, ROUTE_W))],
        out_specs=[tok_m, tok_m, pl.BlockSpec((TM, ROUTE_W), lambda i: (i, 0))],
        out_shape=[jax.ShapeDtypeStruct((n, D_MODEL), F32),
                   jax.ShapeDtypeStruct((n, D_MODEL), F32),
                   jax.ShapeDtypeStruct((n, ROUTE_W), F32)],
        compiler_params=_params(("parallel",)),
        name="outproj",
    )(m, x, wo, n2, wrt, brt)


def _route_kernel(lg_ref, ri_ref, rw_ref, cnt_ref, carry_scr):
    i = pl.program_id(0)
    tm = lg_ref.shape[0]

    @pl.when(i == 0)
    def _():
        carry_scr[...] = jnp.zeros_like(carry_scr)

    lg = lg_ref[...]
    lane = lax.broadcasted_iota(jnp.int32, (tm, ROUTE_W), 1)
    neg = jnp.float32(-jnp.inf)
    big = jnp.int32(ROUTE_W)
    grp = jnp.where(lane < N_GROUPS, lg, neg)
    gmax = jnp.max(grp, axis=1, keepdims=True)
    g_sel = jnp.min(jnp.where(grp == gmax, lane, big), axis=1, keepdims=True)
    p_grp = 1.0 / jnp.sum(jnp.where(lane < N_GROUPS, jnp.exp(lg - gmax), 0.0), axis=1, keepdims=True)
    lo = N_GROUPS + g_sel * EXP_PER_GROUP
    le = jnp.where((lane >= lo) & (lane < lo + EXP_PER_GROUP), lg, neg)
    m1 = jnp.max(le, axis=1, keepdims=True)
    i1 = jnp.min(jnp.where(le == m1, lane, big), axis=1, keepdims=True)
    le2 = jnp.where(lane == i1, neg, le)
    m2 = jnp.max(le2, axis=1, keepdims=True)
    i2 = jnp.min(jnp.where(le2 == m2, lane, big), axis=1, keepdims=True)
    e2 = jnp.exp(m2 - m1)
    wa = p_grp / (1.0 + e2)
    wb = p_grp * e2 / (1.0 + e2)
    ea = i1 - N_GROUPS
    eb = i2 - N_GROUPS
    onehot = jnp.where((lane == ea) | (lane == eb), 1.0, 0.0)
    ri_ = lax.broadcasted_iota(jnp.int32, (tm, tm), 0)
    ci_ = lax.broadcasted_iota(jnp.int32, (tm, tm), 1)
    below = jnp.where(ci_ < ri_, 1.0, 0.0).astype(BF16)
    before = _dot(below, onehot.astype(BF16)) + carry_scr[...]
    rank_a = jnp.sum(jnp.where(lane == ea, before, 0.0), axis=1, keepdims=True).astype(jnp.int32)
    rank_b = jnp.sum(jnp.where(lane == eb, before, 0.0), axis=1, keepdims=True).astype(jnp.int32)
    carry_scr[...] = carry_scr[...] + jnp.sum(onehot, axis=0, keepdims=True)
    ri_ref[...] = jnp.where(lane == 0, ea, jnp.where(lane == 1, eb,
                            jnp.where(lane == 2, rank_a, jnp.where(lane == 3, rank_b, 0))))
    rw_ref[...] = jnp.where(lane == 0, wa, jnp.where(lane == 1, wb, 0.0))
    cnt_ref[...] = jnp.broadcast_to(carry_scr[...], cnt_ref.shape)


def _route(lg):
    n = lg.shape[0]
    tok = pl.BlockSpec((TM, ROUTE_W), lambda i: (i, 0))
    return pl.pallas_call(
        _route_kernel,
        grid=(n // TM,),
        in_specs=[tok],
        out_specs=[tok, tok, pl.BlockSpec((8, ROUTE_W), lambda i: (0, 0))],
        out_shape=[jax.ShapeDtypeStruct((n, ROUTE_W), jnp.int32),
                   jax.ShapeDtypeStruct((n, ROUTE_W), F32),
                   jax.ShapeDtypeStruct((8, ROUTE_W), F32)],
        scratch_shapes=[pltpu.VMEM((1, ROUTE_W), F32)],
        compiler_params=_params(("arbitrary",)),
        name="route",
    )(lg)


def _dispatch_kernel(pos_ref, h_ref, xs_in_ref, xs_ref, sem):
    del xs_in_ref
    tm = h_ref.shape[0]

    def copy(r, slot):
        return pltpu.make_async_copy(h_ref.at[pl.ds(r, 1)],
                                     xs_ref.at[pl.ds(pos_ref[0, slot, r], 1)], sem)

    def start(r, carry):
        copy(r, 0).start()
        copy(r, 1).start()
        return carry

    def wait(r, carry):
        copy(r, 0).wait()
        copy(r, 1).wait()
        return carry

    lax.fori_loop(0, tm, start, 0)
    lax.fori_loop(0, tm, wait, 0)


def _dispatch(pos, h2, xs):
    n = h2.shape[0]
    return pl.pallas_call(
        _dispatch_kernel,
        grid=(n // TM,),
        in_specs=[
            pl.BlockSpec((1, 2, TM), lambda i: (i, 0, 0), memory_space=pltpu.SMEM),
            pl.BlockSpec((TM, D_MODEL), lambda i: (i, 0)),
            pl.BlockSpec(memory_space=pl.ANY),
        ],
        out_specs=pl.BlockSpec(memory_space=pl.ANY),
        out_shape=jax.ShapeDtypeStruct(xs.shape, xs.dtype),
        scratch_shapes=[pltpu.SemaphoreType.DMA(())],
        input_output_aliases={2: 0},
        compiler_params=_params(("arbitrary",)),
        name="dispatch",
    )(pos, h2, xs)


def _experts_kernel(te_ref, tv_ref, xs_ref, wg_ref, wu_ref, wd_ref, y_ref):
    t = pl.program_id(0)

    @pl.when(tv_ref[t] == 1)
    def _():
        x = xs_ref[...].astype(BF16)
        gt = _dot(x, wg_ref[0])
        up = _dot(x, wu_ref[0])
        hid = (gt * jax.nn.sigmoid(gt) * up).astype(BF16)
        y_ref[...] = _dot(hid, wd_ref[0])

    @pl.when(tv_ref[t] == 0)
    def _():
        y_ref[...] = jnp.zeros_like(y_ref)


def _experts(te, tv, xs, wg, wu, wd):
    p = xs.shape[0]
    grid_spec = pltpu.PrefetchScalarGridSpec(
        num_scalar_prefetch=2,
        grid=(p // TMX,),
        in_specs=[
            pl.BlockSpec((TMX, D_MODEL), lambda t, te, tv: (t, 0)),
            pl.BlockSpec((1, D_MODEL, D_EXPERT), lambda t, te, tv: (te[t], 0, 0)),
            pl.BlockSpec((1, D_MODEL, D_EXPERT), lambda t, te, tv: (te[t], 0, 0)),
            pl.BlockSpec((1, D_EXPERT, D_MODEL), lambda t, te, tv: (te[t], 0, 0)),
        ],
        out_specs=pl.BlockSpec((TMX, D_MODEL), lambda t, te, tv: (t, 0)),
    )
    return pl.pallas_call(
        _experts_kernel,
        grid_spec=grid_spec,
        out_shape=jax.ShapeDtypeStruct((p, D_MODEL), F32),
        compiler_params=_params(("arbitrary",)),
        name="experts",
    )(te, tv, xs, wg, wu, wd)


def _combine_kernel(pos_ref, x1_ref, rw_ref, fg_ref, ys_ref, o_ref, ya_scr, yb_scr, sem):
    tm = x1_ref.shape[0]

    def copies(r):
        return (pltpu.make_async_copy(ys_ref.at[pl.ds(pos_ref[0, 0, r], 1)], ya_scr.at[pl.ds(r, 1)], sem),
                pltpu.make_async_copy(ys_ref.at[pl.ds(pos_ref[0, 1, r], 1)], yb_scr.at[pl.ds(r, 1)], sem))

    def start(r, carry):
        ca, cb = copies(r)
        ca.start()
        cb.start()
        return carry

    def wait(r, carry):
        ca, cb = copies(r)
        ca.wait()
        cb.wait()
        return carry

    lax.fori_loop(0, tm, start, 0)
    lax.fori_loop(0, tm, wait, 0)
    rw = rw_ref[...]
    x2 = x1_ref[...] + rw[:, 0:1] * ya_scr[...] + rw[:, 1:2] * yb_scr[...]
    o_ref[...] = _rms(x2, fg_ref[...])


def _combine(pos, x1, rw, fg, ys):
    n = x1.shape[0]
    tok_m = pl.BlockSpec((TM, D_MODEL), lambda i: (i, 0))
    return pl.pallas_call(
        _combine_kernel,
        grid=(n // TM,),
        in_specs=[
            pl.BlockSpec((1, 2, TM), lambda i: (i, 0, 0), memory_space=pltpu.SMEM),
            tok_m,
            pl.BlockSpec((TM, ROUTE_W), lambda i: (i, 0)),
            _const_spec((1, D_MODEL)),
            pl.BlockSpec(memory_space=pl.ANY),
        ],
        out_specs=tok_m,
        out_shape=jax.ShapeDtypeStruct((n, D_MODEL), F32),
        scratch_shapes=[pltpu.VMEM((TM, D_MODEL), F32), pltpu.VMEM((TM, D_MODEL), F32),
                        pltpu.SemaphoreType.DMA(())],
        compiler_params=_params(("arbitrary",)),
        name="combine",
    )(pos, x1, rw, fg, ys)


def _row(v):
    return v.reshape(1, -1).astype(F32)


def _prepare_weights(norm1_g, w_in, b_conv_in, b_gate, mu_shift, w0, w_lora2, a0, a_lora2, g_lora2,
                     k_k, k_a, r_k, lnx_g, lnx_b, w_out_a, conv_w, conv_b, cln_g, cln_b, w_out_b, w_o,
                     norm2_g, w_rg, b_rg, w_re, b_re, w_gate_e, w_up_e, w_down_e, final_g):
    pad_p = P_W - SHIFT_W
    w = {}
    w["n1"] = _row(norm1_g)
    w["wr"] = jnp.pad(w_in[:, :SHIFT_W][:, _PERM], ((0, 0), (0, pad_p))).astype(BF16)
    w["wc"] = w_in[:, SHIFT_W:SHIFT_W + 2 * D_B].astype(BF16)
    w["wg"] = w_in[:, SHIFT_W + 2 * D_B:].astype(BF16)
    w["bc"] = _row(b_conv_in)
    w["bg"] = _row(b_gate)
    w["mu"] = jnp.pad(mu_shift[_PERM], (0, pad_p)).reshape(1, P_W)
    w["vp"] = jnp.concatenate([jnp.stack([w0, a0, k_k, k_a, r_k.reshape(-1)]),
                               jnp.zeros((3, D_A), F32)], axis=0)
    z = jnp.zeros((LORA_W, D_A), F32)
    w2 = jnp.concatenate([jnp.concatenate([w_lora2, z], axis=1),
                          jnp.concatenate([z, a_lora2], axis=1)], axis=0)
    w["w2h"], w["w2l"] = _split(w2)
    g2 = jnp.pad(g_lora2, ((0, 2 * LANE - LORA_G), (0, 0)))
    w["g2h"], w["g2l"] = _split(g2)
    w["lng"] = _row(lnx_g)
    w["lnb"] = _row(lnx_b)
    w["woa"] = w_out_a.astype(BF16)
    w["wob"] = w_out_b.astype(BF16)
    w["cw"] = jnp.pad(conv_w, ((0, HALO - CONV_W), (0, 0)))
    w["cb"] = _row(conv_b)
    w["clg"] = _row(cln_g)
    w["clb"] = _row(cln_b)
    w["wo"] = w_o.astype(BF16)
    w["n2"] = _row(norm2_g)
    pad_r = ROUTE_W - N_GROUPS - N_EXPERTS
    w["wrt"] = jnp.pad(jnp.concatenate([w_rg, w_re], axis=1), ((0, 0), (0, pad_r)))
    w["brt"] = jnp.pad(jnp.concatenate([b_rg, b_re]), (0, pad_r)).reshape(1, ROUTE_W)
    w["wge"] = w_gate_e.astype(BF16)
    w["wue"] = w_up_e.astype(BF16)
    w["wde"] = w_down_e.astype(BF16)
    w["fg"] = _row(final_g)
    return w


def _trunk(x, shift0, wkv0, conv0, w, *, chunk, nchunk, conv_tm):
    b, t, _ = x.shape
    n = b * t
    xf = x.reshape(n, D_MODEL)
    p, u = _inproj(xf, w["n1"], w["wr"], w["wc"], w["bc"])
    shift_p = jnp.pad(shift0[:, _PERM], ((0, 0), (0, P_W - SHIFT_W)))
    if t >= TM:
        tps = t // TM
        nb, step = 1, TM
        tails = p[TM - 1::TM]
        prev = jnp.concatenate([jnp.zeros((1, P_W), F32), tails[:-1]], axis=0)
        is_start = (jnp.arange(n // TM) % tps == 0)[:, None]
        bnd = jnp.where(is_start, jnp.repeat(shift_p, tps, axis=0), prev).reshape(n // TM, 1, P_W)
    else:
        nb, step = TM // t, t
        bnd = shift_p.reshape(n // TM, nb, P_W)
    rt, at, bt, kt, vv, ge, bonus, g = _prep(p, bnd, w["mu"], w["vp"], w["w2h"], w["w2l"],
                                             w["g2h"], w["g2l"], chunk=chunk, nb=nb, step=step)
    y, wkv_new = _wkv(rt, at, bt, kt, vv, ge, wkv0, n_streams=b, t_len=t, chunk=chunk, nchunk=nchunk)
    hinit = jnp.pad(conv0, ((0, 0), (HALO - (CONV_W - 1), 0), (0, 0)))
    ob = _conv(u, hinit, w["cw"], w["cb"], w["clg"], w["clb"], n_streams=b, t_len=t, tm=conv_tm)
    merged = _merge(xf, y, bonus, g, ob, w["n1"], w["wg"], w["bg"], w["lng"], w["lnb"], w["woa"], w["wob"])
    x1, h2, lg = _outproj(merged, xf, w["wo"], w["n2"], w["wrt"], w["brt"])
    new_shift = p.reshape(b, t, P_W)[:, -1, :SHIFT_W][:, _INV_PERM]
    ext_tail = jnp.concatenate([conv0, u.reshape(b, t, D_B)[:, -(CONV_W - 1):]], axis=1)[:, -(CONV_W - 1):]
    return x1, h2, lg, new_shift, wkv_new, ext_tail


def kernel(x_prompt, x_sample, state_shift, state_wkv, cache_conv, norm1_g, w_in, b_conv_in, b_gate,
           mu_shift, w0, w_lora2, a0, a_lora2, g_lora2, k_k, k_a, r_k, lnx_g, lnx_b, w_out_a, conv_w,
           conv_b, cln_g, cln_b, w_out_b, w_o, norm2_g, w_rg, b_rg, w_re, b_re, w_gate_e, w_up_e,
           w_down_e, final_g):
    per_layer = (norm1_g, w_in, b_conv_in, b_gate, mu_shift, w0, w_lora2, a0, a_lora2, g_lora2, k_k, k_a,
                 r_k, lnx_g, lnx_b, w_out_a, conv_w, conv_b, cln_g, cln_b, w_out_b, w_o, norm2_g, w_rg,
                 b_rg, w_re, b_re, w_gate_e, w_up_e, w_down_e)
    assert norm1_g.shape[0] == 1, "single layer only"
    w = _prepare_weights(*[p[0] for p in per_layer], final_g)
    bp, tp, _ = x_prompt.shape
    bs, ts, _ = x_sample.shape
    zero_shift = jnp.zeros((bp, SHIFT_W), F32)
    zero_wkv = jnp.zeros((bp, N_HEADS, HEAD, HEAD), F32)
    zero_conv = jnp.zeros((bp, CONV_W - 1, D_B), F32)
    x1p, h2p, lgp, shift_p, wkv_p, conv_p = _trunk(
        x_prompt, zero_shift, zero_wkv, zero_conv, w, chunk=64, nchunk=min(8, tp // 64), conv_tm=TM)
    x1s, h2s, lgs, shift_s, wkv_s, conv_s = _trunk(
        x_sample, state_shift[0], state_wkv[0], cache_conv[0], w, chunk=ts, nchunk=1, conv_tm=ts)

    n_p, n_s = bp * tp, bs * ts
    n_all = n_p + n_s
    ri, rw, cnt = _route(jnp.concatenate([lgp, lgs], axis=0))
    counts = cnt[0, :N_EXPERTS].astype(jnp.int32)
    tiles_per = (counts + TMX - 1) // TMX
    tile_end = jnp.cumsum(tiles_per)
    offs = (tile_end - tiles_per) * TMX
    n_tiles = (2 * n_all + N_EXPERTS * (TMX - 1) + TMX - 1) // TMX
    t_idx = jnp.arange(n_tiles, dtype=jnp.int32)
    te = jnp.searchsorted(tile_end, t_idx, side="right").astype(jnp.int32)
    tv = (t_idx < tile_end[-1]).astype(jnp.int32)
    last_e = jnp.max(jnp.where(tiles_per > 0, jnp.arange(N_EXPERTS, dtype=jnp.int32), 0))
    te = jnp.where(tv == 1, jnp.minimum(te, N_EXPERTS - 1), last_e)
    pos = jnp.stack([offs[ri[:, 0]] + ri[:, 2], offs[ri[:, 1]] + ri[:, 3]], axis=0)
    pos = pos.reshape(2, n_all // TM, TM).transpose(1, 0, 2)
    xs = jnp.zeros((n_tiles * TMX, D_MODEL), F32)
    xs = _dispatch(pos[:n_p // TM], h2p, xs)
    xs = _dispatch(pos[n_p // TM:], h2s, xs)
    ys = _experts(te, tv, xs, w["wge"], w["wue"], w["wde"])
    yp = _combine(pos[:n_p // TM], x1p, rw[:n_p], w["fg"], ys)
    ysm = _combine(pos[n_p // TM:], x1s, rw[n_p:], w["fg"], ys)
    return (yp.reshape(bp, tp, D_MODEL), ysm.reshape(bs, ts, D_MODEL),
            shift_p[None], wkv_p[None], conv_p[None],
            shift_s[None], wkv_s[None], conv_s[None])
```

```python
import functools
import math

import numpy as np
import jax
import jax.numpy as jnp
from jax import lax
from jax.experimental import pallas as pl
from jax.experimental.pallas import tpu as pltpu

F32 = jnp.float32
BF16 = jnp.bfloat16

D_MODEL = 2048
D_A = 1024
HEAD = 64
N_HEADS = D_A // HEAD
LORA_W = 64
LORA_A = 64
LORA_G = 160
SHIFT_W = 3 * D_A + LORA_W + LORA_A + LORA_G
D_B = 1024
CONV_W = 31
N_GROUPS = 4
EXP_PER_GROUP = 8
N_EXPERTS = N_GROUPS * EXP_PER_GROUP
D_EXPERT = 512
RMS_EPS = 1e-6
LN_EPS = 1e-5
GN_EPS = 64e-5

LANE = 128
SUBLANE = 8
P_W = 27 * LANE
LORA_IN0 = 3 * D_A
XG0 = LORA_IN0 + LANE
HALO = 32
ROUTE_W = LANE
TM = 256
TMX = 256
VMEM_LIMIT = 56 * 1024 * 1024

_PERM = np.concatenate([
    np.arange(0, D_A),
    np.arange(D_A + LORA_W, 2 * D_A + LORA_W),
    np.arange(2 * D_A + LORA_W, 3 * D_A + LORA_W),
    np.arange(D_A, D_A + LORA_W),
    np.arange(3 * D_A + LORA_W, SHIFT_W),
])
_INV_PERM = np.argsort(_PERM)


def _dot(a, b):
    return jnp.dot(a, b, preferred_element_type=F32)


def _dot_nt(a, b):
    return lax.dot_general(a, b, (((1,), (1,)), ((), ())), preferred_element_type=F32)


def _dot_tn(a, b):
    return lax.dot_general(a, b, (((0,), (0,)), ((), ())), preferred_element_type=F32)


def _split(x):
    hi = x.astype(BF16)
    lo = (x - hi.astype(F32)).astype(BF16)
    return hi, lo


def _dot3(x, w_hi, w_lo):
    x_hi, x_lo = _split(x)
    return _dot(x_hi, w_hi) + (_dot(x_lo, w_hi) + _dot(x_hi, w_lo))


def _rms(x, g):
    return x * lax.rsqrt(jnp.mean(x * x, axis=-1, keepdims=True) + RMS_EPS) * g


def _head_sum(x):
    r = lax.broadcasted_iota(jnp.int32, (LANE, LANE), 0) // HEAD
    c = lax.broadcasted_iota(jnp.int32, (LANE, LANE), 1) // HEAD
    ones = jnp.where(r == c, 1.0, 0.0).astype(BF16)
    outs = []
    for j in range(x.shape[1] // LANE):
        hi, lo = _split(x[:, j * LANE:(j + 1) * LANE])
        outs.append(_dot(hi, ones) + _dot(lo, ones))
    return jnp.concatenate(outs, axis=1)


def _const_spec(shape):
    nd = len(shape)
    return pl.BlockSpec(shape, lambda *_: (0,) * nd, pipeline_mode=pl.Buffered(1))


def _params(sem):
    return pltpu.CompilerParams(dimension_semantics=sem, vmem_limit_bytes=VMEM_LIMIT)


def _inproj_kernel(x_ref, g_ref, wr_ref, wc_ref, bc_ref, p_ref, u_ref, tail_ref):
    h = _rms(x_ref[...], g_ref[...]).astype(BF16)
    p = _dot(h, wr_ref[...])
    p_ref[...] = p
    tail_ref[0] = p[TM - 1:TM, :]
    c = _dot(h, wc_ref[...]) + bc_ref[...]
    u_ref[...] = c[:, :D_B] * jax.nn.sigmoid(c[:, D_B:])


def _inproj(x, g, wr, wc, bc):
    n = x.shape[0]
    return pl.pallas_call(
        _inproj_kernel,
        grid=(n // TM,),
        in_specs=[
            pl.BlockSpec((TM, D_MODEL), lambda i: (i, 0)),
            _const_spec((1, D_MODEL)),
            _const_spec((D_MODEL, P_W)),
            _const_spec((D_MODEL, 2 * D_B)),
            _const_spec((1, 2 * D_B)),
        ],
        out_specs=[
            pl.BlockSpec((TM, P_W), lambda i: (i, 0)),
            pl.BlockSpec((TM, D_B), lambda i: (i, 0)),
            pl.BlockSpec((1, 1, P_W), lambda i: (i, 0, 0)),
        ],
        out_shape=[
            jax.ShapeDtypeStruct((n, P_W), F32),
            jax.ShapeDtypeStruct((n, D_B), F32),
            jax.ShapeDtypeStruct((n // TM, 1, P_W), F32),
        ],
        compiler_params=_params(("parallel",)),
        name="inproj",
    )(x, g, wr, wc, bc)


def _prep_kernel(p_ref, bnd_ref, mu_ref, vp_ref, w2h_ref, w2l_ref, g2h_ref, g2l_ref,
                 rt_ref, at_ref, bt_ref, kt_ref, v_ref, ge_ref, bonus_ref, g_ref,
                 *, chunk, nb, step):
    tm = p_ref.shape[0]
    p = p_ref[...]
    row = lax.broadcasted_iota(jnp.int32, (tm, 1), 0)
    prev = pltpu.roll(p, 1, axis=0)
    for j in range(nb):
        prev = jnp.where(row == j * step, bnd_ref[0, j:j + 1, :], prev)
    xs = p + (prev - p) * mu_ref[...]
    r = xs[:, 0:D_A]
    k = xs[:, D_A:2 * D_A]
    v = xs[:, 2 * D_A:3 * D_A]
    lin = xs[:, LORA_IN0:LORA_IN0 + LANE]
    lane = lax.broadcasted_iota(jnp.int32, (1, LANE), 1)
    lin = jnp.where(lane < LORA_W, jnp.tanh(lin), lin)
    lo = _dot3(lin, w2h_ref[...], w2l_ref[...])
    w0 = vp_ref[0:1, :]
    a0 = vp_ref[1:2, :]
    k_k = vp_ref[2:3, :]
    k_a = vp_ref[3:4, :]
    r_k = vp_ref[4:5, :]
    zw = -(w0 + lo[:, :D_A])
    softplus = jnp.maximum(zw, 0.0) + jnp.log1p(jnp.exp(-jnp.abs(zw)))
    logw = -jnp.exp(-softplus - 0.5)
    a = jax.nn.sigmoid(a0 + lo[:, D_A:])
    g = _dot3(jax.nn.sigmoid(xs[:, XG0:XG0 + 2 * LANE]), g2h_ref[...], g2l_ref[...])
    kku = k * k_k
    kk = kku / jnp.maximum(jnp.sqrt(_head_sum(kku * kku)), 1e-12)
    k2 = k * (1.0 + (a - 1.0) * k_a)
    bonus_ref[...] = _head_sum(r * k2 * r_k) * v
    g_ref[...] = g
    ri = lax.broadcasted_iota(jnp.int32, (tm, tm), 0)
    ci = lax.broadcasted_iota(jnp.int32, (tm, tm), 1)
    tri = jnp.where((ri // chunk == ci // chunk) & (ci <= ri), 1.0, 0.0).astype(BF16)
    lw_hi, lw_lo = _split(logw)
    cum = _dot(tri, lw_hi) + _dot(tri, lw_lo)
    eg = jnp.exp(cum)
    egi = jnp.exp(-cum)
    rt_ref[...] = (r * eg).astype(BF16)
    at_ref[...] = (-kk * jnp.exp(cum - logw)).astype(BF16)
    bt_ref[...] = (kk * a * egi).astype(BF16)
    kt_ref[...] = (k2 * egi).astype(BF16)
    v_ref[...] = v.astype(BF16)
    for c in range(tm // chunk):
        ge_ref[c] = eg[(c + 1) * chunk - 1:(c + 1) * chunk, :]


def _prep(p, bnd, mu, vp, w2h, w2l, g2h, g2l, *, chunk, nb, step):
    n = p.shape[0]
    tok = pl.BlockSpec((TM, D_A), lambda i: (i, 0))
    return pl.pallas_call(
        functools.partial(_prep_kernel, chunk=chunk, nb=nb, step=step),
        grid=(n // TM,),
        in_specs=[
            pl.BlockSpec((TM, P_W), lambda i: (i, 0)),
            pl.BlockSpec((1, nb, P_W), lambda i: (i, 0, 0)),
            _const_spec((1, P_W)),
            _const_spec((8, D_A)),
            _const_spec((LANE, 2 * D_A)),
            _const_spec((LANE, 2 * D_A)),
            _const_spec((2 * LANE, D_A)),
            _const_spec((2 * LANE, D_A)),
        ],
        out_specs=[tok, tok, tok, tok, tok,
                   pl.BlockSpec((TM // chunk, 1, D_A), lambda i: (i, 0, 0)),
                   tok, tok],
        out_shape=[jax.ShapeDtypeStruct((n, D_A), BF16)] * 5 + [
            jax.ShapeDtypeStruct((n // chunk, 1, D_A), F32),
            jax.ShapeDtypeStruct((n, D_A), F32),
            jax.ShapeDtypeStruct((n, D_A), F32),
        ],
        compiler_params=_params(("parallel",)),
        name="rwkv_prep",
    )(p, bnd, mu, vp, w2h, w2l, g2h, g2l)


def _wkv_intra_kernel(rt_ref, at_ref, bt_ref, kt_ref, v_ref, ge_ref, rc_ref, yc_ref, m2_ref, n2_ref,
                      *, chunk, nchunk):
    c = chunk
    row = lax.broadcasted_iota(jnp.int32, (c, c), 0)
    col = lax.broadcasted_iota(jnp.int32, (c, c), 1)
    strict = col < row
    incl = col <= row
    eye_c = jnp.where(row == col, 1.0, 0.0).astype(F32)
    rk = lax.broadcasted_iota(jnp.int32, (HEAD, HEAD), 0)
    ck = lax.broadcasted_iota(jnp.int32, (HEAD, HEAD), 1)
    eye_k = jnp.where(rk == ck, 1.0, 0.0).astype(F32)
    n_sq = int(math.log2(c)) - 1
    chains = [(ci, hh) for ci in range(nchunk) for hh in range(2)]

    def part(ref, ci, hh):
        return ref[ci * c:(ci + 1) * c, hh * HEAD:(hh + 1) * HEAD]

    a_ = [part(at_ref, *ch) for ch in chains]
    b_ = [part(bt_ref, *ch) for ch in chains]
    k_ = [part(kt_ref, *ch) for ch in chains]
    v_ = [part(v_ref, *ch) for ch in chains]
    m = [_dot_nt(jnp.concatenate([a_[i], part(rt_ref, *ch)], axis=0), jnp.concatenate([b_[i], k_[i]], axis=0))
         for i, ch in enumerate(chains)]
    a_ab = [jnp.where(strict, x[:c, :c], 0.0) for x in m]
    a_rb = [jnp.where(incl, x[c:, :c], 0.0).astype(BF16) for x in m]
    akv = [_dot(jnp.concatenate([jnp.where(strict, x[:c, c:], 0.0), jnp.where(incl, x[c:, c:], 0.0)],
                                axis=0).astype(BF16), v_[i]) for i, x in enumerate(m)]
    pw = [x.astype(BF16) for x in a_ab]
    t = [eye_c + x for x in a_ab]
    for _ in range(n_sq):
        pw = [_dot(x, x).astype(BF16) for x in pw]
        t = [x + _dot(x.astype(BF16), y) for x, y in zip(t, pw)]
    twb = [_dot(t[i].astype(BF16), jnp.concatenate([a_[i], akv[i][:c].astype(BF16)], axis=1)).astype(BF16)
           for i in range(len(chains))]
    rb = [_dot(a_rb[i], twb[i]) for i in range(len(chains))]
    xb = [_dot_tn(b_[i], twb[i]) for i in range(len(chains))]
    kv = [_dot_tn(k_[i], v_[i]) for i in range(len(chains))]
    for ci in range(nchunk):
        rows = slice(ci * c, (ci + 1) * c)
        ge = ge_ref[ci]
        rcs, ycs, m2s, n2s = [], [], [], []
        for hh in range(2):
            i = 2 * ci + hh
            sl = slice(hh * HEAD, (hh + 1) * HEAD)
            rcs.append((part(rt_ref, ci, hh).astype(F32) + rb[i][:, :HEAD]).astype(BF16))
            ycs.append(rb[i][:, HEAD:] + akv[i][c:])
            g_col = jnp.broadcast_to(ge[:, sl], (HEAD, HEAD)).T
            m2s.append((g_col * (eye_k + xb[i][:, :HEAD]) - eye_k).astype(BF16))
            n2s.append(g_col * (xb[i][:, HEAD:] + kv[i]))
        rc_ref[rows, :] = jnp.concatenate(rcs, axis=1)
        yc_ref[rows, :] = jnp.concatenate(ycs, axis=1)
        m2_ref[ci] = jnp.concatenate(m2s, axis=1)
        n2_ref[ci] = jnp.concatenate(n2s, axis=1)


def _wkv_intra(rt, at, bt, kt, v, ge, *, chunk, nchunk):
    n = rt.shape[0]
    rows = chunk * nchunk
    tok = pl.BlockSpec((rows, LANE), lambda i, hp: (i, hp))
    mat = pl.BlockSpec((nchunk, HEAD, LANE), lambda i, hp: (i, 0, hp))
    return pl.pallas_call(
        functools.partial(_wkv_intra_kernel, chunk=chunk, nchunk=nchunk),
        grid=(n // rows, N_HEADS // 2),
        in_specs=[tok, tok, tok, tok, tok,
                  pl.BlockSpec((nchunk, 1, LANE), lambda i, hp: (i, 0, hp))],
        out_specs=[tok, tok, mat, mat],
        out_shape=[jax.ShapeDtypeStruct((n, D_A), BF16),
                   jax.ShapeDtypeStruct((n, D_A), F32),
                   jax.ShapeDtypeStruct((n // chunk, HEAD, D_A), BF16),
                   jax.ShapeDtypeStruct((n // chunk, HEAD, D_A), F32)],
        compiler_params=_params(("parallel", "parallel")),
        name="wkv_intra",
    )(rt, at, bt, kt, v, ge)


def _wkv_scan_kernel(rc_ref, yc_ref, m2_ref, n2_ref, s0_ref, y_ref, st_ref, h_scr, *, chunk, nchunk):
    j = pl.program_id(1)

    @pl.when(j == 0)
    def _():
        for h in range(N_HEADS):
            h_scr[:, h * HEAD:(h + 1) * HEAD] = s0_ref[0, h].T

    c = chunk

    def chunk_body(ci, carry):
        off = pl.multiple_of(ci * c, c)
        h_all = h_scr[...]
        hb = h_all.astype(BF16)
        rc = rc_ref[pl.ds(off, c), :]
        m2 = m2_ref[ci]
        ys, mh = [], []
        for h in range(N_HEADS):
            sl = slice(h * HEAD, (h + 1) * HEAD)
            ys.append(_dot(rc[:, sl], hb[:, sl]))
            mh.append(_dot(m2[:, sl], hb[:, sl]))
        y_ref[pl.ds(off, c), :] = jnp.concatenate(ys, axis=1) + yc_ref[pl.ds(off, c), :]
        h_scr[...] = h_all + jnp.concatenate(mh, axis=1) + n2_ref[ci]
        return carry

    lax.fori_loop(0, nchunk, chunk_body, 0)

    @pl.when(j == pl.num_programs(1) - 1)
    def _():
        for h in range(N_HEADS):
            st_ref[0, h] = h_scr[:, h * HEAD:(h + 1) * HEAD].T


def _wkv_scan(rc, yc, m2, n2, s0, *, n_streams, t_len, chunk, nchunk):
    n = rc.shape[0]
    rows = chunk * nchunk
    nj = t_len // rows
    tok = pl.BlockSpec((rows, D_A), lambda b, j: (b * nj + j, 0))
    mat = pl.BlockSpec((nchunk, HEAD, D_A), lambda b, j: (b * nj + j, 0, 0))
    st = pl.BlockSpec((1, N_HEADS, HEAD, HEAD), lambda b, j: (b, 0, 0, 0))
    return pl.pallas_call(
        functools.partial(_wkv_scan_kernel, chunk=chunk, nchunk=nchunk),
        grid=(n_streams, nj),
        in_specs=[tok, tok, mat, mat, st],
        out_specs=[tok, st],
        out_shape=[jax.ShapeDtypeStruct((n, D_A), F32),
                   jax.ShapeDtypeStruct((n_streams, N_HEADS, HEAD, HEAD), F32)],
        scratch_shapes=[pltpu.VMEM((HEAD, D_A), F32)],
        compiler_params=_params(("parallel", "arbitrary")),
        name="wkv_scan",
    )(rc, yc, m2, n2, s0)


CONV_ROWS = 32


def _conv_kernel(u_ref, hown_ref, hinit_ref, cw_ref, cb_ref, lg_ref, lb_ref, o_ref, ext_scr, sh_scr):
    tm = u_ref.shape[0]
    first = pl.program_id(1) == 0

    @pl.when(first)
    def _():
        ext_scr[0:HALO, :] = hinit_ref[0]

    @pl.when(jnp.logical_not(first))
    def _():
        ext_scr[0:HALO, :] = hown_ref[...]

    ext_scr[HALO:HALO + tm, :] = u_ref[...]
    span = sh_scr.shape[1]
    for s in range(1, SUBLANE):
        sh_scr[s - 1] = ext_scr[s:s + span, :]
    rows = min(CONV_ROWS, tm)
    lead = HALO - (CONV_W - 1)

    def body(i, carry):
        r0 = pl.multiple_of(i * rows, rows)
        acc = jnp.broadcast_to(cb_ref[...], (rows, D_B))
        for j in range(CONV_W):
            s, q = (lead + j) % SUBLANE, (lead + j) // SUBLANE
            start = pl.multiple_of(r0 + q * SUBLANE, SUBLANE)
            win = ext_scr[pl.ds(start, rows), :] if s == 0 else sh_scr[s - 1, pl.ds(start, rows), :]
            acc = acc + cw_ref[j:j + 1, :] * win
        mu = jnp.mean(acc, axis=-1, keepdims=True)
        d = acc - mu
        var = jnp.mean(d * d, axis=-1, keepdims=True)
        zn = d * lax.rsqrt(var + LN_EPS) * lg_ref[...] + lb_ref[...]
        o_ref[pl.ds(r0, rows), :] = (zn * jax.nn.sigmoid(zn)).astype(BF16)
        return carry

    lax.fori_loop(0, tm // rows, body, 0)


def _conv(u, hinit, cw, cb, lg, lb, *, n_streams, t_len, tm):
    n = u.shape[0]
    tps = t_len // tm
    if tps > 1:
        blocks_per_tile = tm // HALO
        own_map = lambda b, i: (jnp.maximum((b * tps + i) * blocks_per_tile - 1, 0), 0)
    else:
        own_map = lambda b, i: (0, 0)
    return pl.pallas_call(
        _conv_kernel,
        grid=(n_streams, tps),
        in_specs=[
            pl.BlockSpec((tm, D_B), lambda b, i: (b * tps + i, 0)),
            pl.BlockSpec((HALO, D_B), own_map),
            pl.BlockSpec((1, HALO, D_B), lambda b, i: (b, 0, 0)),
            _const_spec((HALO, D_B)),
            _const_spec((1, D_B)),
            _const_spec((1, D_B)),
            _const_spec((1, D_B)),
        ],
        out_specs=pl.BlockSpec((tm, D_B), lambda b, i: (b * tps + i, 0)),
        out_shape=jax.ShapeDtypeStruct((n, D_B), BF16),
        scratch_shapes=[pltpu.VMEM((HALO + tm, D_B), F32),
                        pltpu.VMEM((SUBLANE - 1, HALO + tm - SUBLANE, D_B), F32)],
        compiler_params=_params(("parallel", "arbitrary")),
        name="conv",
    )(u, u, hinit, cw, cb, lg, lb)


def _merge_kernel(x_ref, y_ref, bonus_ref, g_ref, ob_ref, n1_ref, wg_ref, bg_ref,
                  lng_ref, lnb_ref, woa_ref, wob_ref, o_ref):
    h = _rms(x_ref[...], n1_ref[...]).astype(BF16)
    gate = jax.nn.sigmoid(_dot(h, wg_ref[...]) + bg_ref[...])
    y = y_ref[...]
    d = y - _head_sum(y) * (1.0 / HEAD)
    var = _head_sum(d * d) * (1.0 / HEAD)
    yn = d * lax.rsqrt(var + GN_EPS) * lng_ref[...] + lnb_ref[...]
    oa = ((yn + bonus_ref[...]) * g_ref[...]).astype(BF16)
    merged = (gate[:, :D_MODEL] * _dot(oa, woa_ref[...])
              + gate[:, D_MODEL:] * _dot(ob_ref[...], wob_ref[...]))
    o_ref[...] = merged.astype(BF16)


def _merge(x, y, bonus, g, ob, n1, wg, bg, lng, lnb, woa, wob):
    n = x.shape[0]
    tok_a = pl.BlockSpec((TM, D_A), lambda i: (i, 0))
    tok_m = pl.BlockSpec((TM, D_MODEL), lambda i: (i, 0))
    return pl.pallas_call(
        _merge_kernel,
        grid=(n // TM,),
        in_specs=[tok_m, tok_a, tok_a, tok_a, tok_a,
                  _const_spec((1, D_MODEL)),
                  _const_spec((D_MODEL, 2 * D_MODEL)),
                  _const_spec((1, 2 * D_MODEL)),
                  _const_spec((1, D_A)),
                  _const_spec((1, D_A)),
                  _const_spec((D_A, D_MODEL)),
                  _const_spec((D_B, D_MODEL))],
        out_specs=tok_m,
        out_shape=jax.ShapeDtypeStruct((n, D_MODEL), BF16),
        compiler_params=_params(("parallel",)),
        name="merge",
    )(x, y, bonus, g, ob, n1, wg, bg, lng, lnb, woa, wob)


def _pack_bf16_pairs(x):
    w = x.shape[1] // 2
    bits = pltpu.bitcast(x.astype(BF16).astype(F32), jnp.uint32)
    return (bits[:, w:] & jnp.uint32(0xFFFF0000)) | (bits[:, :w] >> jnp.uint32(16))


def _unpack_bf16_pairs(u):
    lo = pltpu.bitcast(u << jnp.uint32(16), F32)
    hi = pltpu.bitcast(u & jnp.uint32(0xFFFF0000), F32)
    return jnp.concatenate([lo, hi], axis=1).astype(BF16)


def _outproj_kernel(m_ref, x_ref, wo_ref, n2_ref, wrh_ref, wrl_ref, brt_ref, x1_ref, h2_ref, lg_ref):
    x1 = x_ref[...] + _dot(m_ref[...], wo_ref[...])
    x1_ref[...] = x1
    h2 = _rms(x1, n2_ref[...])
    h2_ref[...] = _pack_bf16_pairs(h2)
    lg_ref[...] = _dot3(h2, wrh_ref[...], wrl_ref[...]) + brt_ref[...]


def _outproj(m, x, wo, n2, wrh, wrl, brt):
    n = x.shape[0]
    tok_m = pl.BlockSpec((TM, D_MODEL), lambda i: (i, 0))
    return pl.pallas_call(
        _outproj_kernel,
        grid=(n // TM,),
        in_specs=[tok_m, tok_m,
                  _const_spec((D_MODEL, D_MODEL)),
                  _const_spec((1, D_MODEL)),
                  _const_spec((D_MODEL, ROUTE_W)),
                  _const_spec((D_MODEL, ROUTE_W)),
                  _const_spec((1, ROUTE_W))],
        out_specs=[tok_m, pl.BlockSpec((TM, D_MODEL // 2), lambda i: (i, 0)),
                   pl.BlockSpec((TM, ROUTE_W), lambda i: (i, 0))],
        out_shape=[jax.ShapeDtypeStruct((n, D_MODEL), F32),
                   jax.ShapeDtypeStruct((n, D_MODEL // 2), jnp.uint32),
                   jax.ShapeDtypeStruct((n, ROUTE_W), F32)],
        compiler_params=_params(("parallel",)),
        name="outproj",
    )(m, x, wo, n2, wrh, wrl, brt)


def _route_kernel(lg_ref, ri_ref, rw_ref, cnt_ref, carry_scr):
    i = pl.program_id(0)
    tm = lg_ref.shape[0]

    @pl.when(i == 0)
    def _():
        carry_scr[...] = jnp.zeros_like(carry_scr)

    lg = lg_ref[...]
    lane = lax.broadcasted_iota(jnp.int32, (tm, ROUTE_W), 1)
    neg = jnp.float32(-jnp.inf)
    big = jnp.int32(ROUTE_W)
    grp = jnp.where(lane < N_GROUPS, lg, neg)
    gmax = jnp.max(grp, axis=1, keepdims=True)
    g_sel = jnp.min(jnp.where(grp == gmax, lane, big), axis=1, keepdims=True)
    p_grp = 1.0 / jnp.sum(jnp.where(lane < N_GROUPS, jnp.exp(lg - gmax), 0.0), axis=1, keepdims=True)
    lo = N_GROUPS + g_sel * EXP_PER_GROUP
    le = jnp.where((lane >= lo) & (lane < lo + EXP_PER_GROUP), lg, neg)
    m1 = jnp.max(le, axis=1, keepdims=True)
    i1 = jnp.min(jnp.where(le == m1, lane, big), axis=1, keepdims=True)
    le2 = jnp.where(lane == i1, neg, le)
    m2 = jnp.max(le2, axis=1, keepdims=True)
    i2 = jnp.min(jnp.where(le2 == m2, lane, big), axis=1, keepdims=True)
    e2 = jnp.exp(m2 - m1)
    wa = p_grp / (1.0 + e2)
    wb = p_grp * e2 / (1.0 + e2)
    ea = i1 - N_GROUPS
    eb = i2 - N_GROUPS
    onehot = jnp.where((lane == ea) | (lane == eb), 1.0, 0.0)
    ri_ = lax.broadcasted_iota(jnp.int32, (tm, tm), 0)
    ci_ = lax.broadcasted_iota(jnp.int32, (tm, tm), 1)
    below = jnp.where(ci_ < ri_, 1.0, 0.0).astype(BF16)
    before = _dot(below, onehot.astype(BF16)) + carry_scr[...]
    rank_a = jnp.sum(jnp.where(lane == ea, before, 0.0), axis=1, keepdims=True).astype(jnp.int32)
    rank_b = jnp.sum(jnp.where(lane == eb, before, 0.0), axis=1, keepdims=True).astype(jnp.int32)
    carry_scr[...] = carry_scr[...] + jnp.sum(onehot, axis=0, keepdims=True)
    ri = jnp.where(lane == 0, ea, jnp.where(lane == 1, eb,
                   jnp.where(lane == 2, rank_a, jnp.where(lane == 3, rank_b, 0))))
    ri_ref[0] = ri.T[0:SUBLANE, :]
    rw_ref[...] = jnp.where(lane == 0, wa, jnp.where(lane == 1, wb, 0.0))
    cnt_ref[...] = jnp.broadcast_to(carry_scr[...], cnt_ref.shape)


def _route(lg):
    n = lg.shape[0]
    tok = pl.BlockSpec((TM, ROUTE_W), lambda i: (i, 0))
    return pl.pallas_call(
        _route_kernel,
        grid=(n // TM,),
        in_specs=[tok],
        out_specs=[pl.BlockSpec((1, SUBLANE, TM), lambda i: (i, 0, 0)), tok,
                   pl.BlockSpec((8, ROUTE_W), lambda i: (0, 0))],
        out_shape=[jax.ShapeDtypeStruct((n // TM, SUBLANE, TM), jnp.int32),
                   jax.ShapeDtypeStruct((n, ROUTE_W), F32),
                   jax.ShapeDtypeStruct((8, ROUTE_W), F32)],
        scratch_shapes=[pltpu.VMEM((1, ROUTE_W), F32)],
        compiler_params=_params(("arbitrary",)),
        name="route",
    )(lg)


def _dispatch_kernel(pos_ref, h_ref, xs_in_ref, xs_ref, sem):
    del xs_in_ref
    tm = h_ref.shape[0]

    def copy(r, slot):
        return pltpu.make_async_copy(h_ref.at[pl.ds(r, 1)],
                                     xs_ref.at[pl.ds(pos_ref[0, slot, r], 1)], sem)

    def start(r, carry):
        copy(r, 0).start(priority=0)
        copy(r, 1).start(priority=1)
        return carry

    def wait(r, carry):
        copy(r, 0).wait()
        copy(r, 1).wait()
        return carry

    lax.fori_loop(0, tm, start, 0)
    lax.fori_loop(0, tm, wait, 0)


def _dispatch(pos, h2, xs):
    n = h2.shape[0]
    return pl.pallas_call(
        _dispatch_kernel,
        grid=(n // TM,),
        in_specs=[
            pl.BlockSpec((1, 2, TM), lambda i: (i, 0, 0), memory_space=pltpu.SMEM),
            pl.BlockSpec((TM, D_MODEL // 2), lambda i: (i, 0)),
            pl.BlockSpec(memory_space=pl.ANY),
        ],
        out_specs=pl.BlockSpec(memory_space=pl.ANY),
        out_shape=jax.ShapeDtypeStruct(xs.shape, xs.dtype),
        scratch_shapes=[pltpu.SemaphoreType.DMA(())],
        input_output_aliases={2: 0},
        compiler_params=_params(("arbitrary",)),
        name="dispatch",
    )(pos, h2, xs)


def _experts_kernel(te_ref, tv_ref, xs_ref, wg_ref, wu_ref, wd_ref, y_ref):
    t = pl.program_id(0)

    @pl.when(tv_ref[t] == 1)
    def _():
        x = _unpack_bf16_pairs(xs_ref[...])
        gt = _dot(x, wg_ref[0])
        up = _dot(x, wu_ref[0])
        hid = (gt * jax.nn.sigmoid(gt) * up).astype(BF16)
        y_ref[...] = _pack_bf16_pairs(_dot(hid, wd_ref[0]))

    @pl.when(tv_ref[t] == 0)
    def _():
        y_ref[...] = jnp.zeros_like(y_ref)


def _experts(te, tv, xs, wg, wu, wd):
    p = xs.shape[0]
    grid_spec = pltpu.PrefetchScalarGridSpec(
        num_scalar_prefetch=2,
        grid=(p // TMX,),
        in_specs=[
            pl.BlockSpec((TMX, D_MODEL // 2), lambda t, te, tv: (t, 0)),
            pl.BlockSpec((1, D_MODEL, D_EXPERT), lambda t, te, tv: (te[t], 0, 0)),
            pl.BlockSpec((1, D_MODEL, D_EXPERT), lambda t, te, tv: (te[t], 0, 0)),
            pl.BlockSpec((1, D_EXPERT, D_MODEL), lambda t, te, tv: (te[t], 0, 0)),
        ],
        out_specs=pl.BlockSpec((TMX, D_MODEL // 2), lambda t, te, tv: (t, 0)),
    )
    return pl.pallas_call(
        _experts_kernel,
        grid_spec=grid_spec,
        out_shape=jax.ShapeDtypeStruct((p, D_MODEL // 2), jnp.uint32),
        compiler_params=_params(("arbitrary",)),
        name="experts",
    )(te, tv, xs, wg, wu, wd)


def _combine_kernel(pos_ref, x1_ref, rw_ref, fg_ref, ys_ref, o_ref, ya_scr, yb_scr, sem):
    tm = x1_ref.shape[0]

    def copies(r):
        return (pltpu.make_async_copy(ys_ref.at[pl.ds(pos_ref[0, 0, r], 1)], ya_scr.at[pl.ds(r, 1)], sem),
                pltpu.make_async_copy(ys_ref.at[pl.ds(pos_ref[0, 1, r], 1)], yb_scr.at[pl.ds(r, 1)], sem))

    def start(r, carry):
        ca, cb = copies(r)
        ca.start(priority=0)
        cb.start(priority=1)
        return carry

    def wait(r, carry):
        ca, cb = copies(r)
        ca.wait()
        cb.wait()
        return carry

    lax.fori_loop(0, tm, start, 0)
    lax.fori_loop(0, tm, wait, 0)
    rw = rw_ref[...]
    ya = _unpack_bf16_pairs(ya_scr[...]).astype(F32)
    yb = _unpack_bf16_pairs(yb_scr[...]).astype(F32)
    x2 = x1_ref[...] + rw[:, 0:1] * ya + rw[:, 1:2] * yb
    o_ref[...] = _rms(x2, fg_ref[...])


def _combine(pos, x1, rw, fg, ys):
    n = x1.shape[0]
    tok_m = pl.BlockSpec((TM, D_MODEL), lambda i: (i, 0))
    return pl.pallas_call(
        _combine_kernel,
        grid=(n // TM,),
        in_specs=[
            pl.BlockSpec((1, 2, TM), lambda i: (i, 0, 0), memory_space=pltpu.SMEM),
            tok_m,
            pl.BlockSpec((TM, ROUTE_W), lambda i: (i, 0)),
            _const_spec((1, D_MODEL)),
            pl.BlockSpec(memory_space=pl.ANY),
        ],
        out_specs=tok_m,
        out_shape=jax.ShapeDtypeStruct((n, D_MODEL), F32),
        scratch_shapes=[pltpu.VMEM((TM, D_MODEL // 2), jnp.uint32),
                        pltpu.VMEM((TM, D_MODEL // 2), jnp.uint32),
                        pltpu.SemaphoreType.DMA(())],
        compiler_params=_params(("arbitrary",)),
        name="combine",
    )(pos, x1, rw, fg, ys)


def _row(v):
    return v.reshape(1, -1).astype(F32)


def _prepare_weights(norm1_g, w_in, b_conv_in, b_gate, mu_shift, w0, w_lora2, a0, a_lora2, g_lora2,
                     k_k, k_a, r_k, lnx_g, lnx_b, w_out_a, conv_w, conv_b, cln_g, cln_b, w_out_b, w_o,
                     norm2_g, w_rg, b_rg, w_re, b_re, w_gate_e, w_up_e, w_down_e, final_g):
    pad_p = P_W - SHIFT_W
    w = {}
    w["n1"] = _row(norm1_g)
    w["wr"] = jnp.pad(w_in[:, :SHIFT_W][:, _PERM], ((0, 0), (0, pad_p))).astype(BF16)
    w["wc"] = w_in[:, SHIFT_W:SHIFT_W + 2 * D_B].astype(BF16)
    w["wg"] = w_in[:, SHIFT_W + 2 * D_B:].astype(BF16)
    w["bc"] = _row(b_conv_in)
    w["bg"] = _row(b_gate)
    w["mu"] = jnp.pad(mu_shift[_PERM], (0, pad_p)).reshape(1, P_W)
    w["vp"] = jnp.concatenate([jnp.stack([w0, a0, k_k, k_a, r_k.reshape(-1)]),
                               jnp.zeros((3, D_A), F32)], axis=0)
    z = jnp.zeros((LORA_W, D_A), F32)
    w2 = jnp.concatenate([jnp.concatenate([w_lora2, z], axis=1),
                          jnp.concatenate([z, a_lora2], axis=1)], axis=0)
    w["w2h"], w["w2l"] = _split(w2)
    g2 = jnp.pad(g_lora2, ((0, 2 * LANE - LORA_G), (0, 0)))
    w["g2h"], w["g2l"] = _split(g2)
    w["lng"] = _row(lnx_g)
    w["lnb"] = _row(lnx_b)
    w["woa"] = w_out_a.astype(BF16)
    w["wob"] = w_out_b.astype(BF16)
    w["cw"] = jnp.pad(conv_w, ((0, HALO - CONV_W), (0, 0)))
    w["cb"] = _row(conv_b)
    w["clg"] = _row(cln_g)
    w["clb"] = _row(cln_b)
    w["wo"] = w_o.astype(BF16)
    w["n2"] = _row(norm2_g)
    pad_r = ROUTE_W - N_GROUPS - N_EXPERTS
    w["wrh"], w["wrl"] = _split(jnp.pad(jnp.concatenate([w_rg, w_re], axis=1), ((0, 0), (0, pad_r))))
    w["brt"] = jnp.pad(jnp.concatenate([b_rg, b_re]), (0, pad_r)).reshape(1, ROUTE_W)
    w["wge"] = w_gate_e.astype(BF16)
    w["wue"] = w_up_e.astype(BF16)
    w["wde"] = w_down_e.astype(BF16)
    w["fg"] = _row(final_g)
    return w


def _trunk(x, shift0, wkv0, conv0, w, *, chunk, nchunk, intra_chunks, conv_tm):
    b, t, _ = x.shape
    n = b * t
    xf = x.reshape(n, D_MODEL)
    p, u, tails = _inproj(xf, w["n1"], w["wr"], w["wc"], w["bc"])
    shift_p = jnp.pad(shift0[:, _PERM], ((0, 0), (0, P_W - SHIFT_W)))
    if t >= TM:
        tps = t // TM
        nb, step = 1, TM
        prev = jnp.concatenate([jnp.zeros((1, P_W), F32), tails[:-1, 0]], axis=0)
        is_start = (jnp.arange(n // TM) % tps == 0)[:, None]
        bnd = jnp.where(is_start, jnp.repeat(shift_p, tps, axis=0), prev).reshape(n // TM, 1, P_W)
    else:
        nb, step = TM // t, t
        bnd = shift_p.reshape(n // TM, nb, P_W)
    rt, at, bt, kt, vv, ge, bonus, g = _prep(p, bnd, w["mu"], w["vp"], w["w2h"], w["w2l"],
                                             w["g2h"], w["g2l"], chunk=chunk, nb=nb, step=step)
    rc, yc, m2, n2 = _wkv_intra(rt, at, bt, kt, vv, ge, chunk=chunk, nchunk=intra_chunks)
    y, wkv_new = _wkv_scan(rc, yc, m2, n2, wkv0, n_streams=b, t_len=t, chunk=chunk, nchunk=nchunk)
    hinit = jnp.pad(conv0, ((0, 0), (HALO - (CONV_W - 1), 0), (0, 0)))
    ob = _conv(u, hinit, w["cw"], w["cb"], w["clg"], w["clb"], n_streams=b, t_len=t, tm=conv_tm)
    merged = _merge(xf, y, bonus, g, ob, w["n1"], w["wg"], w["bg"], w["lng"], w["lnb"], w["woa"], w["wob"])
    x1, h2, lg = _outproj(merged, xf, w["wo"], w["n2"], w["wrh"], w["wrl"], w["brt"])
    new_shift = p.reshape(b, t, P_W)[:, -1, :SHIFT_W][:, _INV_PERM]
    ext_tail = jnp.concatenate([conv0, u.reshape(b, t, D_B)[:, -(CONV_W - 1):]], axis=1)[:, -(CONV_W - 1):]
    return x1, h2, lg, new_shift, wkv_new, ext_tail


def kernel(x_prompt, x_sample, state_shift, state_wkv, cache_conv, norm1_g, w_in, b_conv_in, b_gate,
           mu_shift, w0, w_lora2, a0, a_lora2, g_lora2, k_k, k_a, r_k, lnx_g, lnx_b, w_out_a, conv_w,
           conv_b, cln_g, cln_b, w_out_b, w_o, norm2_g, w_rg, b_rg, w_re, b_re, w_gate_e, w_up_e,
           w_down_e, final_g):
    per_layer = (norm1_g, w_in, b_conv_in, b_gate, mu_shift, w0, w_lora2, a0, a_lora2, g_lora2, k_k, k_a,
                 r_k, lnx_g, lnx_b, w_out_a, conv_w, conv_b, cln_g, cln_b, w_out_b, w_o, norm2_g, w_rg,
                 b_rg, w_re, b_re, w_gate_e, w_up_e, w_down_e)
    assert norm1_g.shape[0] == 1, "single layer only"
    w = _prepare_weights(*[p[0] for p in per_layer], final_g)
    bp, tp, _ = x_prompt.shape
    bs, ts, _ = x_sample.shape
    zero_shift = jnp.zeros((bp, SHIFT_W), F32)
    zero_wkv = jnp.zeros((bp, N_HEADS, HEAD, HEAD), F32)
    zero_conv = jnp.zeros((bp, CONV_W - 1, D_B), F32)
    x1p, h2p, lgp, shift_p, wkv_p, conv_p = _trunk(
        x_prompt, zero_shift, zero_wkv, zero_conv, w, chunk=64, nchunk=min(8, tp // 64),
        intra_chunks=min(8, tp // 64), conv_tm=TM)
    x1s, h2s, lgs, shift_s, wkv_s, conv_s = _trunk(
        x_sample, state_shift[0], state_wkv[0], cache_conv[0], w, chunk=ts, nchunk=1,
        intra_chunks=TM // ts, conv_tm=ts)

    n_p, n_s = bp * tp, bs * ts
    n_all = n_p + n_s
    ri, rw, cnt = _route(jnp.concatenate([lgp, lgs], axis=0))
    counts = cnt[0, :N_EXPERTS].astype(jnp.int32)
    tiles_per = (counts + TMX - 1) // TMX
    tile_end = jnp.cumsum(tiles_per)
    offs = (tile_end - tiles_per) * TMX
    n_tiles = (2 * n_all + N_EXPERTS * (TMX - 1) + TMX - 1) // TMX
    t_idx = jnp.arange(n_tiles, dtype=jnp.int32)
    te = jnp.sum(t_idx[:, None] >= tile_end[None, :], axis=1).astype(jnp.int32)
    tv = (t_idx < tile_end[-1]).astype(jnp.int32)
    e_ids = jnp.arange(N_EXPERTS, dtype=jnp.int32)
    last_e = jnp.max(jnp.where(tiles_per > 0, e_ids, 0))
    te = jnp.where(tv == 1, jnp.minimum(te, N_EXPERTS - 1), last_e)
    base = jnp.sum(jnp.where(ri[:, 0:2, :, None] == e_ids, offs, 0), axis=-1)
    pos = base + ri[:, 2:4, :]
    xs = jnp.zeros((n_tiles * TMX, D_MODEL // 2), jnp.uint32)
    xs = _dispatch(pos[:n_p // TM], h2p, xs)
    xs = _dispatch(pos[n_p // TM:], h2s, xs)
    ys = _experts(te, tv, xs, w["wge"], w["wue"], w["wde"])
    yp = _combine(pos[:n_p // TM], x1p, rw[:n_p], w["fg"], ys)
    ysm = _combine(pos[n_p // TM:], x1s, rw[n_p:], w["fg"], ys)
    return (yp.reshape(bp, tp, D_MODEL), ysm.reshape(bs, ts, D_MODEL),
            shift_p[None], wkv_p[None], conv_p[None],
            shift_s[None], wkv_s[None], conv_s[None])
```

```python
import functools
import math

import numpy as np
import jax
import jax.numpy as jnp
from jax import lax
from jax.experimental import pallas as pl
from jax.experimental.pallas import tpu as pltpu

F32 = jnp.float32
BF16 = jnp.bfloat16

D_MODEL = 2048
D_A = 1024
HEAD = 64
N_HEADS = D_A // HEAD
LORA_W = 64
LORA_A = 64
LORA_G = 160
SHIFT_W = 3 * D_A + LORA_W + LORA_A + LORA_G
D_B = 1024
CONV_W = 31
N_GROUPS = 4
EXP_PER_GROUP = 8
N_EXPERTS = N_GROUPS * EXP_PER_GROUP
D_EXPERT = 512
RMS_EPS = 1e-6
LN_EPS = 1e-5
GN_EPS = 64e-5

LANE = 128
SUBLANE = 8
P_W = 27 * LANE
LORA_IN0 = 3 * D_A
XG0 = LORA_IN0 + LANE
HALO = 32
ROUTE_W = LANE
TM = 256
TMX = 256
DMA_UNROLL = 8
VMEM_LIMIT = 56 * 1024 * 1024

_PERM = np.concatenate([
    np.arange(0, D_A),
    np.arange(D_A + LORA_W, 2 * D_A + LORA_W),
    np.arange(2 * D_A + LORA_W, 3 * D_A + LORA_W),
    np.arange(D_A, D_A + LORA_W),
    np.arange(3 * D_A + LORA_W, SHIFT_W),
])
_INV_PERM = np.argsort(_PERM)


def _dot(a, b):
    return jnp.dot(a, b, preferred_element_type=F32)


def _dot_nt(a, b):
    return lax.dot_general(a, b, (((1,), (1,)), ((), ())), preferred_element_type=F32)


def _dot_tn(a, b):
    return lax.dot_general(a, b, (((0,), (0,)), ((), ())), preferred_element_type=F32)


def _split(x):
    hi = x.astype(BF16)
    lo = (x - hi.astype(F32)).astype(BF16)
    return hi, lo


def _dot3(x, w_hi, w_lo):
    x_hi, x_lo = _split(x)
    return _dot(x_hi, w_hi) + (_dot(x_lo, w_hi) + _dot(x_hi, w_lo))


def _rms(x, g):
    return x * lax.rsqrt(jnp.mean(x * x, axis=-1, keepdims=True) + RMS_EPS) * g


def _head_sum(x):
    r = lax.broadcasted_iota(jnp.int32, (LANE, LANE), 0) // HEAD
    c = lax.broadcasted_iota(jnp.int32, (LANE, LANE), 1) // HEAD
    ones = jnp.where(r == c, 1.0, 0.0).astype(BF16)
    outs = []
    for j in range(x.shape[1] // LANE):
        hi, lo = _split(x[:, j * LANE:(j + 1) * LANE])
        outs.append(_dot(hi, ones) + _dot(lo, ones))
    return jnp.concatenate(outs, axis=1)


def _const_spec(shape):
    nd = len(shape)
    return pl.BlockSpec(shape, lambda *_: (0,) * nd, pipeline_mode=pl.Buffered(1))


def _params(sem):
    return pltpu.CompilerParams(dimension_semantics=sem, vmem_limit_bytes=VMEM_LIMIT)


def _inproj_kernel(x_ref, g_ref, wr_ref, wc_ref, bc_ref, p_ref, u_ref, tail_ref):
    h = _rms(x_ref[...], g_ref[...]).astype(BF16)
    p = _dot(h, wr_ref[...])
    p_ref[...] = p
    tail_ref[0] = p[TM - 1:TM, :]
    c = _dot(h, wc_ref[...]) + bc_ref[...]
    u_ref[...] = c[:, :D_B] * jax.nn.sigmoid(c[:, D_B:])


def _inproj(x, g, wr, wc, bc):
    n = x.shape[0]
    return pl.pallas_call(
        _inproj_kernel,
        grid=(n // TM,),
        in_specs=[
            pl.BlockSpec((TM, D_MODEL), lambda i: (i, 0)),
            _const_spec((1, D_MODEL)),
            _const_spec((D_MODEL, P_W)),
            _const_spec((D_MODEL, 2 * D_B)),
            _const_spec((1, 2 * D_B)),
        ],
        out_specs=[
            pl.BlockSpec((TM, P_W), lambda i: (i, 0)),
            pl.BlockSpec((TM, D_B), lambda i: (i, 0)),
            pl.BlockSpec((1, 1, P_W), lambda i: (i, 0, 0)),
        ],
        out_shape=[
            jax.ShapeDtypeStruct((n, P_W), F32),
            jax.ShapeDtypeStruct((n, D_B), F32),
            jax.ShapeDtypeStruct((n // TM, 1, P_W), F32),
        ],
        compiler_params=_params(("parallel",)),
        name="inproj",
    )(x, g, wr, wc, bc)


def _prep_kernel(p_ref, bnd_ref, mu_ref, vp_ref, w2h_ref, w2l_ref, g2h_ref, g2l_ref,
                 rt_ref, at_ref, bt_ref, kt_ref, v_ref, ge_ref, bonus_ref, g_ref,
                 *, chunk, nb, step):
    tm = p_ref.shape[0]
    p = p_ref[...]
    row = lax.broadcasted_iota(jnp.int32, (tm, 1), 0)
    prev = pltpu.roll(p, 1, axis=0)
    for j in range(nb):
        prev = jnp.where(row == j * step, bnd_ref[0, j:j + 1, :], prev)
    xs = p + (prev - p) * mu_ref[...]
    r = xs[:, 0:D_A]
    k = xs[:, D_A:2 * D_A]
    v = xs[:, 2 * D_A:3 * D_A]
    lin = xs[:, LORA_IN0:LORA_IN0 + LANE]
    lane = lax.broadcasted_iota(jnp.int32, (1, LANE), 1)
    lin = jnp.where(lane < LORA_W, jnp.tanh(lin), lin)
    lo = _dot3(lin, w2h_ref[...], w2l_ref[...])
    w0 = vp_ref[0:1, :]
    a0 = vp_ref[1:2, :]
    k_k = vp_ref[2:3, :]
    k_a = vp_ref[3:4, :]
    r_k = vp_ref[4:5, :]
    zw = -(w0 + lo[:, :D_A])
    softplus = jnp.maximum(zw, 0.0) + jnp.log1p(jnp.exp(-jnp.abs(zw)))
    logw = -jnp.exp(-softplus - 0.5)
    a = jax.nn.sigmoid(a0 + lo[:, D_A:])
    g = _dot3(jax.nn.sigmoid(xs[:, XG0:XG0 + 2 * LANE]), g2h_ref[...], g2l_ref[...])
    kku = k * k_k
    kk = kku / jnp.maximum(jnp.sqrt(_head_sum(kku * kku)), 1e-12)
    k2 = k * (1.0 + (a - 1.0) * k_a)
    bonus_ref[...] = (_head_sum(r * k2 * r_k) * v).astype(BF16)
    g_ref[...] = g.astype(BF16)
    ri = lax.broadcasted_iota(jnp.int32, (tm, tm), 0)
    ci = lax.broadcasted_iota(jnp.int32, (tm, tm), 1)
    tri = jnp.where((ri // chunk == ci // chunk) & (ci <= ri), 1.0, 0.0).astype(BF16)
    lw_hi, lw_lo = _split(logw)
    cum = _dot(tri, lw_hi) + _dot(tri, lw_lo)
    eg = jnp.exp(cum)
    egi = jnp.exp(-cum)
    rt_ref[...] = (r * eg).astype(BF16)
    at_ref[...] = (-kk * jnp.exp(cum - logw)).astype(BF16)
    bt_ref[...] = (kk * a * egi).astype(BF16)
    kt_ref[...] = (k2 * egi).astype(BF16)
    v_ref[...] = v.astype(BF16)
    for c in range(tm // chunk):
        ge_ref[c] = eg[(c + 1) * chunk - 1:(c + 1) * chunk, :]


def _prep(p, bnd, mu, vp, w2h, w2l, g2h, g2l, *, chunk, nb, step):
    n = p.shape[0]
    tok = pl.BlockSpec((TM, D_A), lambda i: (i, 0))
    return pl.pallas_call(
        functools.partial(_prep_kernel, chunk=chunk, nb=nb, step=step),
        grid=(n // TM,),
        in_specs=[
            pl.BlockSpec((TM, P_W), lambda i: (i, 0)),
            pl.BlockSpec((1, nb, P_W), lambda i: (i, 0, 0)),
            _const_spec((1, P_W)),
            _const_spec((8, D_A)),
            _const_spec((LANE, 2 * D_A)),
            _const_spec((LANE, 2 * D_A)),
            _const_spec((2 * LANE, D_A)),
            _const_spec((2 * LANE, D_A)),
        ],
        out_specs=[tok, tok, tok, tok, tok,
                   pl.BlockSpec((TM // chunk, 1, D_A), lambda i: (i, 0, 0)),
                   tok, tok],
        out_shape=[jax.ShapeDtypeStruct((n, D_A), BF16)] * 5 + [
            jax.ShapeDtypeStruct((n // chunk, 1, D_A), F32),
            jax.ShapeDtypeStruct((n, D_A), BF16),
            jax.ShapeDtypeStruct((n, D_A), BF16),
        ],
        compiler_params=_params(("parallel",)),
        name="rwkv_prep",
    )(p, bnd, mu, vp, w2h, w2l, g2h, g2l)


def _wkv_intra_kernel(rt_ref, at_ref, bt_ref, kt_ref, v_ref, ge_ref, rc_ref, yc_ref, m2_ref, n2_ref,
                      *, chunk, nchunk):
    c = chunk
    w2 = 2 * c
    row = lax.broadcasted_iota(jnp.int32, (c, w2), 0)
    col = lax.broadcasted_iota(jnp.int32, (c, w2), 1)
    colc = jnp.where(col >= c, col - c, col)
    strict = colc < row
    incl = colc <= row
    eye_c = jnp.where(colc == row, 1.0, 0.0).astype(F32)
    tok0 = lax.broadcasted_iota(jnp.int32, (1, w2), 1) < c
    ch0 = lax.broadcasted_iota(jnp.int32, (1, LANE), 1) < HEAD
    rk = lax.broadcasted_iota(jnp.int32, (HEAD, LANE), 0)
    ck = lax.broadcasted_iota(jnp.int32, (HEAD, LANE), 1)
    eye_k = jnp.where(jnp.where(ck >= HEAD, ck - HEAD, ck) == rk, 1.0, 0.0).astype(F32)
    n_sq = int(math.log2(c)) - 1

    def bd(x, first):
        zero = jnp.zeros_like(x)
        return jnp.concatenate([jnp.where(first, x, zero), jnp.where(first, zero, x)], axis=0)

    def head_blocks(x):
        return [jnp.where(ch0, x[:HEAD, j * LANE:(j + 1) * LANE], x[HEAD:, j * LANE:(j + 1) * LANE])
                for j in range(x.shape[1] // LANE)]

    rng = range(nchunk)
    rows = [slice(ci * c, (ci + 1) * c) for ci in rng]
    at = [at_ref[r, :] for r in rows]
    rt = [rt_ref[r, :] for r in rows]
    bt = [bt_ref[r, :] for r in rows]
    kt = [kt_ref[r, :] for r in rows]
    vv = [v_ref[r, :] for r in rows]
    m = [_dot_nt(jnp.concatenate([at[i], rt[i]], axis=0),
                 jnp.concatenate([bd(bt[i], ch0), bd(kt[i], ch0)], axis=0)) for i in rng]
    a_ab = [jnp.where(strict, x[:c, :w2], 0.0) for x in m]
    a_rb = [jnp.where(incl, x[c:, :w2], 0.0).astype(BF16) for x in m]
    akv = [_dot(jnp.concatenate([jnp.where(strict, x[:c, w2:], 0.0), jnp.where(incl, x[c:, w2:], 0.0)],
                                axis=0).astype(BF16), bd(vv[i], ch0)) for i, x in enumerate(m)]
    kv = [head_blocks(_dot_tn(kt[i], vv[i]))[0] for i in rng]
    t = [eye_c + x for x in a_ab]
    pw = [x.astype(BF16) for x in a_ab]
    pw = [_dot(x, bd(x, tok0)).astype(BF16) for x in pw]
    for _ in range(n_sq - 1):
        st = [_dot(jnp.concatenate([t[i].astype(BF16), pw[i]], axis=0), bd(pw[i], tok0)) for i in rng]
        t = [t[i] + st[i][:c] for i in rng]
        pw = [x[c:].astype(BF16) for x in st]
    t = [t[i] + _dot(t[i].astype(BF16), bd(pw[i], tok0)) for i in rng]
    twb = [_dot(t[i].astype(BF16),
                jnp.concatenate([bd(at[i], ch0), bd(akv[i][:c].astype(BF16), ch0)], axis=1)).astype(BF16)
           for i in rng]
    rb = [_dot(a_rb[i], jnp.concatenate([bd(twb[i][:, :LANE], ch0), bd(twb[i][:, LANE:], ch0)], axis=1))
          for i in rng]
    xb = [head_blocks(_dot_tn(bt[i], twb[i])) for i in rng]
    for i in rng:
        rc_ref[rows[i], :] = (rt[i].astype(F32) + rb[i][:, :LANE]).astype(BF16)
        yc_ref[rows[i], :] = rb[i][:, LANE:] + akv[i][c:]
        g_full = jnp.broadcast_to(ge_ref[i], (LANE, LANE)).T
        g_col = jnp.where(ch0, g_full[:HEAD], g_full[HEAD:])
        m2_ref[i] = (g_col * (eye_k + xb[i][0]) - eye_k).astype(BF16)
        n2_ref[i] = g_col * (xb[i][1] + kv[i])


def _wkv_intra(rt, at, bt, kt, v, ge, *, chunk, nchunk):
    n = rt.shape[0]
    rows = chunk * nchunk
    tok = pl.BlockSpec((rows, LANE), lambda i, hp: (i, hp))
    mat = pl.BlockSpec((nchunk, HEAD, LANE), lambda i, hp: (i, 0, hp))
    return pl.pallas_call(
        functools.partial(_wkv_intra_kernel, chunk=chunk, nchunk=nchunk),
        grid=(n // rows, N_HEADS // 2),
        in_specs=[tok, tok, tok, tok, tok,
                  pl.BlockSpec((nchunk, 1, LANE), lambda i, hp: (i, 0, hp))],
        out_specs=[tok, tok, mat, mat],
        out_shape=[jax.ShapeDtypeStruct((n, D_A), BF16),
                   jax.ShapeDtypeStruct((n, D_A), F32),
                   jax.ShapeDtypeStruct((n // chunk, HEAD, D_A), BF16),
                   jax.ShapeDtypeStruct((n // chunk, HEAD, D_A), F32)],
        compiler_params=_params(("parallel", "parallel")),
        name="wkv_intra",
    )(rt, at, bt, kt, v, ge)


def _wkv_scan_kernel(rc_ref, yc_ref, m2_ref, n2_ref, s0_ref, y_ref, st_ref, h_scr, *, chunk, nchunk):
    j = pl.program_id(1)

    @pl.when(j == 0)
    def _():
        for h in range(N_HEADS):
            h_scr[:, h * HEAD:(h + 1) * HEAD] = s0_ref[0, h].T

    c = chunk

    def chunk_body(ci, carry):
        off = pl.multiple_of(ci * c, c)
        h_all = h_scr[...]
        hb = h_all.astype(BF16)
        rc = rc_ref[pl.ds(off, c), :]
        m2 = m2_ref[ci]
        ys, mh = [], []
        for h in range(N_HEADS):
            sl = slice(h * HEAD, (h + 1) * HEAD)
            ys.append(_dot(rc[:, sl], hb[:, sl]))
            mh.append(_dot(m2[:, sl], hb[:, sl]))
        y_ref[pl.ds(off, c), :] = (jnp.concatenate(ys, axis=1) + yc_ref[pl.ds(off, c), :]).astype(BF16)
        h_scr[...] = h_all + jnp.concatenate(mh, axis=1) + n2_ref[ci]
        return carry

    lax.fori_loop(0, nchunk, chunk_body, 0)

    @pl.when(j == pl.num_programs(1) - 1)
    def _():
        for h in range(N_HEADS):
            st_ref[0, h] = h_scr[:, h * HEAD:(h + 1) * HEAD].T


def _wkv_scan(rc, yc, m2, n2, s0, *, n_streams, t_len, chunk, nchunk):
    n = rc.shape[0]
    rows = chunk * nchunk
    nj = t_len // rows
    tok = pl.BlockSpec((rows, D_A), lambda b, j: (b * nj + j, 0))
    mat = pl.BlockSpec((nchunk, HEAD, D_A), lambda b, j: (b * nj + j, 0, 0))
    st = pl.BlockSpec((1, N_HEADS, HEAD, HEAD), lambda b, j: (b, 0, 0, 0))
    return pl.pallas_call(
        functools.partial(_wkv_scan_kernel, chunk=chunk, nchunk=nchunk),
        grid=(n_streams, nj),
        in_specs=[tok, tok, mat, mat, st],
        out_specs=[tok, st],
        out_shape=[jax.ShapeDtypeStruct((n, D_A), BF16),
                   jax.ShapeDtypeStruct((n_streams, N_HEADS, HEAD, HEAD), F32)],
        scratch_shapes=[pltpu.VMEM((HEAD, D_A), F32)],
        compiler_params=_params(("parallel", "arbitrary")),
        name="wkv_scan",
    )(rc, yc, m2, n2, s0)


CONV_ROWS = 128


def _conv_kernel(u_ref, hown_ref, hinit_ref, cw_ref, cb_ref, lg_ref, lb_ref, o_ref, ext_scr, sh_scr, z_scr):
    tm = u_ref.shape[0]
    n_lt = D_B // LANE
    first = pl.program_id(1) == 0

    @pl.when(first)
    def _():
        for l in range(n_lt):
            ext_scr[l, 0:HALO, :] = hinit_ref[0, :, l * LANE:(l + 1) * LANE]

    @pl.when(jnp.logical_not(first))
    def _():
        for l in range(n_lt):
            ext_scr[l, 0:HALO, :] = hown_ref[:, l * LANE:(l + 1) * LANE]

    for l in range(n_lt):
        ext_scr[l, HALO:HALO + tm, :] = u_ref[:, l * LANE:(l + 1) * LANE]
    span = sh_scr.shape[1]
    lead = HALO - (CONV_W - 1)

    def lane_tile(l, carry):
        for s in range(1, SUBLANE):
            sh_scr[s - 1] = ext_scr[l, s:s + span, :]
        acc = jnp.broadcast_to(cb_ref[l], (tm, LANE))
        for j in range(CONV_W):
            s, q = (lead + j) % SUBLANE, (lead + j) // SUBLANE
            r0 = q * SUBLANE
            win = ext_scr[l, r0:r0 + tm, :] if s == 0 else sh_scr[s - 1, r0:r0 + tm, :]
            acc = acc + cw_ref[l, j:j + 1, :] * win
        z_scr[l] = acc
        return carry

    lax.fori_loop(0, n_lt, lane_tile, 0)
    rows = min(CONV_ROWS, tm)

    def norm(i, carry):
        r0 = pl.multiple_of(i * rows, rows)
        z = jnp.concatenate([z_scr[l, pl.ds(r0, rows), :] for l in range(n_lt)], axis=1)
        mu = jnp.mean(z, axis=-1, keepdims=True)
        d = z - mu
        var = jnp.mean(d * d, axis=-1, keepdims=True)
        zn = d * lax.rsqrt(var + LN_EPS) * lg_ref[...] + lb_ref[...]
        o_ref[pl.ds(r0, rows), :] = (zn * jax.nn.sigmoid(zn)).astype(BF16)
        return carry

    lax.fori_loop(0, tm // rows, norm, 0)


def _conv(u, hinit, cw, cb, lg, lb, *, n_streams, t_len, tm):
    n = u.shape[0]
    tps = t_len // tm
    if tps > 1:
        blocks_per_tile = tm // HALO
        own_map = lambda b, i: (jnp.maximum((b * tps + i) * blocks_per_tile - 1, 0), 0)
    else:
        own_map = lambda b, i: (0, 0)
    return pl.pallas_call(
        _conv_kernel,
        grid=(n_streams, tps),
        in_specs=[
            pl.BlockSpec((tm, D_B), lambda b, i: (b * tps + i, 0)),
            pl.BlockSpec((HALO, D_B), own_map),
            pl.BlockSpec((1, HALO, D_B), lambda b, i: (b, 0, 0)),
            _const_spec((D_B // LANE, HALO, LANE)),
            _const_spec((D_B // LANE, 1, LANE)),
            _const_spec((1, D_B)),
            _const_spec((1, D_B)),
        ],
        out_specs=pl.BlockSpec((tm, D_B), lambda b, i: (b * tps + i, 0)),
        out_shape=jax.ShapeDtypeStruct((n, D_B), BF16),
        scratch_shapes=[pltpu.VMEM((D_B // LANE, HALO + tm, LANE), F32),
                        pltpu.VMEM((SUBLANE - 1, HALO + tm - SUBLANE, LANE), F32),
                        pltpu.VMEM((D_B // LANE, tm, LANE), F32)],
        compiler_params=_params(("parallel", "arbitrary")),
        name="conv",
    )(u, u, hinit, cw, cb, lg, lb)


def _merge_kernel(x_ref, y_ref, bonus_ref, g_ref, ob_ref, n1_ref, wg_ref, bg_ref,
                  lng_ref, lnb_ref, woa_ref, wob_ref, o_ref):
    h = _rms(x_ref[...], n1_ref[...]).astype(BF16)
    gate = jax.nn.sigmoid(_dot(h, wg_ref[...]) + bg_ref[...])
    y = y_ref[...].astype(F32)
    d = y - _head_sum(y) * (1.0 / HEAD)
    var = _head_sum(d * d) * (1.0 / HEAD)
    yn = d * lax.rsqrt(var + GN_EPS) * lng_ref[...] + lnb_ref[...]
    oa = ((yn + bonus_ref[...].astype(F32)) * g_ref[...].astype(F32)).astype(BF16)
    merged = (gate[:, :D_MODEL] * _dot(oa, woa_ref[...])
              + gate[:, D_MODEL:] * _dot(ob_ref[...], wob_ref[...]))
    o_ref[...] = merged.astype(BF16)


def _merge(x, y, bonus, g, ob, n1, wg, bg, lng, lnb, woa, wob):
    n = x.shape[0]
    tok_a = pl.BlockSpec((TM, D_A), lambda i: (i, 0))
    tok_m = pl.BlockSpec((TM, D_MODEL), lambda i: (i, 0))
    return pl.pallas_call(
        _merge_kernel,
        grid=(n // TM,),
        in_specs=[tok_m, tok_a, tok_a, tok_a, tok_a,
                  _const_spec((1, D_MODEL)),
                  _const_spec((D_MODEL, 2 * D_MODEL)),
                  _const_spec((1, 2 * D_MODEL)),
                  _const_spec((1, D_A)),
                  _const_spec((1, D_A)),
                  _const_spec((D_A, D_MODEL)),
                  _const_spec((D_B, D_MODEL))],
        out_specs=tok_m,
        out_shape=jax.ShapeDtypeStruct((n, D_MODEL), BF16),
        compiler_params=_params(("parallel",)),
        name="merge",
    )(x, y, bonus, g, ob, n1, wg, bg, lng, lnb, woa, wob)


def _pack_bf16_pairs(x):
    w = x.shape[1] // 2
    bits = pltpu.bitcast(x.astype(BF16).astype(F32), jnp.uint32)
    return (bits[:, w:] & jnp.uint32(0xFFFF0000)) | (bits[:, :w] >> jnp.uint32(16))


def _unpack_bf16_pairs(u):
    lo = pltpu.bitcast(u << jnp.uint32(16), F32)
    hi = pltpu.bitcast(u & jnp.uint32(0xFFFF0000), F32)
    return jnp.concatenate([lo, hi], axis=1).astype(BF16)


def _outproj_kernel(m_ref, x_ref, wo_ref, n2_ref, wrh_ref, wrl_ref, brt_ref, x1_ref, h2_ref, lg_ref):
    x1 = x_ref[...] + _dot(m_ref[...], wo_ref[...])
    x1_ref[...] = x1
    h2 = _rms(x1, n2_ref[...])
    h2_ref[...] = _pack_bf16_pairs(h2)
    lg_ref[...] = _dot3(h2, wrh_ref[...], wrl_ref[...]) + brt_ref[...]


def _outproj(m, x, wo, n2, wrh, wrl, brt):
    n = x.shape[0]
    tok_m = pl.BlockSpec((TM, D_MODEL), lambda i: (i, 0))
    return pl.pallas_call(
        _outproj_kernel,
        grid=(n // TM,),
        in_specs=[tok_m, tok_m,
                  _const_spec((D_MODEL, D_MODEL)),
                  _const_spec((1, D_MODEL)),
                  _const_spec((D_MODEL, ROUTE_W)),
                  _const_spec((D_MODEL, ROUTE_W)),
                  _const_spec((1, ROUTE_W))],
        out_specs=[tok_m, pl.BlockSpec((TM, D_MODEL // 2), lambda i: (i, 0)),
                   pl.BlockSpec((TM, ROUTE_W), lambda i: (i, 0))],
        out_shape=[jax.ShapeDtypeStruct((n, D_MODEL), F32),
                   jax.ShapeDtypeStruct((n, D_MODEL // 2), jnp.uint32),
                   jax.ShapeDtypeStruct((n, ROUTE_W), F32)],
        compiler_params=_params(("parallel",)),
        name="outproj",
    )(m, x, wo, n2, wrh, wrl, brt)


def _route_kernel(lg_ref, ri_ref, rw_ref, cnt_ref, carry_scr):
    i = pl.program_id(0)
    tm = lg_ref.shape[0]

    @pl.when(i == 0)
    def _():
        carry_scr[...] = jnp.zeros_like(carry_scr)

    lg = lg_ref[...]
    lane = lax.broadcasted_iota(jnp.int32, (tm, ROUTE_W), 1)
    neg = jnp.float32(-jnp.inf)
    big = jnp.int32(ROUTE_W)
    grp = jnp.where(lane < N_GROUPS, lg, neg)
    gmax = jnp.max(grp, axis=1, keepdims=True)
    g_sel = jnp.min(jnp.where(grp == gmax, lane, big), axis=1, keepdims=True)
    p_grp = 1.0 / jnp.sum(jnp.where(lane < N_GROUPS, jnp.exp(lg - gmax), 0.0), axis=1, keepdims=True)
    lo = N_GROUPS + g_sel * EXP_PER_GROUP
    le = jnp.where((lane >= lo) & (lane < lo + EXP_PER_GROUP), lg, neg)
    m1 = jnp.max(le, axis=1, keepdims=True)
    i1 = jnp.min(jnp.where(le == m1, lane, big), axis=1, keepdims=True)
    le2 = jnp.where(lane == i1, neg, le)
    m2 = jnp.max(le2, axis=1, keepdims=True)
    i2 = jnp.min(jnp.where(le2 == m2, lane, big), axis=1, keepdims=True)
    e2 = jnp.exp(m2 - m1)
    wa = p_grp / (1.0 + e2)
    wb = p_grp * e2 / (1.0 + e2)
    ea = i1 - N_GROUPS
    eb = i2 - N_GROUPS
    onehot = jnp.where((lane == ea) | (lane == eb), 1.0, 0.0)
    ri_ = lax.broadcasted_iota(jnp.int32, (tm, tm), 0)
    ci_ = lax.broadcasted_iota(jnp.int32, (tm, tm), 1)
    below = jnp.where(ci_ < ri_, 1.0, 0.0).astype(BF16)
    before = _dot(below, onehot.astype(BF16)) + carry_scr[...]
    rank_a = jnp.sum(jnp.where(lane == ea, before, 0.0), axis=1, keepdims=True).astype(jnp.int32)
    rank_b = jnp.sum(jnp.where(lane == eb, before, 0.0), axis=1, keepdims=True).astype(jnp.int32)
    carry_scr[...] = carry_scr[...] + jnp.sum(onehot, axis=0, keepdims=True)
    ri = jnp.where(lane == 0, ea, jnp.where(lane == 1, eb,
                   jnp.where(lane == 2, rank_a, jnp.where(lane == 3, rank_b, 0))))
    ri_ref[0] = ri.T[0:SUBLANE, :]
    rw_ref[...] = jnp.where(lane == 0, wa, jnp.where(lane == 1, wb, 0.0))
    cnt_ref[...] = jnp.broadcast_to(carry_scr[...], cnt_ref.shape)


def _route(lg):
    n = lg.shape[0]
    tok = pl.BlockSpec((TM, ROUTE_W), lambda i: (i, 0))
    return pl.pallas_call(
        _route_kernel,
        grid=(n // TM,),
        in_specs=[tok],
        out_specs=[pl.BlockSpec((1, SUBLANE, TM), lambda i: (i, 0, 0)), tok,
                   pl.BlockSpec((8, ROUTE_W), lambda i: (0, 0))],
        out_shape=[jax.ShapeDtypeStruct((n // TM, SUBLANE, TM), jnp.int32),
                   jax.ShapeDtypeStruct((n, ROUTE_W), F32),
                   jax.ShapeDtypeStruct((8, ROUTE_W), F32)],
        scratch_shapes=[pltpu.VMEM((1, ROUTE_W), F32)],
        compiler_params=_params(("arbitrary",)),
        name="route",
    )(lg)


def _dispatch_kernel(pos_ref, h_ref, xs_in_ref, xs_ref, sem):
    del xs_in_ref
    tm = h_ref.shape[0]

    def copy(r, slot):
        return pltpu.make_async_copy(h_ref.at[pl.ds(r, 1)],
                                     xs_ref.at[pl.ds(pos_ref[0, slot, r], 1)], sem)

    def start(r, carry):
        copy(r, 0).start(priority=0)
        copy(r, 1).start(priority=1)
        return carry

    def wait(r, carry):
        copy(r, 0).wait()
        copy(r, 1).wait()
        return carry

    lax.fori_loop(0, tm, start, 0, unroll=DMA_UNROLL)
    lax.fori_loop(0, tm, wait, 0, unroll=DMA_UNROLL)


def _dispatch(pos, h2, xs):
    n = h2.shape[0]
    return pl.pallas_call(
        _dispatch_kernel,
        grid=(n // TM,),
        in_specs=[
            pl.BlockSpec((1, 2, TM), lambda i: (i, 0, 0), memory_space=pltpu.SMEM),
            pl.BlockSpec((TM, D_MODEL // 2), lambda i: (i, 0)),
            pl.BlockSpec(memory_space=pl.ANY),
        ],
        out_specs=pl.BlockSpec(memory_space=pl.ANY),
        out_shape=jax.ShapeDtypeStruct(xs.shape, xs.dtype),
        scratch_shapes=[pltpu.SemaphoreType.DMA(())],
        input_output_aliases={2: 0},
        compiler_params=_params(("arbitrary",)),
        name="dispatch",
    )(pos, h2, xs)


def _experts_kernel(te_ref, tv_ref, xs_ref, wg_ref, wu_ref, wd_ref, y_ref):
    t = pl.program_id(0)

    @pl.when(tv_ref[t] == 1)
    def _():
        x = _unpack_bf16_pairs(xs_ref[...])
        gt = _dot(x, wg_ref[0])
        up = _dot(x, wu_ref[0])
        hid = (gt * jax.nn.sigmoid(gt) * up).astype(BF16)
        y_ref[...] = _pack_bf16_pairs(_dot(hid, wd_ref[0]))

    @pl.when(tv_ref[t] == 0)
    def _():
        y_ref[...] = jnp.zeros_like(y_ref)


def _experts(te, tv, xs, wg, wu, wd):
    p = xs.shape[0]
    grid_spec = pltpu.PrefetchScalarGridSpec(
        num_scalar_prefetch=2,
        grid=(p // TMX,),
        in_specs=[
            pl.BlockSpec((TMX, D_MODEL // 2), lambda t, te, tv: (t, 0)),
            pl.BlockSpec((1, D_MODEL, D_EXPERT), lambda t, te, tv: (te[t], 0, 0)),
            pl.BlockSpec((1, D_MODEL, D_EXPERT), lambda t, te, tv: (te[t], 0, 0)),
            pl.BlockSpec((1, D_EXPERT, D_MODEL), lambda t, te, tv: (te[t], 0, 0)),
        ],
        out_specs=pl.BlockSpec((TMX, D_MODEL // 2), lambda t, te, tv: (t, 0)),
    )
    return pl.pallas_call(
        _experts_kernel,
        grid_spec=grid_spec,
        out_shape=jax.ShapeDtypeStruct((p, D_MODEL // 2), jnp.uint32),
        compiler_params=_params(("arbitrary",)),
        name="experts",
    )(te, tv, xs, wg, wu, wd)


def _combine_kernel(pos_ref, x1_ref, rw_ref, fg_ref, ys_ref, o_ref, ya_scr, yb_scr, sem):
    tm = x1_ref.shape[0]

    def copies(r):
        return (pltpu.make_async_copy(ys_ref.at[pl.ds(pos_ref[0, 0, r], 1)], ya_scr.at[pl.ds(r, 1)], sem),
                pltpu.make_async_copy(ys_ref.at[pl.ds(pos_ref[0, 1, r], 1)], yb_scr.at[pl.ds(r, 1)], sem))

    def start(r, carry):
        ca, cb = copies(r)
        ca.start(priority=0)
        cb.start(priority=1)
        return carry

    def wait(r, carry):
        ca, cb = copies(r)
        ca.wait()
        cb.wait()
        return carry

    lax.fori_loop(0, tm, start, 0, unroll=DMA_UNROLL)
    lax.fori_loop(0, tm, wait, 0, unroll=DMA_UNROLL)
    rw = rw_ref[...]
    ya = _unpack_bf16_pairs(ya_scr[...]).astype(F32)
    yb = _unpack_bf16_pairs(yb_scr[...]).astype(F32)
    x2 = x1_ref[...] + rw[:, 0:1] * ya + rw[:, 1:2] * yb
    o_ref[...] = _rms(x2, fg_ref[...])


def _combine(pos, x1, rw, fg, ys):
    n = x1.shape[0]
    tok_m = pl.BlockSpec((TM, D_MODEL), lambda i: (i, 0))
    return pl.pallas_call(
        _combine_kernel,
        grid=(n // TM,),
        in_specs=[
            pl.BlockSpec((1, 2, TM), lambda i: (i, 0, 0), memory_space=pltpu.SMEM),
            tok_m,
            pl.BlockSpec((TM, ROUTE_W), lambda i: (i, 0)),
            _const_spec((1, D_MODEL)),
            pl.BlockSpec(memory_space=pl.ANY),
        ],
        out_specs=tok_m,
        out_shape=jax.ShapeDtypeStruct((n, D_MODEL), F32),
        scratch_shapes=[pltpu.VMEM((TM, D_MODEL // 2), jnp.uint32),
                        pltpu.VMEM((TM, D_MODEL // 2), jnp.uint32),
                        pltpu.SemaphoreType.DMA(())],
        compiler_params=_params(("arbitrary",)),
        name="combine",
    )(pos, x1, rw, fg, ys)


def _row(v):
    return v.reshape(1, -1).astype(F32)


def _prepare_weights(norm1_g, w_in, b_conv_in, b_gate, mu_shift, w0, w_lora2, a0, a_lora2, g_lora2,
                     k_k, k_a, r_k, lnx_g, lnx_b, w_out_a, conv_w, conv_b, cln_g, cln_b, w_out_b, w_o,
                     norm2_g, w_rg, b_rg, w_re, b_re, w_gate_e, w_up_e, w_down_e, final_g):
    pad_p = P_W - SHIFT_W
    w = {}
    w["n1"] = _row(norm1_g)
    w["wr"] = jnp.pad(w_in[:, :SHIFT_W][:, _PERM], ((0, 0), (0, pad_p))).astype(BF16)
    w["wc"] = w_in[:, SHIFT_W:SHIFT_W + 2 * D_B].astype(BF16)
    w["wg"] = w_in[:, SHIFT_W + 2 * D_B:].astype(BF16)
    w["bc"] = _row(b_conv_in)
    w["bg"] = _row(b_gate)
    w["mu"] = jnp.pad(mu_shift[_PERM], (0, pad_p)).reshape(1, P_W)
    w["vp"] = jnp.concatenate([jnp.stack([w0, a0, k_k, k_a, r_k.reshape(-1)]),
                               jnp.zeros((3, D_A), F32)], axis=0)
    z = jnp.zeros((LORA_W, D_A), F32)
    w2 = jnp.concatenate([jnp.concatenate([w_lora2, z], axis=1),
                          jnp.concatenate([z, a_lora2], axis=1)], axis=0)
    w["w2h"], w["w2l"] = _split(w2)
    g2 = jnp.pad(g_lora2, ((0, 2 * LANE - LORA_G), (0, 0)))
    w["g2h"], w["g2l"] = _split(g2)
    w["lng"] = _row(lnx_g)
    w["lnb"] = _row(lnx_b)
    w["woa"] = w_out_a.astype(BF16)
    w["wob"] = w_out_b.astype(BF16)
    w["cw"] = jnp.pad(conv_w, ((0, HALO - CONV_W), (0, 0))).reshape(HALO, D_B // LANE, LANE).transpose(1, 0, 2)
    w["cb"] = conv_b.reshape(D_B // LANE, 1, LANE)
    w["clg"] = _row(cln_g)
    w["clb"] = _row(cln_b)
    w["wo"] = w_o.astype(BF16)
    w["n2"] = _row(norm2_g)
    pad_r = ROUTE_W - N_GROUPS - N_EXPERTS
    w["wrh"], w["wrl"] = _split(jnp.pad(jnp.concatenate([w_rg, w_re], axis=1), ((0, 0), (0, pad_r))))
    w["brt"] = jnp.pad(jnp.concatenate([b_rg, b_re]), (0, pad_r)).reshape(1, ROUTE_W)
    w["wge"] = w_gate_e.astype(BF16)
    w["wue"] = w_up_e.astype(BF16)
    w["wde"] = w_down_e.astype(BF16)
    w["fg"] = _row(final_g)
    return w


def _trunk(x, shift0, wkv0, conv0, w, *, chunk, nchunk, intra_chunks, conv_tm):
    b, t, _ = x.shape
    n = b * t
    xf = x.reshape(n, D_MODEL)
    p, u, tails = _inproj(xf, w["n1"], w["wr"], w["wc"], w["bc"])
    shift_p = jnp.pad(shift0[:, _PERM], ((0, 0), (0, P_W - SHIFT_W)))
    if t >= TM:
        tps = t // TM
        nb, step = 1, TM
        prev = jnp.concatenate([jnp.zeros((1, P_W), F32), tails[:-1, 0]], axis=0)
        is_start = (jnp.arange(n // TM) % tps == 0)[:, None]
        bnd = jnp.where(is_start, jnp.repeat(shift_p, tps, axis=0), prev).reshape(n // TM, 1, P_W)
    else:
        nb, step = TM // t, t
        bnd = shift_p.reshape(n // TM, nb, P_W)
    rt, at, bt, kt, vv, ge, bonus, g = _prep(p, bnd, w["mu"], w["vp"], w["w2h"], w["w2l"],
                                             w["g2h"], w["g2l"], chunk=chunk, nb=nb, step=step)
    rc, yc, m2, n2 = _wkv_intra(rt, at, bt, kt, vv, ge, chunk=chunk, nchunk=intra_chunks)
    y, wkv_new = _wkv_scan(rc, yc, m2, n2, wkv0, n_streams=b, t_len=t, chunk=chunk, nchunk=nchunk)
    hinit = jnp.pad(conv0, ((0, 0), (HALO - (CONV_W - 1), 0), (0, 0)))
    ob = _conv(u, hinit, w["cw"], w["cb"], w["clg"], w["clb"], n_streams=b, t_len=t, tm=conv_tm)
    merged = _merge(xf, y, bonus, g, ob, w["n1"], w["wg"], w["bg"], w["lng"], w["lnb"], w["woa"], w["wob"])
    x1, h2, lg = _outproj(merged, xf, w["wo"], w["n2"], w["wrh"], w["wrl"], w["brt"])
    new_shift = p.reshape(b, t, P_W)[:, -1, :SHIFT_W][:, _INV_PERM]
    ext_tail = jnp.concatenate([conv0, u.reshape(b, t, D_B)[:, -(CONV_W - 1):]], axis=1)[:, -(CONV_W - 1):]
    return x1, h2, lg, new_shift, wkv_new, ext_tail


def kernel(x_prompt, x_sample, state_shift, state_wkv, cache_conv, norm1_g, w_in, b_conv_in, b_gate,
           mu_shift, w0, w_lora2, a0, a_lora2, g_lora2, k_k, k_a, r_k, lnx_g, lnx_b, w_out_a, conv_w,
           conv_b, cln_g, cln_b, w_out_b, w_o, norm2_g, w_rg, b_rg, w_re, b_re, w_gate_e, w_up_e,
           w_down_e, final_g):
    per_layer = (norm1_g, w_in, b_conv_in, b_gate, mu_shift, w0, w_lora2, a0, a_lora2, g_lora2, k_k, k_a,
                 r_k, lnx_g, lnx_b, w_out_a, conv_w, conv_b, cln_g, cln_b, w_out_b, w_o, norm2_g, w_rg,
                 b_rg, w_re, b_re, w_gate_e, w_up_e, w_down_e)
    assert norm1_g.shape[0] == 1, "single layer only"
    w = _prepare_weights(*[p[0] for p in per_layer], final_g)
    bp, tp, _ = x_prompt.shape
    bs, ts, _ = x_sample.shape
    zero_shift = jnp.zeros((bp, SHIFT_W), F32)
    zero_wkv = jnp.zeros((bp, N_HEADS, HEAD, HEAD), F32)
    zero_conv = jnp.zeros((bp, CONV_W - 1, D_B), F32)
    x1p, h2p, lgp, shift_p, wkv_p, conv_p = _trunk(
        x_prompt, zero_shift, zero_wkv, zero_conv, w, chunk=64, nchunk=min(8, tp // 64),
        intra_chunks=min(16, tp // 64), conv_tm=TM)
    x1s, h2s, lgs, shift_s, wkv_s, conv_s = _trunk(
        x_sample, state_shift[0], state_wkv[0], cache_conv[0], w, chunk=ts, nchunk=1,
        intra_chunks=TM // ts, conv_tm=ts)

    n_p, n_s = bp * tp, bs * ts
    n_all = n_p + n_s
    ri, rw, cnt = _route(jnp.concatenate([lgp, lgs], axis=0))
    counts = cnt[0, :N_EXPERTS].astype(jnp.int32)
    tiles_per = (counts + TMX - 1) // TMX
    tile_end = jnp.cumsum(tiles_per)
    offs = (tile_end - tiles_per) * TMX
    n_tiles = (2 * n_all + N_EXPERTS * (TMX - 1) + TMX - 1) // TMX
    t_idx = jnp.arange(n_tiles, dtype=jnp.int32)
    te = jnp.sum(t_idx[:, None] >= tile_end[None, :], axis=1).astype(jnp.int32)
    tv = (t_idx < tile_end[-1]).astype(jnp.int32)
    e_ids = jnp.arange(N_EXPERTS, dtype=jnp.int32)
    last_e = jnp.max(jnp.where(tiles_per > 0, e_ids, 0))
    te = jnp.where(tv == 1, jnp.minimum(te, N_EXPERTS - 1), last_e)
    base = jnp.sum(jnp.where(ri[:, 0:2, :, None] == e_ids, offs, 0), axis=-1)
    pos = base + ri[:, 2:4, :]
    xs = jnp.zeros((n_tiles * TMX, D_MODEL // 2), jnp.uint32)
    xs = _dispatch(pos[:n_p // TM], h2p, xs)
    xs = _dispatch(pos[n_p // TM:], h2s, xs)
    ys = _experts(te, tv, xs, w["wge"], w["wue"], w["wde"])
    yp = _combine(pos[:n_p // TM], x1p, rw[:n_p], w["fg"], ys)
    ysm = _combine(pos[n_p // TM:], x1s, rw[n_p:], w["fg"], ys)
    return (yp.reshape(bp, tp, D_MODEL), ysm.reshape(bs, ts, D_MODEL),
            shift_p[None], wkv_p[None], conv_p[None],
            shift_s[None], wkv_s[None], conv_s[None])
```

```python
import functools
import math

import numpy as np
import jax
import jax.numpy as jnp
from jax import lax
from jax.experimental import pallas as pl
from jax.experimental.pallas import tpu as pltpu

F32 = jnp.float32
BF16 = jnp.bfloat16

D_MODEL = 2048
D_A = 1024
HEAD = 64
N_HEADS = D_A // HEAD
LORA_W = 64
LORA_A = 64
LORA_G = 160
SHIFT_W = 3 * D_A + LORA_W + LORA_A + LORA_G
D_B = 1024
CONV_W = 31
N_GROUPS = 4
EXP_PER_GROUP = 8
N_EXPERTS = N_GROUPS * EXP_PER_GROUP
D_EXPERT = 512
RMS_EPS = 1e-6
LN_EPS = 1e-5
GN_EPS = 64e-5

LANE = 128
SUBLANE = 8
P_W = 27 * LANE
LORA_IN0 = 3 * D_A
XG0 = LORA_IN0 + LANE
HALO = 32
ROUTE_W = LANE
TM = 256
TMX = 256
DMA_UNROLL = 8
VMEM_LIMIT = 56 * 1024 * 1024

_PERM_RANGES = ((0, D_A), (D_A + LORA_W, 2 * D_A + LORA_W), (2 * D_A + LORA_W, 3 * D_A + LORA_W),
                (D_A, D_A + LORA_W), (3 * D_A + LORA_W, SHIFT_W))
_PERM = np.concatenate([np.arange(a, b) for a, b in _PERM_RANGES])
_INV_PERM = np.argsort(_PERM)
assert len(_PERM) == SHIFT_W and (np.sort(_PERM) == np.arange(SHIFT_W)).all()


def _permute_cols(x):
    pad = jnp.zeros(x.shape[:-1] + (P_W - SHIFT_W,), x.dtype)
    return jnp.concatenate([x[..., a:b] for a, b in _PERM_RANGES] + [pad], axis=-1)


def _dot(a, b):
    return jnp.dot(a, b, preferred_element_type=F32)


def _dot_nt(a, b):
    return lax.dot_general(a, b, (((1,), (1,)), ((), ())), preferred_element_type=F32)


def _dot_tn(a, b):
    return lax.dot_general(a, b, (((0,), (0,)), ((), ())), preferred_element_type=F32)


def _split(x):
    hi = x.astype(BF16)
    lo = (x - hi.astype(F32)).astype(BF16)
    return hi, lo


def _dot3(x, w_hi, w_lo):
    x_hi, x_lo = _split(x)
    return _dot(x_hi, w_hi) + (_dot(x_lo, w_hi) + _dot(x_hi, w_lo))


def _rms(x, g):
    return x * lax.rsqrt(jnp.mean(x * x, axis=-1, keepdims=True) + RMS_EPS) * g


def _head_sum(x):
    r = lax.broadcasted_iota(jnp.int32, (LANE, LANE), 0) // HEAD
    c = lax.broadcasted_iota(jnp.int32, (LANE, LANE), 1) // HEAD
    ones = jnp.where(r == c, 1.0, 0.0).astype(BF16)
    outs = []
    for j in range(x.shape[1] // LANE):
        hi, lo = _split(x[:, j * LANE:(j + 1) * LANE])
        outs.append(_dot(hi, ones) + _dot(lo, ones))
    return jnp.concatenate(outs, axis=1)


def _const_spec(shape):
    nd = len(shape)
    return pl.BlockSpec(shape, lambda *_: (0,) * nd, pipeline_mode=pl.Buffered(1))


def _params(sem):
    return pltpu.CompilerParams(dimension_semantics=sem, vmem_limit_bytes=VMEM_LIMIT)


def _inprep_kernel(x_ref, n1_ref, wr_ref, wc_ref, bc_ref, bnd_ref, mu_ref, vp_ref, w2h_ref, w2l_ref,
                   g2h_ref, g2l_ref,
                   u_ref, tail_ref, rt_ref, at_ref, bt_ref, kt_ref, v_ref, ge_ref, bonus_ref, g_ref,
                   first_scr, last_scr, *, chunk, nb, step, tps):
    tm = x_ref.shape[0]
    if nb == 1:
        starts_stream = lax.rem(pl.program_id(0), tps) == 0

        @pl.when(starts_stream)
        def _():
            first_scr[...] = bnd_ref[0, 0:1, :]

        @pl.when(jnp.logical_not(starts_stream))
        def _():
            first_scr[...] = last_scr[...]

    h = _rms(x_ref[...], n1_ref[...]).astype(BF16)
    p = _dot(h, wr_ref[...])
    for j in range(nb):
        tail_ref[0, j:j + 1, :] = p[(j + 1) * step - 1:(j + 1) * step, :]
    c = _dot(h, wc_ref[...]) + bc_ref[...]
    u_ref[...] = c[:, :D_B] * jax.nn.sigmoid(c[:, D_B:])

    row = lax.broadcasted_iota(jnp.int32, (tm, 1), 0)
    prev = pltpu.roll(p, 1, axis=0)
    if nb == 1:
        prev = jnp.where(row == 0, first_scr[...], prev)
        last_scr[...] = p[tm - 1:tm, :]
    else:
        for j in range(nb):
            prev = jnp.where(row == j * step, bnd_ref[0, j:j + 1, :], prev)
    xs = p + (prev - p) * mu_ref[...]
    r = xs[:, 0:D_A]
    k = xs[:, D_A:2 * D_A]
    v = xs[:, 2 * D_A:3 * D_A]
    lin = xs[:, LORA_IN0:LORA_IN0 + LANE]
    lane = lax.broadcasted_iota(jnp.int32, (1, LANE), 1)
    lin = jnp.where(lane < LORA_W, jnp.tanh(lin), lin)
    lo = _dot3(lin, w2h_ref[...], w2l_ref[...])
    w0 = vp_ref[0:1, :]
    a0 = vp_ref[1:2, :]
    k_k = vp_ref[2:3, :]
    k_a = vp_ref[3:4, :]
    r_k = vp_ref[4:5, :]
    zw = -(w0 + lo[:, :D_A])
    softplus = jnp.maximum(zw, 0.0) + jnp.log1p(jnp.exp(-jnp.abs(zw)))
    logw = -jnp.exp(-softplus - 0.5)
    a = jax.nn.sigmoid(a0 + lo[:, D_A:])
    g = _dot3(jax.nn.sigmoid(xs[:, XG0:XG0 + 2 * LANE]), g2h_ref[...], g2l_ref[...])
    kku = k * k_k
    kk = kku / jnp.maximum(jnp.sqrt(_head_sum(kku * kku)), 1e-12)
    k2 = k * (1.0 + (a - 1.0) * k_a)
    bonus_ref[...] = (_head_sum(r * k2 * r_k) * v).astype(BF16)
    g_ref[...] = g.astype(BF16)
    ri = lax.broadcasted_iota(jnp.int32, (tm, tm), 0)
    ci = lax.broadcasted_iota(jnp.int32, (tm, tm), 1)
    tri = jnp.where((ri // chunk == ci // chunk) & (ci <= ri), 1.0, 0.0).astype(BF16)
    lw_hi, lw_lo = _split(logw)
    cum = _dot(tri, lw_hi) + _dot(tri, lw_lo)
    eg = jnp.exp(cum)
    egi = jnp.exp(-cum)
    rt_ref[...] = (r * eg).astype(BF16)
    at_ref[...] = (-kk * jnp.exp(cum - logw)).astype(BF16)
    bt_ref[...] = (kk * a * egi).astype(BF16)
    kt_ref[...] = (k2 * egi).astype(BF16)
    v_ref[...] = v.astype(BF16)
    for cc in range(tm // chunk):
        ge_ref[cc] = eg[(cc + 1) * chunk - 1:(cc + 1) * chunk, :]


def _inprep(x, n1, wr, wc, bc, bnd, mu, vp, w2h, w2l, g2h, g2l, *, chunk, nb, step, tps):
    n = x.shape[0]
    tok = pl.BlockSpec((TM, D_A), lambda i: (i, 0))
    return pl.pallas_call(
        functools.partial(_inprep_kernel, chunk=chunk, nb=nb, step=step, tps=tps),
        grid=(n // TM,),
        in_specs=[
            pl.BlockSpec((TM, D_MODEL), lambda i: (i, 0)),
            _const_spec((1, D_MODEL)),
            _const_spec((D_MODEL, P_W)),
            _const_spec((D_MODEL, 2 * D_B)),
            _const_spec((1, 2 * D_B)),
            pl.BlockSpec((1, nb, P_W), lambda i: (i // tps, 0, 0)),
            _const_spec((1, P_W)),
            _const_spec((8, D_A)),
            _const_spec((LANE, 2 * D_A)),
            _const_spec((LANE, 2 * D_A)),
            _const_spec((2 * LANE, D_A)),
            _const_spec((2 * LANE, D_A)),
        ],
        out_specs=[tok,
                   pl.BlockSpec((1, nb, P_W), lambda i: (i, 0, 0)),
                   tok, tok, tok, tok, tok,
                   pl.BlockSpec((TM // chunk, 1, D_A), lambda i: (i, 0, 0)),
                   tok, tok],
        out_shape=[jax.ShapeDtypeStruct((n, D_B), F32),
                   jax.ShapeDtypeStruct((n // TM, nb, P_W), F32)]
        + [jax.ShapeDtypeStruct((n, D_A), BF16)] * 5
        + [jax.ShapeDtypeStruct((n // chunk, 1, D_A), F32),
           jax.ShapeDtypeStruct((n, D_A), BF16),
           jax.ShapeDtypeStruct((n, D_A), BF16)],
        scratch_shapes=[pltpu.VMEM((1, P_W), F32), pltpu.VMEM((1, P_W), F32)],
        compiler_params=_params(("arbitrary",)),
        name="inprep",
    )(x, n1, wr, wc, bc, bnd, mu, vp, w2h, w2l, g2h, g2l)


def _wkv_intra_kernel(rt_ref, at_ref, bt_ref, kt_ref, v_ref, ge_ref, rc_ref, yc_ref, m2_ref, n2_ref,
                      *, chunk, nchunk):
    c = chunk
    w2 = 2 * c
    row = lax.broadcasted_iota(jnp.int32, (c, w2), 0)
    col = lax.broadcasted_iota(jnp.int32, (c, w2), 1)
    colc = jnp.where(col >= c, col - c, col)
    strict = colc < row
    incl = colc <= row
    eye_c = jnp.where(colc == row, 1.0, 0.0).astype(F32)
    tok0 = lax.broadcasted_iota(jnp.int32, (1, w2), 1) < c
    ch0 = lax.broadcasted_iota(jnp.int32, (1, LANE), 1) < HEAD
    rk = lax.broadcasted_iota(jnp.int32, (HEAD, LANE), 0)
    ck = lax.broadcasted_iota(jnp.int32, (HEAD, LANE), 1)
    eye_k = jnp.where(jnp.where(ck >= HEAD, ck - HEAD, ck) == rk, 1.0, 0.0).astype(F32)
    n_sq = int(math.log2(c)) - 1

    def bd(x, first):
        zero = jnp.zeros_like(x)
        return jnp.concatenate([jnp.where(first, x, zero), jnp.where(first, zero, x)], axis=0)

    def head_blocks(x):
        return [jnp.where(ch0, x[:HEAD, j * LANE:(j + 1) * LANE], x[HEAD:, j * LANE:(j + 1) * LANE])
                for j in range(x.shape[1] // LANE)]

    rng = range(nchunk)
    rows = [slice(ci * c, (ci + 1) * c) for ci in rng]
    at = [at_ref[r, :] for r in rows]
    rt = [rt_ref[r, :] for r in rows]
    bt = [bt_ref[r, :] for r in rows]
    kt = [kt_ref[r, :] for r in rows]
    vv = [v_ref[r, :] for r in rows]
    m = [_dot_nt(jnp.concatenate([at[i], rt[i]], axis=0),
                 jnp.concatenate([bd(bt[i], ch0), bd(kt[i], ch0)], axis=0)) for i in rng]
    a_ab = [jnp.where(strict, x[:c, :w2], 0.0) for x in m]
    a_rb = [jnp.where(incl, x[c:, :w2], 0.0).astype(BF16) for x in m]
    akv = [_dot(jnp.concatenate([jnp.where(strict, x[:c, w2:], 0.0), jnp.where(incl, x[c:, w2:], 0.0)],
                                axis=0).astype(BF16), bd(vv[i], ch0)) for i, x in enumerate(m)]
    kv = [head_blocks(_dot_tn(kt[i], vv[i]))[0] for i in rng]
    t = [eye_c + x for x in a_ab]
    pw = [x.astype(BF16) for x in a_ab]
    pw = [_dot(x, bd(x, tok0)).astype(BF16) for x in pw]
    for _ in range(n_sq - 1):
        st = [_dot(jnp.concatenate([t[i].astype(BF16), pw[i]], axis=0), bd(pw[i], tok0)) for i in rng]
        t = [t[i] + st[i][:c] for i in rng]
        pw = [x[c:].astype(BF16) for x in st]
    t = [t[i] + _dot(t[i].astype(BF16), bd(pw[i], tok0)) for i in rng]
    twb = [_dot(t[i].astype(BF16),
                jnp.concatenate([bd(at[i], ch0), bd(akv[i][:c].astype(BF16), ch0)], axis=1)).astype(BF16)
           for i in rng]
    rb = [_dot(a_rb[i], jnp.concatenate([bd(twb[i][:, :LANE], ch0), bd(twb[i][:, LANE:], ch0)], axis=1))
          for i in rng]
    xb = [head_blocks(_dot_tn(bt[i], twb[i])) for i in rng]
    for i in rng:
        rc_ref[rows[i], :] = (rt[i].astype(F32) + rb[i][:, :LANE]).astype(BF16)
        yc_ref[rows[i], :] = rb[i][:, LANE:] + akv[i][c:]
        g_full = jnp.broadcast_to(ge_ref[i], (LANE, LANE)).T
        g_col = jnp.where(ch0, g_full[:HEAD], g_full[HEAD:])
        m2_ref[i] = (g_col * (eye_k + xb[i][0]) - eye_k).astype(BF16)
        n2_ref[i] = g_col * (xb[i][1] + kv[i])


def _wkv_intra(rt, at, bt, kt, v, ge, *, chunk, nchunk):
    n = rt.shape[0]
    rows = chunk * nchunk
    tok = pl.BlockSpec((rows, LANE), lambda i, hp: (i, hp))
    mat = pl.BlockSpec((nchunk, HEAD, LANE), lambda i, hp: (i, 0, hp))
    return pl.pallas_call(
        functools.partial(_wkv_intra_kernel, chunk=chunk, nchunk=nchunk),
        grid=(n // rows, N_HEADS // 2),
        in_specs=[tok, tok, tok, tok, tok,
                  pl.BlockSpec((nchunk, 1, LANE), lambda i, hp: (i, 0, hp))],
        out_specs=[tok, tok, mat, mat],
        out_shape=[jax.ShapeDtypeStruct((n, D_A), BF16),
                   jax.ShapeDtypeStruct((n, D_A), F32),
                   jax.ShapeDtypeStruct((n // chunk, HEAD, D_A), BF16),
                   jax.ShapeDtypeStruct((n // chunk, HEAD, D_A), F32)],
        compiler_params=_params(("parallel", "parallel")),
        name="wkv_intra",
    )(rt, at, bt, kt, v, ge)


def _wkv_scan_kernel(rc_ref, yc_ref, m2_ref, n2_ref, s0_ref, y_ref, st_ref, h_scr, *, chunk, nchunk):
    j = pl.program_id(1)

    @pl.when(j == 0)
    def _():
        for h in range(N_HEADS):
            h_scr[:, h * HEAD:(h + 1) * HEAD] = s0_ref[0, h].T

    c = chunk

    def chunk_body(ci, carry):
        off = pl.multiple_of(ci * c, c)
        h_all = h_scr[...]
        hb = h_all.astype(BF16)
        rc = rc_ref[pl.ds(off, c), :]
        m2 = m2_ref[ci]
        ch0 = lax.broadcasted_iota(jnp.int32, (1, LANE), 1) < HEAD
        zero = jnp.zeros((HEAD, LANE), BF16)
        ys, mh = [], []
        for hp in range(N_HEADS // 2):
            sl = slice(hp * LANE, (hp + 1) * LANE)
            hpair = hb[:, sl]
            hbd = jnp.concatenate([jnp.where(ch0, hpair, zero), jnp.where(ch0, zero, hpair)], axis=0)
            both = _dot(jnp.concatenate([rc[:, sl], m2[:, sl]], axis=0), hbd)
            ys.append(both[:c])
            mh.append(both[c:])
        y_ref[pl.ds(off, c), :] = (jnp.concatenate(ys, axis=1) + yc_ref[pl.ds(off, c), :]).astype(BF16)
        h_scr[...] = h_all + jnp.concatenate(mh, axis=1) + n2_ref[ci]
        return carry

    lax.fori_loop(0, nchunk, chunk_body, 0)

    @pl.when(j == pl.num_programs(1) - 1)
    def _():
        for h in range(N_HEADS):
            st_ref[0, h] = h_scr[:, h * HEAD:(h + 1) * HEAD].T


def _wkv_scan(rc, yc, m2, n2, s0, *, n_streams, t_len, chunk, nchunk):
    n = rc.shape[0]
    rows = chunk * nchunk
    nj = t_len // rows
    tok = pl.BlockSpec((rows, D_A), lambda b, j: (b * nj + j, 0))
    mat = pl.BlockSpec((nchunk, HEAD, D_A), lambda b, j: (b * nj + j, 0, 0))
    st = pl.BlockSpec((1, N_HEADS, HEAD, HEAD), lambda b, j: (b, 0, 0, 0))
    return pl.pallas_call(
        functools.partial(_wkv_scan_kernel, chunk=chunk, nchunk=nchunk),
        grid=(n_streams, nj),
        in_specs=[tok, tok, mat, mat, st],
        out_specs=[tok, st],
        out_shape=[jax.ShapeDtypeStruct((n, D_A), BF16),
                   jax.ShapeDtypeStruct((n_streams, N_HEADS, HEAD, HEAD), F32)],
        scratch_shapes=[pltpu.VMEM((HEAD, D_A), F32)],
        compiler_params=_params(("parallel", "arbitrary")),
        name="wkv_scan",
    )(rc, yc, m2, n2, s0)


CONV_ROWS = 128


def _conv_kernel(u_ref, hown_ref, hinit_ref, cw_ref, cb_ref, lg_ref, lb_ref, o_ref, ext_scr, sh_scr, z_scr):
    tm = u_ref.shape[0]
    n_lt = D_B // LANE
    first = pl.program_id(1) == 0

    @pl.when(first)
    def _():
        for l in range(n_lt):
            ext_scr[l, 0:HALO, :] = hinit_ref[0, :, l * LANE:(l + 1) * LANE]

    @pl.when(jnp.logical_not(first))
    def _():
        for l in range(n_lt):
            ext_scr[l, 0:HALO, :] = hown_ref[:, l * LANE:(l + 1) * LANE]

    for l in range(n_lt):
        ext_scr[l, HALO:HALO + tm, :] = u_ref[:, l * LANE:(l + 1) * LANE]
    span = sh_scr.shape[1]
    lead = HALO - (CONV_W - 1)

    def lane_tile(l, carry):
        for s in range(1, SUBLANE):
            sh_scr[s - 1] = ext_scr[l, s:s + span, :]
        acc = jnp.broadcast_to(cb_ref[l], (tm, LANE))
        for j in range(CONV_W):
            s, q = (lead + j) % SUBLANE, (lead + j) // SUBLANE
            r0 = q * SUBLANE
            win = ext_scr[l, r0:r0 + tm, :] if s == 0 else sh_scr[s - 1, r0:r0 + tm, :]
            acc = acc + cw_ref[l, j:j + 1, :] * win
        z_scr[l] = acc
        return carry

    lax.fori_loop(0, n_lt, lane_tile, 0)
    rows = min(CONV_ROWS, tm)

    def norm(i, carry):
        r0 = pl.multiple_of(i * rows, rows)
        z = jnp.concatenate([z_scr[l, pl.ds(r0, rows), :] for l in range(n_lt)], axis=1)
        mu = jnp.mean(z, axis=-1, keepdims=True)
        d = z - mu
        var = jnp.mean(d * d, axis=-1, keepdims=True)
        zn = d * lax.rsqrt(var + LN_EPS) * lg_ref[...] + lb_ref[...]
        o_ref[pl.ds(r0, rows), :] = (zn * jax.nn.sigmoid(zn)).astype(BF16)
        return carry

    lax.fori_loop(0, tm // rows, norm, 0)


def _conv(u, hinit, cw, cb, lg, lb, *, n_streams, t_len, tm):
    n = u.shape[0]
    tps = t_len // tm
    if tps > 1:
        blocks_per_tile = tm // HALO
        own_map = lambda b, i: (jnp.maximum((b * tps + i) * blocks_per_tile - 1, 0), 0)
    else:
        own_map = lambda b, i: (0, 0)
    return pl.pallas_call(
        _conv_kernel,
        grid=(n_streams, tps),
        in_specs=[
            pl.BlockSpec((tm, D_B), lambda b, i: (b * tps + i, 0)),
            pl.BlockSpec((HALO, D_B), own_map),
            pl.BlockSpec((1, HALO, D_B), lambda b, i: (b, 0, 0)),
            _const_spec((D_B // LANE, HALO, LANE)),
            _const_spec((D_B // LANE, 1, LANE)),
            _const_spec((1, D_B)),
            _const_spec((1, D_B)),
        ],
        out_specs=pl.BlockSpec((tm, D_B), lambda b, i: (b * tps + i, 0)),
        out_shape=jax.ShapeDtypeStruct((n, D_B), BF16),
        scratch_shapes=[pltpu.VMEM((D_B // LANE, HALO + tm, LANE), F32),
                        pltpu.VMEM((SUBLANE - 1, HALO + tm - SUBLANE, LANE), F32),
                        pltpu.VMEM((D_B // LANE, tm, LANE), F32)],
        compiler_params=_params(("parallel", "arbitrary")),
        name="conv",
    )(u, u, hinit, cw, cb, lg, lb)


def _merge_kernel(x_ref, y_ref, bonus_ref, g_ref, ob_ref, n1_ref, wg_ref, bg_ref,
                  lng_ref, lnb_ref, woa_ref, wob_ref, o_ref):
    h = _rms(x_ref[...], n1_ref[...]).astype(BF16)
    gate = jax.nn.sigmoid(_dot(h, wg_ref[...]) + bg_ref[...])
    y = y_ref[...].astype(F32)
    d = y - _head_sum(y) * (1.0 / HEAD)
    var = _head_sum(d * d) * (1.0 / HEAD)
    yn = d * lax.rsqrt(var + GN_EPS) * lng_ref[...] + lnb_ref[...]
    oa = ((yn + bonus_ref[...].astype(F32)) * g_ref[...].astype(F32)).astype(BF16)
    merged = (gate[:, :D_MODEL] * _dot(oa, woa_ref[...])
              + gate[:, D_MODEL:] * _dot(ob_ref[...], wob_ref[...]))
    o_ref[...] = merged.astype(BF16)


def _merge(x, y, bonus, g, ob, n1, wg, bg, lng, lnb, woa, wob):
    n = x.shape[0]
    tok_a = pl.BlockSpec((TM, D_A), lambda i: (i, 0))
    tok_m = pl.BlockSpec((TM, D_MODEL), lambda i: (i, 0))
    return pl.pallas_call(
        _merge_kernel,
        grid=(n // TM,),
        in_specs=[tok_m, tok_a, tok_a, tok_a, tok_a,
                  _const_spec((1, D_MODEL)),
                  _const_spec((D_MODEL, 2 * D_MODEL)),
                  _const_spec((1, 2 * D_MODEL)),
                  _const_spec((1, D_A)),
                  _const_spec((1, D_A)),
                  _const_spec((D_A, D_MODEL)),
                  _const_spec((D_B, D_MODEL))],
        out_specs=tok_m,
        out_shape=jax.ShapeDtypeStruct((n, D_MODEL), BF16),
        compiler_params=_params(("parallel",)),
        name="merge",
    )(x, y, bonus, g, ob, n1, wg, bg, lng, lnb, woa, wob)


def _pack_bf16_pairs(x):
    w = x.shape[1] // 2
    bits = pltpu.bitcast(x.astype(BF16).astype(F32), jnp.uint32)
    return (bits[:, w:] & jnp.uint32(0xFFFF0000)) | (bits[:, :w] >> jnp.uint32(16))


def _unpack_bf16_pairs(u):
    lo = pltpu.bitcast(u << jnp.uint32(16), F32)
    hi = pltpu.bitcast(u & jnp.uint32(0xFFFF0000), F32)
    return jnp.concatenate([lo, hi], axis=1).astype(BF16)


def _outproj_kernel(m_ref, x_ref, wo_ref, n2_ref, wrh_ref, wrl_ref, brt_ref, x1_ref, h2_ref, lg_ref):
    x1 = x_ref[...] + _dot(m_ref[...], wo_ref[...])
    x1_ref[...] = x1
    h2 = _rms(x1, n2_ref[...])
    h2_ref[...] = _pack_bf16_pairs(h2)
    lg_ref[...] = _dot3(h2, wrh_ref[...], wrl_ref[...]) + brt_ref[...]


def _outproj(m, x, wo, n2, wrh, wrl, brt):
    n = x.shape[0]
    tok_m = pl.BlockSpec((TM, D_MODEL), lambda i: (i, 0))
    return pl.pallas_call(
        _outproj_kernel,
        grid=(n // TM,),
        in_specs=[tok_m, tok_m,
                  _const_spec((D_MODEL, D_MODEL)),
                  _const_spec((1, D_MODEL)),
                  _const_spec((D_MODEL, ROUTE_W)),
                  _const_spec((D_MODEL, ROUTE_W)),
                  _const_spec((1, ROUTE_W))],
        out_specs=[tok_m, pl.BlockSpec((TM, D_MODEL // 2), lambda i: (i, 0)),
                   pl.BlockSpec((TM, ROUTE_W), lambda i: (i, 0))],
        out_shape=[jax.ShapeDtypeStruct((n, D_MODEL), F32),
                   jax.ShapeDtypeStruct((n, D_MODEL // 2), jnp.uint32),
                   jax.ShapeDtypeStruct((n, ROUTE_W), F32)],
        compiler_params=_params(("parallel",)),
        name="outproj",
    )(m, x, wo, n2, wrh, wrl, brt)


def _route_kernel(lg_ref, ri_ref, rw_ref, cnt_ref, carry_scr):
    i = pl.program_id(0)
    tm = lg_ref.shape[0]

    @pl.when(i == 0)
    def _():
        carry_scr[...] = jnp.zeros_like(carry_scr)

    lg = lg_ref[...]
    lane = lax.broadcasted_iota(jnp.int32, (tm, ROUTE_W), 1)
    neg = jnp.float32(-jnp.inf)
    big = jnp.int32(ROUTE_W)
    grp = jnp.where(lane < N_GROUPS, lg, neg)
    gmax = jnp.max(grp, axis=1, keepdims=True)
    g_sel = jnp.min(jnp.where(grp == gmax, lane, big), axis=1, keepdims=True)
    p_grp = 1.0 / jnp.sum(jnp.where(lane < N_GROUPS, jnp.exp(lg - gmax), 0.0), axis=1, keepdims=True)
    lo = N_GROUPS + g_sel * EXP_PER_GROUP
    le = jnp.where((lane >= lo) & (lane < lo + EXP_PER_GROUP), lg, neg)
    m1 = jnp.max(le, axis=1, keepdims=True)
    i1 = jnp.min(jnp.where(le == m1, lane, big), axis=1, keepdims=True)
    le2 = jnp.where(lane == i1, neg, le)
    m2 = jnp.max(le2, axis=1, keepdims=True)
    i2 = jnp.min(jnp.where(le2 == m2, lane, big), axis=1, keepdims=True)
    e2 = jnp.exp(m2 - m1)
    wa = p_grp / (1.0 + e2)
    wb = p_grp * e2 / (1.0 + e2)
    ea = i1 - N_GROUPS
    eb = i2 - N_GROUPS
    onehot = jnp.where((lane == ea) | (lane == eb), 1.0, 0.0)
    ri_ = lax.broadcasted_iota(jnp.int32, (tm, tm), 0)
    ci_ = lax.broadcasted_iota(jnp.int32, (tm, tm), 1)
    below = jnp.where(ci_ < ri_, 1.0, 0.0).astype(BF16)
    before = _dot(below, onehot.astype(BF16)) + carry_scr[...]
    rank_a = jnp.sum(jnp.where(lane == ea, before, 0.0), axis=1, keepdims=True).astype(jnp.int32)
    rank_b = jnp.sum(jnp.where(lane == eb, before, 0.0), axis=1, keepdims=True).astype(jnp.int32)
    carry_scr[...] = carry_scr[...] + jnp.sum(onehot, axis=0, keepdims=True)
    ri = jnp.where(lane == 0, ea, jnp.where(lane == 1, eb,
                   jnp.where(lane == 2, rank_a, jnp.where(lane == 3, rank_b, 0))))
    ri_ref[0] = ri.T[0:SUBLANE, :]
    rw_ref[...] = jnp.where(lane == 0, wa, jnp.where(lane == 1, wb, 0.0))
    cnt_ref[...] = jnp.broadcast_to(carry_scr[...], cnt_ref.shape)


def _route(lg):
    n = lg.shape[0]
    tok = pl.BlockSpec((TM, ROUTE_W), lambda i: (i, 0))
    return pl.pallas_call(
        _route_kernel,
        grid=(n // TM,),
        in_specs=[tok],
        out_specs=[pl.BlockSpec((1, SUBLANE, TM), lambda i: (i, 0, 0)), tok,
                   pl.BlockSpec((8, ROUTE_W), lambda i: (0, 0))],
        out_shape=[jax.ShapeDtypeStruct((n // TM, SUBLANE, TM), jnp.int32),
                   jax.ShapeDtypeStruct((n, ROUTE_W), F32),
                   jax.ShapeDtypeStruct((8, ROUTE_W), F32)],
        scratch_shapes=[pltpu.VMEM((1, ROUTE_W), F32)],
        compiler_params=_params(("arbitrary",)),
        name="route",
    )(lg)


def _dispatch_kernel(pos_ref, h_ref, xs_in_ref, xs_ref, sem):
    del xs_in_ref
    tm = h_ref.shape[0]

    def copy(r, slot):
        return pltpu.make_async_copy(h_ref.at[pl.ds(r, 1)],
                                     xs_ref.at[pl.ds(pos_ref[0, slot, r], 1)], sem)

    def start(r, carry):
        copy(r, 0).start(priority=0)
        copy(r, 1).start(priority=1)
        return carry

    def wait(r, carry):
        copy(r, 0).wait()
        copy(r, 1).wait()
        return carry

    lax.fori_loop(0, tm, start, 0, unroll=DMA_UNROLL)
    lax.fori_loop(0, tm, wait, 0, unroll=DMA_UNROLL)


def _dispatch(pos, h2, xs, *, tile0):
    n = h2.shape[0]
    return pl.pallas_call(
        _dispatch_kernel,
        grid=(n // TM,),
        in_specs=[
            pl.BlockSpec((1, 2, TM), lambda i: (i + tile0, 0, 0), memory_space=pltpu.SMEM),
            pl.BlockSpec((TM, D_MODEL // 2), lambda i: (i, 0)),
            pl.BlockSpec(memory_space=pl.ANY),
        ],
        out_specs=pl.BlockSpec(memory_space=pl.ANY),
        out_shape=jax.ShapeDtypeStruct(xs.shape, xs.dtype),
        scratch_shapes=[pltpu.SemaphoreType.DMA(())],
        input_output_aliases={2: 0},
        compiler_params=_params(("arbitrary",)),
        name="dispatch",
    )(pos, h2, xs)


def _experts_kernel(te_ref, tv_ref, xs_ref, wg_ref, wu_ref, wd_ref, y_ref):
    t = pl.program_id(0)

    @pl.when(tv_ref[t] == 1)
    def _():
        x = _unpack_bf16_pairs(xs_ref[...])
        gt = _dot(x, wg_ref[0])
        up = _dot(x, wu_ref[0])
        hid = (gt * jax.nn.sigmoid(gt) * up).astype(BF16)
        y_ref[...] = _pack_bf16_pairs(_dot(hid, wd_ref[0]))

    @pl.when(tv_ref[t] == 0)
    def _():
        y_ref[...] = jnp.zeros_like(y_ref)


def _experts(te, tv, xs, wg, wu, wd):
    p = xs.shape[0]
    grid_spec = pltpu.PrefetchScalarGridSpec(
        num_scalar_prefetch=2,
        grid=(p // TMX,),
        in_specs=[
            pl.BlockSpec((TMX, D_MODEL // 2), lambda t, te, tv: (t, 0)),
            pl.BlockSpec((1, D_MODEL, D_EXPERT), lambda t, te, tv: (te[t], 0, 0)),
            pl.BlockSpec((1, D_MODEL, D_EXPERT), lambda t, te, tv: (te[t], 0, 0)),
            pl.BlockSpec((1, D_EXPERT, D_MODEL), lambda t, te, tv: (te[t], 0, 0)),
        ],
        out_specs=pl.BlockSpec((TMX, D_MODEL // 2), lambda t, te, tv: (t, 0)),
    )
    return pl.pallas_call(
        _experts_kernel,
        grid_spec=grid_spec,
        out_shape=jax.ShapeDtypeStruct((p, D_MODEL // 2), jnp.uint32),
        compiler_params=_params(("arbitrary",)),
        name="experts",
    )(te, tv, xs, wg, wu, wd)


def _combine_kernel(pos_ref, x1_ref, rw_ref, fg_ref, ys_ref, o_ref, ya_scr, yb_scr, sem):
    tm = x1_ref.shape[0]

    def copies(r):
        return (pltpu.make_async_copy(ys_ref.at[pl.ds(pos_ref[0, 0, r], 1)], ya_scr.at[pl.ds(r, 1)], sem),
                pltpu.make_async_copy(ys_ref.at[pl.ds(pos_ref[0, 1, r], 1)], yb_scr.at[pl.ds(r, 1)], sem))

    def start(r, carry):
        ca, cb = copies(r)
        ca.start(priority=0)
        cb.start(priority=1)
        return carry

    def wait(r, carry):
        ca, cb = copies(r)
        ca.wait()
        cb.wait()
        return carry

    lax.fori_loop(0, tm, start, 0, unroll=DMA_UNROLL)
    lax.fori_loop(0, tm, wait, 0, unroll=DMA_UNROLL)
    rw = rw_ref[...]
    ya = _unpack_bf16_pairs(ya_scr[...]).astype(F32)
    yb = _unpack_bf16_pairs(yb_scr[...]).astype(F32)
    x2 = x1_ref[...] + rw[:, 0:1] * ya + rw[:, 1:2] * yb
    o_ref[...] = _rms(x2, fg_ref[...])


def _combine(pos, x1, rw, fg, ys, *, tile0):
    n = x1.shape[0]
    tok_m = pl.BlockSpec((TM, D_MODEL), lambda i: (i, 0))
    return pl.pallas_call(
        _combine_kernel,
        grid=(n // TM,),
        in_specs=[
            pl.BlockSpec((1, 2, TM), lambda i: (i + tile0, 0, 0), memory_space=pltpu.SMEM),
            tok_m,
            pl.BlockSpec((TM, ROUTE_W), lambda i: (i + tile0, 0)),
            _const_spec((1, D_MODEL)),
            pl.BlockSpec(memory_space=pl.ANY),
        ],
        out_specs=tok_m,
        out_shape=jax.ShapeDtypeStruct((n, D_MODEL), F32),
        scratch_shapes=[pltpu.VMEM((TM, D_MODEL // 2), jnp.uint32),
                        pltpu.VMEM((TM, D_MODEL // 2), jnp.uint32),
                        pltpu.SemaphoreType.DMA(())],
        compiler_params=_params(("arbitrary",)),
        name="combine",
    )(pos, x1, rw, fg, ys)


def _row(v):
    return v.reshape(1, -1).astype(F32)


def _prepare_weights(norm1_g, w_in, b_conv_in, b_gate, mu_shift, w0, w_lora2, a0, a_lora2, g_lora2,
                     k_k, k_a, r_k, lnx_g, lnx_b, w_out_a, conv_w, conv_b, cln_g, cln_b, w_out_b, w_o,
                     norm2_g, w_rg, b_rg, w_re, b_re, w_gate_e, w_up_e, w_down_e, final_g):
    w = {}
    w["n1"] = _row(norm1_g)
    w["wr"] = _permute_cols(w_in[:, :SHIFT_W]).astype(BF16)
    w["wc"] = w_in[:, SHIFT_W:SHIFT_W + 2 * D_B].astype(BF16)
    w["wg"] = w_in[:, SHIFT_W + 2 * D_B:].astype(BF16)
    w["bc"] = _row(b_conv_in)
    w["bg"] = _row(b_gate)
    w["mu"] = _permute_cols(mu_shift).reshape(1, P_W)
    w["vp"] = jnp.concatenate([jnp.stack([w0, a0, k_k, k_a, r_k.reshape(-1)]),
                               jnp.zeros((3, D_A), F32)], axis=0)
    z = jnp.zeros((LORA_W, D_A), F32)
    w2 = jnp.concatenate([jnp.concatenate([w_lora2, z], axis=1),
                          jnp.concatenate([z, a_lora2], axis=1)], axis=0)
    w["w2h"], w["w2l"] = _split(w2)
    g2 = jnp.pad(g_lora2, ((0, 2 * LANE - LORA_G), (0, 0)))
    w["g2h"], w["g2l"] = _split(g2)
    w["lng"] = _row(lnx_g)
    w["lnb"] = _row(lnx_b)
    w["woa"] = w_out_a.astype(BF16)
    w["wob"] = w_out_b.astype(BF16)
    w["cw"] = jnp.pad(conv_w, ((0, HALO - CONV_W), (0, 0))).reshape(HALO, D_B // LANE, LANE).transpose(1, 0, 2)
    w["cb"] = conv_b.reshape(D_B // LANE, 1, LANE)
    w["clg"] = _row(cln_g)
    w["clb"] = _row(cln_b)
    w["wo"] = w_o.astype(BF16)
    w["n2"] = _row(norm2_g)
    pad_r = ROUTE_W - N_GROUPS - N_EXPERTS
    w["wrh"], w["wrl"] = _split(jnp.pad(jnp.concatenate([w_rg, w_re], axis=1), ((0, 0), (0, pad_r))))
    w["brt"] = jnp.pad(jnp.concatenate([b_rg, b_re]), (0, pad_r)).reshape(1, ROUTE_W)
    w["wge"] = w_gate_e.astype(BF16)
    w["wue"] = w_up_e.astype(BF16)
    w["wde"] = w_down_e.astype(BF16)
    w["fg"] = _row(final_g)
    return w


def _trunk(x, shift0, wkv0, conv0, w, *, chunk, nchunk, intra_chunks, conv_tm):
    b, t, _ = x.shape
    n = b * t
    xf = x.reshape(n, D_MODEL)
    shift_p = _permute_cols(shift0)
    tps = max(t // TM, 1)
    nb, step = (1, TM) if t >= TM else (TM // t, t)
    u, tails, rt, at, bt, kt, vv, ge, bonus, g = _inprep(
        xf, w["n1"], w["wr"], w["wc"], w["bc"], shift_p.reshape(-1, nb, P_W), w["mu"], w["vp"],
        w["w2h"], w["w2l"], w["g2h"], w["g2l"], chunk=chunk, nb=nb, step=step, tps=tps)
    rc, yc, m2, n2 = _wkv_intra(rt, at, bt, kt, vv, ge, chunk=chunk, nchunk=intra_chunks)
    y, wkv_new = _wkv_scan(rc, yc, m2, n2, wkv0, n_streams=b, t_len=t, chunk=chunk, nchunk=nchunk)
    hinit = jnp.pad(conv0, ((0, 0), (HALO - (CONV_W - 1), 0), (0, 0)))
    ob = _conv(u, hinit, w["cw"], w["cb"], w["clg"], w["clb"], n_streams=b, t_len=t, tm=conv_tm)
    merged = _merge(xf, y, bonus, g, ob, w["n1"], w["wg"], w["bg"], w["lng"], w["lnb"], w["woa"], w["wob"])
    x1, h2, lg = _outproj(merged, xf, w["wo"], w["n2"], w["wrh"], w["wrl"], w["brt"])
    new_shift = tails.reshape(b, -1, P_W)[:, -1, :SHIFT_W][:, _INV_PERM]
    ext_tail = jnp.concatenate([conv0, u.reshape(b, t, D_B)[:, -(CONV_W - 1):]], axis=1)[:, -(CONV_W - 1):]
    return x1, h2, lg, new_shift, wkv_new, ext_tail


def kernel(x_prompt, x_sample, state_shift, state_wkv, cache_conv, norm1_g, w_in, b_conv_in, b_gate,
           mu_shift, w0, w_lora2, a0, a_lora2, g_lora2, k_k, k_a, r_k, lnx_g, lnx_b, w_out_a, conv_w,
           conv_b, cln_g, cln_b, w_out_b, w_o, norm2_g, w_rg, b_rg, w_re, b_re, w_gate_e, w_up_e,
           w_down_e, final_g):
    per_layer = (norm1_g, w_in, b_conv_in, b_gate, mu_shift, w0, w_lora2, a0, a_lora2, g_lora2, k_k, k_a,
                 r_k, lnx_g, lnx_b, w_out_a, conv_w, conv_b, cln_g, cln_b, w_out_b, w_o, norm2_g, w_rg,
                 b_rg, w_re, b_re, w_gate_e, w_up_e, w_down_e)
    assert norm1_g.shape[0] == 1, "single layer only"
    w = _prepare_weights(*[p[0] for p in per_layer], final_g)
    bp, tp, _ = x_prompt.shape
    bs, ts, _ = x_sample.shape
    zero_shift = jnp.zeros((bp, SHIFT_W), F32)
    zero_wkv = jnp.zeros((bp, N_HEADS, HEAD, HEAD), F32)
    zero_conv = jnp.zeros((bp, CONV_W - 1, D_B), F32)
    x1p, h2p, lgp, shift_p, wkv_p, conv_p = _trunk(
        x_prompt, zero_shift, zero_wkv, zero_conv, w, chunk=64, nchunk=min(8, tp // 64),
        intra_chunks=min(16, tp // 64), conv_tm=TM)
    x1s, h2s, lgs, shift_s, wkv_s, conv_s = _trunk(
        x_sample, state_shift[0], state_wkv[0], cache_conv[0], w, chunk=ts, nchunk=1,
        intra_chunks=TM // ts, conv_tm=ts)

    n_p, n_s = bp * tp, bs * ts
    n_all = n_p + n_s
    ri, rw, cnt = _route(jnp.concatenate([lgp, lgs], axis=0))
    counts = cnt[0, :N_EXPERTS].astype(jnp.int32)
    tiles_per = (counts + TMX - 1) // TMX
    tile_end = jnp.cumsum(tiles_per)
    offs = (tile_end - tiles_per) * TMX
    n_tiles = (2 * n_all + N_EXPERTS * (TMX - 1) + TMX - 1) // TMX
    t_idx = jnp.arange(n_tiles, dtype=jnp.int32)
    te = jnp.sum(t_idx[:, None] >= tile_end[None, :], axis=1).astype(jnp.int32)
    tv = (t_idx < tile_end[-1]).astype(jnp.int32)
    e_ids = jnp.arange(N_EXPERTS, dtype=jnp.int32)
    last_e = jnp.max(jnp.where(tiles_per > 0, e_ids, 0))
    te = jnp.where(tv == 1, jnp.minimum(te, N_EXPERTS - 1), last_e)
    base = jnp.sum(jnp.where(ri[:, 0:2, :, None] == e_ids, offs, 0), axis=-1)
    pos = base + ri[:, 2:4, :]
    xs = jnp.zeros((n_tiles * TMX, D_MODEL // 2), jnp.uint32)
    xs = _dispatch(pos, h2p, xs, tile0=0)
    xs = _dispatch(pos, h2s, xs, tile0=n_p // TM)
    ys = _experts(te, tv, xs, w["wge"], w["wue"], w["wde"])
    yp = _combine(pos, x1p, rw, w["fg"], ys, tile0=0)
    ysm = _combine(pos, x1s, rw, w["fg"], ys, tile0=n_p // TM)
    return (yp.reshape(bp, tp, D_MODEL), ysm.reshape(bs, ts, D_MODEL),
            shift_p[None], wkv_p[None], conv_p[None],
            shift_s[None], wkv_s[None], conv_s[None])
```

```python
import functools
import math

import numpy as np
import jax
import jax.numpy as jnp
from jax import lax
from jax.experimental import pallas as pl
from jax.experimental.pallas import tpu as pltpu

F32 = jnp.float32
BF16 = jnp.bfloat16

D_MODEL = 2048
D_A = 1024
HEAD = 64
N_HEADS = D_A // HEAD
LORA_W = 64
LORA_A = 64
LORA_G = 160
SHIFT_W = 3 * D_A + LORA_W + LORA_A + LORA_G
D_B = 1024
CONV_W = 31
N_GROUPS = 4
EXP_PER_GROUP = 8
N_EXPERTS = N_GROUPS * EXP_PER_GROUP
D_EXPERT = 512
RMS_EPS = 1e-6
LN_EPS = 1e-5
GN_EPS = 64e-5

LANE = 128
SUBLANE = 8
P_W = 27 * LANE
LORA_IN0 = 3 * D_A
XG0 = LORA_IN0 + LANE
HALO = 32
ROUTE_W = LANE
TM = 256
TMX = 256
VMEM_LIMIT = 56 * 1024 * 1024

_PERM_RANGES = ((0, D_A), (D_A + LORA_W, 2 * D_A + LORA_W), (2 * D_A + LORA_W, 3 * D_A + LORA_W),
                (D_A, D_A + LORA_W), (3 * D_A + LORA_W, SHIFT_W))
_PERM = np.concatenate([np.arange(a, b) for a, b in _PERM_RANGES])
_INV_PERM = np.argsort(_PERM)
assert len(_PERM) == SHIFT_W and (np.sort(_PERM) == np.arange(SHIFT_W)).all()


def _permute_cols(x):
    pad = jnp.zeros(x.shape[:-1] + (P_W - SHIFT_W,), x.dtype)
    return jnp.concatenate([x[..., a:b] for a, b in _PERM_RANGES] + [pad], axis=-1)


def _dot(a, b):
    return jnp.dot(a, b, preferred_element_type=F32)


def _dot_nt(a, b):
    return lax.dot_general(a, b, (((1,), (1,)), ((), ())), preferred_element_type=F32)


def _dot_tn(a, b):
    return lax.dot_general(a, b, (((0,), (0,)), ((), ())), preferred_element_type=F32)


def _split(x):
    hi = x.astype(BF16)
    lo = (x - hi.astype(F32)).astype(BF16)
    return hi, lo


def _dot3(x, w_hi, w_lo):
    x_hi, x_lo = _split(x)
    return _dot(x_hi, w_hi) + (_dot(x_lo, w_hi) + _dot(x_hi, w_lo))


def _rms(x, g):
    return x * lax.rsqrt(jnp.mean(x * x, axis=-1, keepdims=True) + RMS_EPS) * g


def _head_sum(x):
    r = lax.broadcasted_iota(jnp.int32, (LANE, LANE), 0) // HEAD
    c = lax.broadcasted_iota(jnp.int32, (LANE, LANE), 1) // HEAD
    ones = jnp.where(r == c, 1.0, 0.0).astype(BF16)
    outs = []
    for j in range(x.shape[1] // LANE):
        hi, lo = _split(x[:, j * LANE:(j + 1) * LANE])
        outs.append(_dot(hi, ones) + _dot(lo, ones))
    return jnp.concatenate(outs, axis=1)


def _const_spec(shape):
    nd = len(shape)
    return pl.BlockSpec(shape, lambda *_: (0,) * nd, pipeline_mode=pl.Buffered(1))


def _params(sem):
    return pltpu.CompilerParams(dimension_semantics=sem, vmem_limit_bytes=VMEM_LIMIT)


def _inprep_kernel(x_ref, n1_ref, wr_ref, wc_ref, bc_ref, bnd_ref, mu_ref, vp_ref, w2h_ref, w2l_ref,
                   g2h_ref, g2l_ref,
                   u_ref, tail_ref, rt_ref, at_ref, bt_ref, kt_ref, v_ref, ge_ref, bonus_ref, g_ref,
                   first_scr, last_scr, *, chunk, nb, step, tps):
    tm = x_ref.shape[0]
    if nb == 1:
        starts_stream = lax.rem(pl.program_id(0), tps) == 0

        @pl.when(starts_stream)
        def _():
            first_scr[...] = bnd_ref[0, 0:1, :]

        @pl.when(jnp.logical_not(starts_stream))
        def _():
            first_scr[...] = last_scr[...]

    h = _rms(x_ref[...], n1_ref[...]).astype(BF16)
    p = _dot(h, wr_ref[...])
    for j in range(nb):
        tail_ref[0, j:j + 1, :] = p[(j + 1) * step - 1:(j + 1) * step, :]
    c = _dot(h, wc_ref[...]) + bc_ref[...]
    u_ref[...] = c[:, :D_B] * jax.nn.sigmoid(c[:, D_B:])

    row = lax.broadcasted_iota(jnp.int32, (tm, 1), 0)
    prev = pltpu.roll(p, 1, axis=0)
    if nb == 1:
        prev = jnp.where(row == 0, first_scr[...], prev)
        last_scr[...] = p[tm - 1:tm, :]
    else:
        for j in range(nb):
            prev = jnp.where(row == j * step, bnd_ref[0, j:j + 1, :], prev)
    xs = p + (prev - p) * mu_ref[...]
    r = xs[:, 0:D_A]
    k = xs[:, D_A:2 * D_A]
    v = xs[:, 2 * D_A:3 * D_A]
    lin = xs[:, LORA_IN0:LORA_IN0 + LANE]
    lane = lax.broadcasted_iota(jnp.int32, (1, LANE), 1)
    lin = jnp.where(lane < LORA_W, jnp.tanh(lin), lin)
    lo = _dot3(lin, w2h_ref[...], w2l_ref[...])
    w0 = vp_ref[0:1, :]
    a0 = vp_ref[1:2, :]
    k_k = vp_ref[2:3, :]
    k_a = vp_ref[3:4, :]
    r_k = vp_ref[4:5, :]
    zw = -(w0 + lo[:, :D_A])
    softplus = jnp.maximum(zw, 0.0) + jnp.log1p(jnp.exp(-jnp.abs(zw)))
    logw = -jnp.exp(-softplus - 0.5)
    a = jax.nn.sigmoid(a0 + lo[:, D_A:])
    g = _dot3(jax.nn.sigmoid(xs[:, XG0:XG0 + 2 * LANE]), g2h_ref[...], g2l_ref[...])
    kku = k * k_k
    kk = kku / jnp.maximum(jnp.sqrt(_head_sum(kku * kku)), 1e-12)
    k2 = k * (1.0 + (a - 1.0) * k_a)
    bonus_ref[...] = (_head_sum(r * k2 * r_k) * v).astype(BF16)
    g_ref[...] = g.astype(BF16)
    ri = lax.broadcasted_iota(jnp.int32, (tm, tm), 0)
    ci = lax.broadcasted_iota(jnp.int32, (tm, tm), 1)
    tri = jnp.where((ri // chunk == ci // chunk) & (ci <= ri), 1.0, 0.0).astype(BF16)
    lw_hi, lw_lo = _split(logw)
    cum = _dot(tri, lw_hi) + _dot(tri, lw_lo)
    eg = jnp.exp(cum)
    egi = jnp.exp(-cum)
    rt_ref[...] = (r * eg).astype(BF16)
    at_ref[...] = (-kk * jnp.exp(cum - logw)).astype(BF16)
    bt_ref[...] = (kk * a * egi).astype(BF16)
    kt_ref[...] = (k2 * egi).astype(BF16)
    v_ref[...] = v.astype(BF16)
    for cc in range(tm // chunk):
        ge_ref[cc] = eg[(cc + 1) * chunk - 1:(cc + 1) * chunk, :]


def _inprep(x, n1, wr, wc, bc, bnd, mu, vp, w2h, w2l, g2h, g2l, *, chunk, nb, step, tps):
    n = x.shape[0]
    tok = pl.BlockSpec((TM, D_A), lambda i: (i, 0))
    return pl.pallas_call(
        functools.partial(_inprep_kernel, chunk=chunk, nb=nb, step=step, tps=tps),
        grid=(n // TM,),
        in_specs=[
            pl.BlockSpec((TM, D_MODEL), lambda i: (i, 0)),
            _const_spec((1, D_MODEL)),
            _const_spec((D_MODEL, P_W)),
            _const_spec((D_MODEL, 2 * D_B)),
            _const_spec((1, 2 * D_B)),
            pl.BlockSpec((1, nb, P_W), lambda i: (i // tps, 0, 0)),
            _const_spec((1, P_W)),
            _const_spec((8, D_A)),
            _const_spec((LANE, 2 * D_A)),
            _const_spec((LANE, 2 * D_A)),
            _const_spec((2 * LANE, D_A)),
            _const_spec((2 * LANE, D_A)),
        ],
        out_specs=[tok,
                   pl.BlockSpec((1, nb, P_W), lambda i: (i, 0, 0)),
                   tok, tok, tok, tok, tok,
                   pl.BlockSpec((TM // chunk, 1, D_A), lambda i: (i, 0, 0)),
                   tok, tok],
        out_shape=[jax.ShapeDtypeStruct((n, D_B), F32),
                   jax.ShapeDtypeStruct((n // TM, nb, P_W), F32)]
        + [jax.ShapeDtypeStruct((n, D_A), BF16)] * 5
        + [jax.ShapeDtypeStruct((n // chunk, 1, D_A), F32),
           jax.ShapeDtypeStruct((n, D_A), BF16),
           jax.ShapeDtypeStruct((n, D_A), BF16)],
        scratch_shapes=[pltpu.VMEM((1, P_W), F32), pltpu.VMEM((1, P_W), F32)],
        compiler_params=_params(("arbitrary",)),
        name="inprep",
    )(x, n1, wr, wc, bc, bnd, mu, vp, w2h, w2l, g2h, g2l)


def _wkv_intra_kernel(rt_ref, at_ref, bt_ref, kt_ref, v_ref, ge_ref, rc_ref, yc_ref, m2_ref, n2_ref,
                      *, chunk, nchunk):
    c = chunk
    w2 = 2 * c
    row = lax.broadcasted_iota(jnp.int32, (c, w2), 0)
    col = lax.broadcasted_iota(jnp.int32, (c, w2), 1)
    colc = jnp.where(col >= c, col - c, col)
    strict = colc < row
    incl = colc <= row
    eye_c = jnp.where(colc == row, 1.0, 0.0).astype(F32)
    tok0 = lax.broadcasted_iota(jnp.int32, (1, w2), 1) < c
    ch0 = lax.broadcasted_iota(jnp.int32, (1, LANE), 1) < HEAD
    rk = lax.broadcasted_iota(jnp.int32, (HEAD, LANE), 0)
    ck = lax.broadcasted_iota(jnp.int32, (HEAD, LANE), 1)
    eye_k = jnp.where(jnp.where(ck >= HEAD, ck - HEAD, ck) == rk, 1.0, 0.0).astype(F32)
    n_sq = int(math.log2(c)) - 1

    def bd(x, first):
        zero = jnp.zeros_like(x)
        return jnp.concatenate([jnp.where(first, x, zero), jnp.where(first, zero, x)], axis=0)

    def head_blocks(x):
        return [jnp.where(ch0, x[:HEAD, j * LANE:(j + 1) * LANE], x[HEAD:, j * LANE:(j + 1) * LANE])
                for j in range(x.shape[1] // LANE)]

    rng = range(nchunk)
    rows = [slice(ci * c, (ci + 1) * c) for ci in rng]
    at = [at_ref[r, :] for r in rows]
    rt = [rt_ref[r, :] for r in rows]
    bt = [bt_ref[r, :] for r in rows]
    kt = [kt_ref[r, :] for r in rows]
    vv = [v_ref[r, :] for r in rows]
    m = [_dot_nt(jnp.concatenate([at[i], rt[i]], axis=0),
                 jnp.concatenate([bd(bt[i], ch0), bd(kt[i], ch0)], axis=0)) for i in rng]
    a_ab = [jnp.where(strict, x[:c, :w2], 0.0) for x in m]
    a_rb = [jnp.where(incl, x[c:, :w2], 0.0).astype(BF16) for x in m]
    akv = [_dot(jnp.concatenate([jnp.where(strict, x[:c, w2:], 0.0), jnp.where(incl, x[c:, w2:], 0.0)],
                                axis=0).astype(BF16), bd(vv[i], ch0)) for i, x in enumerate(m)]
    kv = [head_blocks(_dot_tn(kt[i], vv[i]))[0] for i in rng]
    t = [eye_c + x for x in a_ab]
    pw = [x.astype(BF16) for x in a_ab]
    pw = [_dot(x, bd(x, tok0)).astype(BF16) for x in pw]
    for _ in range(n_sq - 1):
        st = [_dot(jnp.concatenate([t[i].astype(BF16), pw[i]], axis=0), bd(pw[i], tok0)) for i in rng]
        t = [t[i] + st[i][:c] for i in rng]
        pw = [x[c:].astype(BF16) for x in st]
    t = [t[i] + _dot(t[i].astype(BF16), bd(pw[i], tok0)) for i in rng]
    twb = [_dot(t[i].astype(BF16),
                jnp.concatenate([bd(at[i], ch0), bd(akv[i][:c].astype(BF16), ch0)], axis=1)).astype(BF16)
           for i in rng]
    rb = [_dot(a_rb[i], jnp.concatenate([bd(twb[i][:, :LANE], ch0), bd(twb[i][:, LANE:], ch0)], axis=1))
          for i in rng]
    xb = [head_blocks(_dot_tn(bt[i], twb[i])) for i in rng]
    for i in rng:
        rc_ref[rows[i], :] = (rt[i].astype(F32) + rb[i][:, :LANE]).astype(BF16)
        yc_ref[rows[i], :] = rb[i][:, LANE:] + akv[i][c:]
        g_full = jnp.broadcast_to(ge_ref[i], (LANE, LANE)).T
        g_col = jnp.where(ch0, g_full[:HEAD], g_full[HEAD:])
        m2_ref[i] = (g_col * (eye_k + xb[i][0]) - eye_k).astype(BF16)
        n2_ref[i] = g_col * (xb[i][1] + kv[i])


def _wkv_intra(rt, at, bt, kt, v, ge, *, chunk, nchunk):
    n = rt.shape[0]
    rows = chunk * nchunk
    tok = pl.BlockSpec((rows, LANE), lambda i, hp: (i, hp))
    mat = pl.BlockSpec((nchunk, HEAD, LANE), lambda i, hp: (i, 0, hp))
    return pl.pallas_call(
        functools.partial(_wkv_intra_kernel, chunk=chunk, nchunk=nchunk),
        grid=(n // rows, N_HEADS // 2),
        in_specs=[tok, tok, tok, tok, tok,
                  pl.BlockSpec((nchunk, 1, LANE), lambda i, hp: (i, 0, hp))],
        out_specs=[tok, tok, mat, mat],
        out_shape=[jax.ShapeDtypeStruct((n, D_A), BF16),
                   jax.ShapeDtypeStruct((n, D_A), F32),
                   jax.ShapeDtypeStruct((n // chunk, HEAD, D_A), BF16),
                   jax.ShapeDtypeStruct((n // chunk, HEAD, D_A), F32)],
        compiler_params=_params(("parallel", "parallel")),
        name="wkv_intra",
    )(rt, at, bt, kt, v, ge)


def _wkv_scan_kernel(rc_ref, yc_ref, m2_ref, n2_ref, s0_ref, y_ref, st_ref, h_scr, *, chunk, nchunk):
    j = pl.program_id(1)

    @pl.when(j == 0)
    def _():
        for h in range(N_HEADS):
            h_scr[:, h * HEAD:(h + 1) * HEAD] = s0_ref[0, h].T

    c = chunk

    def chunk_body(ci, carry):
        off = pl.multiple_of(ci * c, c)
        h_all = h_scr[...]
        hb = h_all.astype(BF16)
        rc = rc_ref[pl.ds(off, c), :]
        m2 = m2_ref[ci]
        ch0 = lax.broadcasted_iota(jnp.int32, (1, LANE), 1) < HEAD
        zero = jnp.zeros((HEAD, LANE), BF16)
        ys, mh = [], []
        for hp in range(N_HEADS // 2):
            sl = slice(hp * LANE, (hp + 1) * LANE)
            hpair = hb[:, sl]
            hbd = jnp.concatenate([jnp.where(ch0, hpair, zero), jnp.where(ch0, zero, hpair)], axis=0)
            both = _dot(jnp.concatenate([rc[:, sl], m2[:, sl]], axis=0), hbd)
            ys.append(both[:c])
            mh.append(both[c:])
        y_ref[pl.ds(off, c), :] = (jnp.concatenate(ys, axis=1) + yc_ref[pl.ds(off, c), :]).astype(BF16)
        h_scr[...] = h_all + jnp.concatenate(mh, axis=1) + n2_ref[ci]
        return carry

    lax.fori_loop(0, nchunk, chunk_body, 0)

    @pl.when(j == pl.num_programs(1) - 1)
    def _():
        for h in range(N_HEADS):
            st_ref[0, h] = h_scr[:, h * HEAD:(h + 1) * HEAD].T


def _wkv_scan(rc, yc, m2, n2, s0, *, n_streams, t_len, chunk, nchunk):
    n = rc.shape[0]
    rows = chunk * nchunk
    nj = t_len // rows
    tok = pl.BlockSpec((rows, D_A), lambda b, j: (b * nj + j, 0))
    mat = pl.BlockSpec((nchunk, HEAD, D_A), lambda b, j: (b * nj + j, 0, 0))
    st = pl.BlockSpec((1, N_HEADS, HEAD, HEAD), lambda b, j: (b, 0, 0, 0))
    return pl.pallas_call(
        functools.partial(_wkv_scan_kernel, chunk=chunk, nchunk=nchunk),
        grid=(n_streams, nj),
        in_specs=[tok, tok, mat, mat, st],
        out_specs=[tok, st],
        out_shape=[jax.ShapeDtypeStruct((n, D_A), BF16),
                   jax.ShapeDtypeStruct((n_streams, N_HEADS, HEAD, HEAD), F32)],
        scratch_shapes=[pltpu.VMEM((HEAD, D_A), F32)],
        compiler_params=_params(("parallel", "arbitrary")),
        name="wkv_scan",
    )(rc, yc, m2, n2, s0)


CONV_ROWS = 128


def _conv_kernel(u_ref, hown_ref, hinit_ref, cw_ref, cb_ref, lg_ref, lb_ref, o_ref, ext_scr, sh_scr, z_scr):
    tm = u_ref.shape[0]
    n_lt = D_B // LANE
    first = pl.program_id(1) == 0

    @pl.when(first)
    def _():
        for l in range(n_lt):
            ext_scr[l, 0:HALO, :] = hinit_ref[0, :, l * LANE:(l + 1) * LANE]

    @pl.when(jnp.logical_not(first))
    def _():
        for l in range(n_lt):
            ext_scr[l, 0:HALO, :] = hown_ref[:, l * LANE:(l + 1) * LANE]

    for l in range(n_lt):
        ext_scr[l, HALO:HALO + tm, :] = u_ref[:, l * LANE:(l + 1) * LANE]
    span = sh_scr.shape[1]
    lead = HALO - (CONV_W - 1)

    def lane_tile(l, carry):
        for s in range(1, SUBLANE):
            sh_scr[s - 1] = ext_scr[l, s:s + span, :]
        acc = jnp.broadcast_to(cb_ref[l], (tm, LANE))
        for j in range(CONV_W):
            s, q = (lead + j) % SUBLANE, (lead + j) // SUBLANE
            r0 = q * SUBLANE
            win = ext_scr[l, r0:r0 + tm, :] if s == 0 else sh_scr[s - 1, r0:r0 + tm, :]
            acc = acc + cw_ref[l, j:j + 1, :] * win
        z_scr[l] = acc
        return carry

    lax.fori_loop(0, n_lt, lane_tile, 0)
    rows = min(CONV_ROWS, tm)

    def norm(i, carry):
        r0 = pl.multiple_of(i * rows, rows)
        z = jnp.concatenate([z_scr[l, pl.ds(r0, rows), :] for l in range(n_lt)], axis=1)
        mu = jnp.mean(z, axis=-1, keepdims=True)
        d = z - mu
        var = jnp.mean(d * d, axis=-1, keepdims=True)
        zn = d * lax.rsqrt(var + LN_EPS) * lg_ref[...] + lb_ref[...]
        o_ref[pl.ds(r0, rows), :] = (zn * jax.nn.sigmoid(zn)).astype(BF16)
        return carry

    lax.fori_loop(0, tm // rows, norm, 0)


def _conv(u, hinit, cw, cb, lg, lb, *, n_streams, t_len, tm):
    n = u.shape[0]
    tps = t_len // tm
    if tps > 1:
        blocks_per_tile = tm // HALO
        own_map = lambda b, i: (jnp.maximum((b * tps + i) * blocks_per_tile - 1, 0), 0)
    else:
        own_map = lambda b, i: (0, 0)
    return pl.pallas_call(
        _conv_kernel,
        grid=(n_streams, tps),
        in_specs=[
            pl.BlockSpec((tm, D_B), lambda b, i: (b * tps + i, 0)),
            pl.BlockSpec((HALO, D_B), own_map),
            pl.BlockSpec((1, HALO, D_B), lambda b, i: (b, 0, 0)),
            _const_spec((D_B // LANE, HALO, LANE)),
            _const_spec((D_B // LANE, 1, LANE)),
            _const_spec((1, D_B)),
            _const_spec((1, D_B)),
        ],
        out_specs=pl.BlockSpec((tm, D_B), lambda b, i: (b * tps + i, 0)),
        out_shape=jax.ShapeDtypeStruct((n, D_B), BF16),
        scratch_shapes=[pltpu.VMEM((D_B // LANE, HALO + tm, LANE), F32),
                        pltpu.VMEM((SUBLANE - 1, HALO + tm - SUBLANE, LANE), F32),
                        pltpu.VMEM((D_B // LANE, tm, LANE), F32)],
        compiler_params=_params(("parallel", "arbitrary")),
        name="conv",
    )(u, u, hinit, cw, cb, lg, lb)


def _merge_kernel(x_ref, y_ref, bonus_ref, g_ref, ob_ref, n1_ref, wg_ref, bg_ref,
                  lng_ref, lnb_ref, woa_ref, wob_ref, o_ref):
    h = _rms(x_ref[...], n1_ref[...]).astype(BF16)
    gate = jax.nn.sigmoid(_dot(h, wg_ref[...]) + bg_ref[...])
    y = y_ref[...].astype(F32)
    d = y - _head_sum(y) * (1.0 / HEAD)
    var = _head_sum(d * d) * (1.0 / HEAD)
    yn = d * lax.rsqrt(var + GN_EPS) * lng_ref[...] + lnb_ref[...]
    oa = ((yn + bonus_ref[...].astype(F32)) * g_ref[...].astype(F32)).astype(BF16)
    merged = (gate[:, :D_MODEL] * _dot(oa, woa_ref[...])
              + gate[:, D_MODEL:] * _dot(ob_ref[...], wob_ref[...]))
    o_ref[...] = merged.astype(BF16)


def _merge(x, y, bonus, g, ob, n1, wg, bg, lng, lnb, woa, wob):
    n = x.shape[0]
    tok_a = pl.BlockSpec((TM, D_A), lambda i: (i, 0))
    tok_m = pl.BlockSpec((TM, D_MODEL), lambda i: (i, 0))
    return pl.pallas_call(
        _merge_kernel,
        grid=(n // TM,),
        in_specs=[tok_m, tok_a, tok_a, tok_a, tok_a,
                  _const_spec((1, D_MODEL)),
                  _const_spec((D_MODEL, 2 * D_MODEL)),
                  _const_spec((1, 2 * D_MODEL)),
                  _const_spec((1, D_A)),
                  _const_spec((1, D_A)),
                  _const_spec((D_A, D_MODEL)),
                  _const_spec((D_B, D_MODEL))],
        out_specs=tok_m,
        out_shape=jax.ShapeDtypeStruct((n, D_MODEL), BF16),
        compiler_params=_params(("parallel",)),
        name="merge",
    )(x, y, bonus, g, ob, n1, wg, bg, lng, lnb, woa, wob)


def _pack_bf16_pairs(x):
    w = x.shape[1] // 2
    bits = pltpu.bitcast(x.astype(BF16).astype(F32), jnp.uint32)
    return (bits[:, w:] & jnp.uint32(0xFFFF0000)) | (bits[:, :w] >> jnp.uint32(16))


def _unpack_bf16_pairs(u):
    lo = pltpu.bitcast(u << jnp.uint32(16), F32)
    hi = pltpu.bitcast(u & jnp.uint32(0xFFFF0000), F32)
    return jnp.concatenate([lo, hi], axis=1).astype(BF16)


def _outproj_kernel(m_ref, x_ref, wo_ref, n2_ref, wrh_ref, wrl_ref, brt_ref, x1_ref, h2_ref, lg_ref):
    x1 = x_ref[...] + _dot(m_ref[...], wo_ref[...])
    x1_ref[...] = x1
    h2 = _rms(x1, n2_ref[...])
    h2_ref[...] = _pack_bf16_pairs(h2)
    lg_ref[...] = _dot3(h2, wrh_ref[...], wrl_ref[...]) + brt_ref[...]


def _outproj(m, x, wo, n2, wrh, wrl, brt):
    n = x.shape[0]
    tok_m = pl.BlockSpec((TM, D_MODEL), lambda i: (i, 0))
    return pl.pallas_call(
        _outproj_kernel,
        grid=(n // TM,),
        in_specs=[tok_m, tok_m,
                  _const_spec((D_MODEL, D_MODEL)),
                  _const_spec((1, D_MODEL)),
                  _const_spec((D_MODEL, ROUTE_W)),
                  _const_spec((D_MODEL, ROUTE_W)),
                  _const_spec((1, ROUTE_W))],
        out_specs=[tok_m, pl.BlockSpec((TM, D_MODEL // 2), lambda i: (i, 0)),
                   pl.BlockSpec((TM, ROUTE_W), lambda i: (i, 0))],
        out_shape=[jax.ShapeDtypeStruct((n, D_MODEL), F32),
                   jax.ShapeDtypeStruct((n, D_MODEL // 2), jnp.uint32),
                   jax.ShapeDtypeStruct((n, ROUTE_W), F32)],
        compiler_params=_params(("parallel",)),
        name="outproj",
    )(m, x, wo, n2, wrh, wrl, brt)


def _route_kernel(lg_ref, ri_ref, rw_ref, cnt_ref, carry_scr):
    i = pl.program_id(0)
    tm = lg_ref.shape[0]

    @pl.when(i == 0)
    def _():
        carry_scr[...] = jnp.zeros_like(carry_scr)

    lg = lg_ref[...]
    lane = lax.broadcasted_iota(jnp.int32, (tm, ROUTE_W), 1)
    neg = jnp.float32(-jnp.inf)
    big = jnp.int32(ROUTE_W)
    grp = jnp.where(lane < N_GROUPS, lg, neg)
    gmax = jnp.max(grp, axis=1, keepdims=True)
    g_sel = jnp.min(jnp.where(grp == gmax, lane, big), axis=1, keepdims=True)
    p_grp = 1.0 / jnp.sum(jnp.where(lane < N_GROUPS, jnp.exp(lg - gmax), 0.0), axis=1, keepdims=True)
    lo = N_GROUPS + g_sel * EXP_PER_GROUP
    le = jnp.where((lane >= lo) & (lane < lo + EXP_PER_GROUP), lg, neg)
    m1 = jnp.max(le, axis=1, keepdims=True)
    i1 = jnp.min(jnp.where(le == m1, lane, big), axis=1, keepdims=True)
    le2 = jnp.where(lane == i1, neg, le)
    m2 = jnp.max(le2, axis=1, keepdims=True)
    i2 = jnp.min(jnp.where(le2 == m2, lane, big), axis=1, keepdims=True)
    e2 = jnp.exp(m2 - m1)
    wa = p_grp / (1.0 + e2)
    wb = p_grp * e2 / (1.0 + e2)
    ea = i1 - N_GROUPS
    eb = i2 - N_GROUPS
    onehot = jnp.where((lane == ea) | (lane == eb), 1.0, 0.0)
    ri_ = lax.broadcasted_iota(jnp.int32, (tm, tm), 0)
    ci_ = lax.broadcasted_iota(jnp.int32, (tm, tm), 1)
    below = jnp.where(ci_ < ri_, 1.0, 0.0).astype(BF16)
    before = _dot(below, onehot.astype(BF16)) + carry_scr[...]
    rank_a = jnp.sum(jnp.where(lane == ea, before, 0.0), axis=1, keepdims=True).astype(jnp.int32)
    rank_b = jnp.sum(jnp.where(lane == eb, before, 0.0), axis=1, keepdims=True).astype(jnp.int32)
    carry_scr[...] = carry_scr[...] + jnp.sum(onehot, axis=0, keepdims=True)
    ri = jnp.where(lane == 0, ea, jnp.where(lane == 1, eb,
                   jnp.where(lane == 2, rank_a, jnp.where(lane == 3, rank_b, 0))))
    ri_ref[0] = ri.T[0:SUBLANE, :]
    rw_ref[...] = jnp.where(lane == 0, wa, jnp.where(lane == 1, wb, 0.0))
    cnt_ref[...] = jnp.broadcast_to(carry_scr[...], cnt_ref.shape)


def _route(lg):
    n = lg.shape[0]
    tok = pl.BlockSpec((TM, ROUTE_W), lambda i: (i, 0))
    return pl.pallas_call(
        _route_kernel,
        grid=(n // TM,),
        in_specs=[tok],
        out_specs=[pl.BlockSpec((1, SUBLANE, TM), lambda i: (i, 0, 0)), tok,
                   pl.BlockSpec((8, ROUTE_W), lambda i: (0, 0))],
        out_shape=[jax.ShapeDtypeStruct((n // TM, SUBLANE, TM), jnp.int32),
                   jax.ShapeDtypeStruct((n, ROUTE_W), F32),
                   jax.ShapeDtypeStruct((8, ROUTE_W), F32)],
        scratch_shapes=[pltpu.VMEM((1, ROUTE_W), F32)],
        compiler_params=_params(("arbitrary",)),
        name="route",
    )(lg)


def _dispatch_kernel(pos_ref, h_ref, xs_in_ref, xs_ref, sem):
    del xs_in_ref
    tm = h_ref.shape[0]

    def copy(r, slot):
        return pltpu.make_async_copy(h_ref.at[pl.ds(r, 1)],
                                     xs_ref.at[pl.ds(pos_ref[0, slot, r], 1)], sem)

    def start(r):
        copy(r, 0).start(priority=0)
        copy(r, 1).start(priority=1)

    def wait(r):
        copy(r, 0).wait()
        copy(r, 1).wait()

    for r in range(tm):
        start(r)
    for r in range(tm):
        wait(r)


def _dispatch(pos, h2, xs, *, tile0):
    n = h2.shape[0]
    return pl.pallas_call(
        _dispatch_kernel,
        grid=(n // TM,),
        in_specs=[
            pl.BlockSpec((1, 2, TM), lambda i: (i + tile0, 0, 0), memory_space=pltpu.SMEM),
            pl.BlockSpec((TM, D_MODEL // 2), lambda i: (i, 0)),
            pl.BlockSpec(memory_space=pl.ANY),
        ],
        out_specs=pl.BlockSpec(memory_space=pl.ANY),
        out_shape=jax.ShapeDtypeStruct(xs.shape, xs.dtype),
        scratch_shapes=[pltpu.SemaphoreType.DMA(())],
        input_output_aliases={2: 0},
        compiler_params=_params(("arbitrary",)),
        name="dispatch",
    )(pos, h2, xs)


def _experts_kernel(te_ref, tv_ref, xs_ref, wg_ref, wu_ref, wd_ref, y_ref, wg_scr, wu_scr, wd_scr):
    t = pl.program_id(0)
    new_expert = jnp.logical_or(t == 0, te_ref[t] != te_ref[jnp.maximum(t - 1, 0)])

    @pl.when(new_expert)
    def _():
        wg_scr[...] = wg_ref[0].astype(BF16)
        wu_scr[...] = wu_ref[0].astype(BF16)
        wd_scr[...] = wd_ref[0].astype(BF16)

    @pl.when(tv_ref[t] == 1)
    def _():
        x = _unpack_bf16_pairs(xs_ref[...])
        gt = _dot(x, wg_scr[...])
        up = _dot(x, wu_scr[...])
        hid = (gt * jax.nn.sigmoid(gt) * up).astype(BF16)
        y_ref[...] = _pack_bf16_pairs(_dot(hid, wd_scr[...]))

    @pl.when(tv_ref[t] == 0)
    def _():
        y_ref[...] = jnp.zeros_like(y_ref)


def _experts(te, tv, xs, wg, wu, wd):
    p = xs.shape[0]
    grid_spec = pltpu.PrefetchScalarGridSpec(
        num_scalar_prefetch=2,
        grid=(p // TMX,),
        in_specs=[
            pl.BlockSpec((TMX, D_MODEL // 2), lambda t, te, tv: (t, 0)),
            pl.BlockSpec((1, D_MODEL, D_EXPERT), lambda t, te, tv: (te[t], 0, 0)),
            pl.BlockSpec((1, D_MODEL, D_EXPERT), lambda t, te, tv: (te[t], 0, 0)),
            pl.BlockSpec((1, D_EXPERT, D_MODEL), lambda t, te, tv: (te[t], 0, 0)),
        ],
        out_specs=pl.BlockSpec((TMX, D_MODEL // 2), lambda t, te, tv: (t, 0)),
        scratch_shapes=[pltpu.VMEM((D_MODEL, D_EXPERT), BF16), pltpu.VMEM((D_MODEL, D_EXPERT), BF16),
                        pltpu.VMEM((D_EXPERT, D_MODEL), BF16)],
    )
    return pl.pallas_call(
        _experts_kernel,
        grid_spec=grid_spec,
        out_shape=jax.ShapeDtypeStruct((p, D_MODEL // 2), jnp.uint32),
        compiler_params=_params(("arbitrary",)),
        name="experts",
    )(te, tv, xs, wg, wu, wd)


def _combine_kernel(pos_ref, x1_ref, rw_ref, fg_ref, ys_ref, o_ref, ya_scr, yb_scr, sem):
    tm = x1_ref.shape[0]

    def copies(r):
        return (pltpu.make_async_copy(ys_ref.at[pl.ds(pos_ref[0, 0, r], 1)], ya_scr.at[pl.ds(r, 1)], sem),
                pltpu.make_async_copy(ys_ref.at[pl.ds(pos_ref[0, 1, r], 1)], yb_scr.at[pl.ds(r, 1)], sem))

    def start(r):
        ca, cb = copies(r)
        ca.start(priority=0)
        cb.start(priority=1)

    def wait(r):
        ca, cb = copies(r)
        ca.wait()
        cb.wait()

    for r in range(tm):
        start(r)
    for r in range(tm):
        wait(r)
    rw = rw_ref[...]
    ya = _unpack_bf16_pairs(ya_scr[...]).astype(F32)
    yb = _unpack_bf16_pairs(yb_scr[...]).astype(F32)
    x2 = x1_ref[...] + rw[:, 0:1] * ya + rw[:, 1:2] * yb
    o_ref[...] = _rms(x2, fg_ref[...])


def _combine(pos, x1, rw, fg, ys, *, tile0):
    n = x1.shape[0]
    tok_m = pl.BlockSpec((TM, D_MODEL), lambda i: (i, 0))
    return pl.pallas_call(
        _combine_kernel,
        grid=(n // TM,),
        in_specs=[
            pl.BlockSpec((1, 2, TM), lambda i: (i + tile0, 0, 0), memory_space=pltpu.SMEM),
            tok_m,
            pl.BlockSpec((TM, ROUTE_W), lambda i: (i + tile0, 0)),
            _const_spec((1, D_MODEL)),
            pl.BlockSpec(memory_space=pl.ANY),
        ],
        out_specs=tok_m,
        out_shape=jax.ShapeDtypeStruct((n, D_MODEL), F32),
        scratch_shapes=[pltpu.VMEM((TM, D_MODEL // 2), jnp.uint32),
                        pltpu.VMEM((TM, D_MODEL // 2), jnp.uint32),
                        pltpu.SemaphoreType.DMA(())],
        compiler_params=_params(("arbitrary",)),
        name="combine",
    )(pos, x1, rw, fg, ys)


def _row(v):
    return v.reshape(1, -1).astype(F32)


def _prepare_weights(norm1_g, w_in, b_conv_in, b_gate, mu_shift, w0, w_lora2, a0, a_lora2, g_lora2,
                     k_k, k_a, r_k, lnx_g, lnx_b, w_out_a, conv_w, conv_b, cln_g, cln_b, w_out_b, w_o,
                     norm2_g, w_rg, b_rg, w_re, b_re, w_gate_e, w_up_e, w_down_e, final_g):
    w = {}
    w["n1"] = _row(norm1_g)
    w["wr"] = _permute_cols(w_in[:, :SHIFT_W]).astype(BF16)
    w["wc"] = w_in[:, SHIFT_W:SHIFT_W + 2 * D_B].astype(BF16)
    w["wg"] = w_in[:, SHIFT_W + 2 * D_B:].astype(BF16)
    w["bc"] = _row(b_conv_in)
    w["bg"] = _row(b_gate)
    w["mu"] = _permute_cols(mu_shift).reshape(1, P_W)
    w["vp"] = jnp.concatenate([jnp.stack([w0, a0, k_k, k_a, r_k.reshape(-1)]),
                               jnp.zeros((3, D_A), F32)], axis=0)
    z = jnp.zeros((LORA_W, D_A), F32)
    w2 = jnp.concatenate([jnp.concatenate([w_lora2, z], axis=1),
                          jnp.concatenate([z, a_lora2], axis=1)], axis=0)
    w["w2h"], w["w2l"] = _split(w2)
    g2 = jnp.pad(g_lora2, ((0, 2 * LANE - LORA_G), (0, 0)))
    w["g2h"], w["g2l"] = _split(g2)
    w["lng"] = _row(lnx_g)
    w["lnb"] = _row(lnx_b)
    w["woa"] = w_out_a.astype(BF16)
    w["wob"] = w_out_b.astype(BF16)
    w["cw"] = jnp.pad(conv_w, ((0, HALO - CONV_W), (0, 0))).reshape(HALO, D_B // LANE, LANE).transpose(1, 0, 2)
    w["cb"] = conv_b.reshape(D_B // LANE, 1, LANE)
    w["clg"] = _row(cln_g)
    w["clb"] = _row(cln_b)
    w["wo"] = w_o.astype(BF16)
    w["n2"] = _row(norm2_g)
    pad_r = ROUTE_W - N_GROUPS - N_EXPERTS
    w["wrh"], w["wrl"] = _split(jnp.pad(jnp.concatenate([w_rg, w_re], axis=1), ((0, 0), (0, pad_r))))
    w["brt"] = jnp.pad(jnp.concatenate([b_rg, b_re]), (0, pad_r)).reshape(1, ROUTE_W)
    w["wge"] = w_gate_e
    w["wue"] = w_up_e
    w["wde"] = w_down_e
    w["fg"] = _row(final_g)
    return w


def _trunk(x, shift0, wkv0, conv0, w, *, chunk, nchunk, intra_chunks, conv_tm):
    b, t, _ = x.shape
    n = b * t
    xf = x.reshape(n, D_MODEL)
    shift_p = _permute_cols(shift0)
    tps = max(t // TM, 1)
    nb, step = (1, TM) if t >= TM else (TM // t, t)
    u, tails, rt, at, bt, kt, vv, ge, bonus, g = _inprep(
        xf, w["n1"], w["wr"], w["wc"], w["bc"], shift_p.reshape(-1, nb, P_W), w["mu"], w["vp"],
        w["w2h"], w["w2l"], w["g2h"], w["g2l"], chunk=chunk, nb=nb, step=step, tps=tps)
    rc, yc, m2, n2 = _wkv_intra(rt, at, bt, kt, vv, ge, chunk=chunk, nchunk=intra_chunks)
    y, wkv_new = _wkv_scan(rc, yc, m2, n2, wkv0, n_streams=b, t_len=t, chunk=chunk, nchunk=nchunk)
    hinit = jnp.pad(conv0, ((0, 0), (HALO - (CONV_W - 1), 0), (0, 0)))
    ob = _conv(u, hinit, w["cw"], w["cb"], w["clg"], w["clb"], n_streams=b, t_len=t, tm=conv_tm)
    merged = _merge(xf, y, bonus, g, ob, w["n1"], w["wg"], w["bg"], w["lng"], w["lnb"], w["woa"], w["wob"])
    x1, h2, lg = _outproj(merged, xf, w["wo"], w["n2"], w["wrh"], w["wrl"], w["brt"])
    new_shift = tails.reshape(b, -1, P_W)[:, -1, :SHIFT_W][:, _INV_PERM]
    ext_tail = jnp.concatenate([conv0, u.reshape(b, t, D_B)[:, -(CONV_W - 1):]], axis=1)[:, -(CONV_W - 1):]
    return x1, h2, lg, new_shift, wkv_new, ext_tail


def kernel(x_prompt, x_sample, state_shift, state_wkv, cache_conv, norm1_g, w_in, b_conv_in, b_gate,
           mu_shift, w0, w_lora2, a0, a_lora2, g_lora2, k_k, k_a, r_k, lnx_g, lnx_b, w_out_a, conv_w,
           conv_b, cln_g, cln_b, w_out_b, w_o, norm2_g, w_rg, b_rg, w_re, b_re, w_gate_e, w_up_e,
           w_down_e, final_g):
    per_layer = (norm1_g, w_in, b_conv_in, b_gate, mu_shift, w0, w_lora2, a0, a_lora2, g_lora2, k_k, k_a,
                 r_k, lnx_g, lnx_b, w_out_a, conv_w, conv_b, cln_g, cln_b, w_out_b, w_o, norm2_g, w_rg,
                 b_rg, w_re, b_re, w_gate_e, w_up_e, w_down_e)
    assert norm1_g.shape[0] == 1, "single layer only"
    w = _prepare_weights(*[p[0] for p in per_layer], final_g)
    bp, tp, _ = x_prompt.shape
    bs, ts, _ = x_sample.shape
    zero_shift = jnp.zeros((bp, SHIFT_W), F32)
    zero_wkv = jnp.zeros((bp, N_HEADS, HEAD, HEAD), F32)
    zero_conv = jnp.zeros((bp, CONV_W - 1, D_B), F32)
    x1p, h2p, lgp, shift_p, wkv_p, conv_p = _trunk(
        x_prompt, zero_shift, zero_wkv, zero_conv, w, chunk=64, nchunk=min(8, tp // 64),
        intra_chunks=min(16, tp // 64), conv_tm=TM)
    x1s, h2s, lgs, shift_s, wkv_s, conv_s = _trunk(
        x_sample, state_shift[0], state_wkv[0], cache_conv[0], w, chunk=ts, nchunk=1,
        intra_chunks=TM // ts, conv_tm=ts)

    n_p, n_s = bp * tp, bs * ts
    n_all = n_p + n_s
    ri, rw, cnt = _route(jnp.concatenate([lgp, lgs], axis=0))
    counts = cnt[0, :N_EXPERTS].astype(jnp.int32)
    tiles_per = (counts + TMX - 1) // TMX
    tile_end = jnp.cumsum(tiles_per)
    offs = (tile_end - tiles_per) * TMX
    n_tiles = (2 * n_all + N_EXPERTS * (TMX - 1) + TMX - 1) // TMX
    t_idx = jnp.arange(n_tiles, dtype=jnp.int32)
    te = jnp.sum(t_idx[:, None] >= tile_end[None, :], axis=1).astype(jnp.int32)
    tv = (t_idx < tile_end[-1]).astype(jnp.int32)
    e_ids = jnp.arange(N_EXPERTS, dtype=jnp.int32)
    last_e = jnp.max(jnp.where(tiles_per > 0, e_ids, 0))
    te = jnp.where(tv == 1, jnp.minimum(te, N_EXPERTS - 1), last_e)
    base = jnp.sum(jnp.where(ri[:, 0:2, :, None] == e_ids, offs, 0), axis=-1)
    pos = base + ri[:, 2:4, :]
    xs = jnp.zeros((n_tiles * TMX, D_MODEL // 2), jnp.uint32)
    xs = _dispatch(pos, h2p, xs, tile0=0)
    xs = _dispatch(pos, h2s, xs, tile0=n_p // TM)
    ys = _experts(te, tv, xs, w["wge"], w["wue"], w["wde"])
    yp = _combine(pos, x1p, rw, w["fg"], ys, tile0=0)
    ysm = _combine(pos, x1s, rw, w["fg"], ys, tile0=n_p // TM)
    return (yp.reshape(bp, tp, D_MODEL), ysm.reshape(bs, ts, D_MODEL),
            shift_p[None], wkv_p[None], conv_p[None],
            shift_s[None], wkv_s[None], conv_s[None])
```

```python
import functools
import math

import numpy as np
import jax
import jax.numpy as jnp
from jax import lax
from jax.experimental import pallas as pl
from jax.experimental.pallas import tpu as pltpu

F32 = jnp.float32
BF16 = jnp.bfloat16

D_MODEL = 2048
D_A = 1024
HEAD = 64
N_HEADS = D_A // HEAD
LORA_W = 64
LORA_A = 64
LORA_G = 160
SHIFT_W = 3 * D_A + LORA_W + LORA_A + LORA_G
D_B = 1024
CONV_W = 31
N_GROUPS = 4
EXP_PER_GROUP = 8
N_EXPERTS = N_GROUPS * EXP_PER_GROUP
D_EXPERT = 512
RMS_EPS = 1e-6
LN_EPS = 1e-5
GN_EPS = 64e-5

LANE = 128
SUBLANE = 8
P_W = 27 * LANE
LORA_IN0 = 3 * D_A
XG0 = LORA_IN0 + LANE
HALO = 32
ROUTE_W = LANE
TM = 256
TM_WIDE = 512
TMX = 256
VMEM_LIMIT = 56 * 1024 * 1024

_PERM_RANGES = ((0, D_A), (D_A + LORA_W, 2 * D_A + LORA_W), (2 * D_A + LORA_W, 3 * D_A + LORA_W),
                (D_A, D_A + LORA_W), (3 * D_A + LORA_W, SHIFT_W))
_PERM = np.concatenate([np.arange(a, b) for a, b in _PERM_RANGES])
_INV_PERM = np.argsort(_PERM)
assert len(_PERM) == SHIFT_W and (np.sort(_PERM) == np.arange(SHIFT_W)).all()


def _permute_cols(x):
    pad = jnp.zeros(x.shape[:-1] + (P_W - SHIFT_W,), x.dtype)
    return jnp.concatenate([x[..., a:b] for a, b in _PERM_RANGES] + [pad], axis=-1)


def _dot(a, b):
    return jnp.dot(a, b, preferred_element_type=F32)


def _dot_nt(a, b):
    return lax.dot_general(a, b, (((1,), (1,)), ((), ())), preferred_element_type=F32)


def _dot_tn(a, b):
    return lax.dot_general(a, b, (((0,), (0,)), ((), ())), preferred_element_type=F32)


def _split(x):
    hi = x.astype(BF16)
    lo = (x - hi.astype(F32)).astype(BF16)
    return hi, lo


def _dot3(x, w_hi, w_lo):
    x_hi, x_lo = _split(x)
    return _dot(x_hi, w_hi) + (_dot(x_lo, w_hi) + _dot(x_hi, w_lo))


def _rms(x, g):
    return x * lax.rsqrt(jnp.mean(x * x, axis=-1, keepdims=True) + RMS_EPS) * g


def _head_sum(x):
    w = 2 * LANE
    r = lax.broadcasted_iota(jnp.int32, (w, w), 0) // HEAD
    c = lax.broadcasted_iota(jnp.int32, (w, w), 1) // HEAD
    ones = jnp.where(r == c, 1.0, 0.0).astype(BF16)
    outs = [_dot(x[:, j * w:(j + 1) * w].astype(BF16), ones) for j in range(x.shape[1] // w)]
    return jnp.concatenate(outs, axis=1)


def _const_spec(shape):
    nd = len(shape)
    return pl.BlockSpec(shape, lambda *_: (0,) * nd, pipeline_mode=pl.Buffered(1))


def _params(sem):
    return pltpu.CompilerParams(dimension_semantics=sem, vmem_limit_bytes=VMEM_LIMIT)


def _inprep_kernel(x_ref, n1_ref, wr_ref, wc_ref, bc_ref, bnd_ref, mu_ref, vp_ref, w2h_ref, w2l_ref,
                   g2h_ref, g2l_ref,
                   u_ref, tail_ref, rt_ref, at_ref, bt_ref, kt_ref, v_ref, ge_ref, bonus_ref, g_ref,
                   first_scr, last_scr, *, chunk, nb, step, tps):
    tm = x_ref.shape[0]
    if nb == 1:
        starts_stream = lax.rem(pl.program_id(0), tps) == 0

        @pl.when(starts_stream)
        def _():
            first_scr[...] = bnd_ref[0, 0:1, :]

        @pl.when(jnp.logical_not(starts_stream))
        def _():
            first_scr[...] = last_scr[...]

    h = _rms(x_ref[...], n1_ref[...]).astype(BF16)
    p = _dot(h, wr_ref[...])
    for j in range(nb):
        tail_ref[0, j:j + 1, :] = p[(j + 1) * step - 1:(j + 1) * step, :]
    c = _dot(h, wc_ref[...]) + bc_ref[...]
    u_ref[...] = c[:, :D_B] * jax.nn.sigmoid(c[:, D_B:])

    row = lax.broadcasted_iota(jnp.int32, (tm, 1), 0)
    prev = pltpu.roll(p, 1, axis=0)
    if nb == 1:
        prev = jnp.where(row == 0, first_scr[...], prev)
        last_scr[...] = p[tm - 1:tm, :]
    else:
        for j in range(nb):
            prev = jnp.where(row == j * step, bnd_ref[0, j:j + 1, :], prev)
    xs = p + (prev - p) * mu_ref[...]
    r = xs[:, 0:D_A]
    k = xs[:, D_A:2 * D_A]
    v = xs[:, 2 * D_A:3 * D_A]
    lin = xs[:, LORA_IN0:LORA_IN0 + LANE]
    lane = lax.broadcasted_iota(jnp.int32, (1, LANE), 1)
    lin = jnp.where(lane < LORA_W, jnp.tanh(lin), lin)
    lo = _dot3(lin, w2h_ref[...], w2l_ref[...])
    w0 = vp_ref[0:1, :]
    a0 = vp_ref[1:2, :]
    k_k = vp_ref[2:3, :]
    k_a = vp_ref[3:4, :]
    r_k = vp_ref[4:5, :]
    zw = -(w0 + lo[:, :D_A])
    softplus = jnp.maximum(zw, 0.0) + jnp.log1p(jnp.exp(-jnp.abs(zw)))
    logw = -jnp.exp(-softplus - 0.5)
    a = jax.nn.sigmoid(a0 + lo[:, D_A:])
    g = _dot3(jax.nn.sigmoid(xs[:, XG0:XG0 + 2 * LANE]), g2h_ref[...], g2l_ref[...])
    kku = k * k_k
    kk = kku / jnp.maximum(jnp.sqrt(_head_sum(kku * kku)), 1e-12)
    k2 = k * (1.0 + (a - 1.0) * k_a)
    bonus_ref[...] = (_head_sum(r * k2 * r_k) * v).astype(BF16)
    g_ref[...] = g.astype(BF16)
    ri = lax.broadcasted_iota(jnp.int32, (tm, tm), 0)
    ci = lax.broadcasted_iota(jnp.int32, (tm, tm), 1)
    tri = jnp.where((ri // chunk == ci // chunk) & (ci <= ri), 1.0, 0.0).astype(BF16)
    lw_hi, lw_lo = _split(logw)
    cum = _dot(tri, lw_hi) + _dot(tri, lw_lo)
    eg = jnp.exp(cum)
    egi = jnp.exp(-cum)
    rt_ref[...] = (r * eg).astype(BF16)
    at_ref[...] = (-kk * jnp.exp(cum - logw)).astype(BF16)
    bt_ref[...] = (kk * a * egi).astype(BF16)
    kt_ref[...] = (k2 * egi).astype(BF16)
    v_ref[...] = v.astype(BF16)
    for cc in range(tm // chunk):
        ge_ref[cc] = eg[(cc + 1) * chunk - 1:(cc + 1) * chunk, :]


def _inprep(x, n1, wr, wc, bc, bnd, mu, vp, w2h, w2l, g2h, g2l, *, chunk, nb, step, tps):
    n = x.shape[0]
    tok = pl.BlockSpec((TM, D_A), lambda i: (i, 0))
    return pl.pallas_call(
        functools.partial(_inprep_kernel, chunk=chunk, nb=nb, step=step, tps=tps),
        grid=(n // TM,),
        in_specs=[
            pl.BlockSpec((TM, D_MODEL), lambda i: (i, 0)),
            _const_spec((1, D_MODEL)),
            _const_spec((D_MODEL, P_W)),
            _const_spec((D_MODEL, 2 * D_B)),
            _const_spec((1, 2 * D_B)),
            pl.BlockSpec((1, nb, P_W), lambda i: (i // tps, 0, 0)),
            _const_spec((1, P_W)),
            _const_spec((8, D_A)),
            _const_spec((LANE, 2 * D_A)),
            _const_spec((LANE, 2 * D_A)),
            _const_spec((2 * LANE, D_A)),
            _const_spec((2 * LANE, D_A)),
        ],
        out_specs=[tok,
                   pl.BlockSpec((1, nb, P_W), lambda i: (i, 0, 0)),
                   tok, tok, tok, tok, tok,
                   pl.BlockSpec((TM // chunk, 1, D_A), lambda i: (i, 0, 0)),
                   tok, tok],
        out_shape=[jax.ShapeDtypeStruct((n, D_B), F32),
                   jax.ShapeDtypeStruct((n // TM, nb, P_W), F32)]
        + [jax.ShapeDtypeStruct((n, D_A), BF16)] * 5
        + [jax.ShapeDtypeStruct((n // chunk, 1, D_A), F32),
           jax.ShapeDtypeStruct((n, D_A), BF16),
           jax.ShapeDtypeStruct((n, D_A), BF16)],
        scratch_shapes=[pltpu.VMEM((1, P_W), F32), pltpu.VMEM((1, P_W), F32)],
        compiler_params=_params(("arbitrary",)),
        name="inprep",
    )(x, n1, wr, wc, bc, bnd, mu, vp, w2h, w2l, g2h, g2l)


def _wkv_intra_kernel(rt_ref, at_ref, bt_ref, kt_ref, v_ref, ge_ref, rc_ref, yc_ref, m2_ref, n2_ref,
                      *, chunk, nchunk):
    c = chunk
    w2 = 2 * c
    row = lax.broadcasted_iota(jnp.int32, (c, w2), 0)
    col = lax.broadcasted_iota(jnp.int32, (c, w2), 1)
    colc = jnp.where(col >= c, col - c, col)
    strict = colc < row
    incl = colc <= row
    eye_c = jnp.where(colc == row, 1.0, 0.0).astype(F32)
    tok0 = lax.broadcasted_iota(jnp.int32, (1, w2), 1) < c
    ch0 = lax.broadcasted_iota(jnp.int32, (1, LANE), 1) < HEAD
    rk = lax.broadcasted_iota(jnp.int32, (HEAD, LANE), 0)
    ck = lax.broadcasted_iota(jnp.int32, (HEAD, LANE), 1)
    eye_k = jnp.where(jnp.where(ck >= HEAD, ck - HEAD, ck) == rk, 1.0, 0.0).astype(F32)
    n_sq = int(math.log2(c)) - 1

    def bd(x, first):
        zero = jnp.zeros_like(x)
        return jnp.concatenate([jnp.where(first, x, zero), jnp.where(first, zero, x)], axis=0)

    def head_blocks(x):
        return [jnp.where(ch0, x[:HEAD, j * LANE:(j + 1) * LANE], x[HEAD:, j * LANE:(j + 1) * LANE])
                for j in range(x.shape[1] // LANE)]

    rng = range(nchunk)
    rows = [slice(ci * c, (ci + 1) * c) for ci in rng]
    at = [at_ref[r, :] for r in rows]
    rt = [rt_ref[r, :] for r in rows]
    bt = [bt_ref[r, :] for r in rows]
    kt = [kt_ref[r, :] for r in rows]
    vv = [v_ref[r, :] for r in rows]
    m = [_dot_nt(jnp.concatenate([at[i], rt[i]], axis=0),
                 jnp.concatenate([bd(bt[i], ch0), bd(kt[i], ch0)], axis=0)) for i in rng]
    a_ab = [jnp.where(strict, x[:c, :w2], 0.0) for x in m]
    a_rb = [jnp.where(incl, x[c:, :w2], 0.0).astype(BF16) for x in m]
    akv = [_dot(jnp.concatenate([jnp.where(strict, x[:c, w2:], 0.0), jnp.where(incl, x[c:, w2:], 0.0)],
                                axis=0).astype(BF16), bd(vv[i], ch0)) for i, x in enumerate(m)]
    kv = [head_blocks(_dot_tn(kt[i], vv[i]))[0] for i in rng]
    t = [eye_c + x for x in a_ab]
    pw = [x.astype(BF16) for x in a_ab]
    pw = [_dot(x, bd(x, tok0)).astype(BF16) for x in pw]
    for _ in range(n_sq - 1):
        st = [_dot(jnp.concatenate([t[i].astype(BF16), pw[i]], axis=0), bd(pw[i], tok0)) for i in rng]
        t = [t[i] + st[i][:c] for i in rng]
        pw = [x[c:].astype(BF16) for x in st]
    t = [t[i] + _dot(t[i].astype(BF16), bd(pw[i], tok0)) for i in rng]
    twb = [_dot(t[i].astype(BF16),
                jnp.concatenate([bd(at[i], ch0), bd(akv[i][:c].astype(BF16), ch0)], axis=1)).astype(BF16)
           for i in rng]
    rb = [_dot(a_rb[i], jnp.concatenate([bd(twb[i][:, :LANE], ch0), bd(twb[i][:, LANE:], ch0)], axis=1))
          for i in rng]
    xb = [head_blocks(_dot_tn(bt[i], twb[i])) for i in rng]
    for i in rng:
        rc_ref[rows[i], :] = (rt[i].astype(F32) + rb[i][:, :LANE]).astype(BF16)
        yc_ref[rows[i], :] = rb[i][:, LANE:] + akv[i][c:]
        g_full = jnp.broadcast_to(ge_ref[i], (LANE, LANE)).T
        g_col = jnp.where(ch0, g_full[:HEAD], g_full[HEAD:])
        m2_ref[i] = (g_col * (eye_k + xb[i][0]) - eye_k).astype(BF16)
        n2_ref[i] = g_col * (xb[i][1] + kv[i])


def _wkv_intra(rt, at, bt, kt, v, ge, *, chunk, nchunk):
    n = rt.shape[0]
    rows = chunk * nchunk
    tok = pl.BlockSpec((rows, LANE), lambda i, hp: (i, hp))
    mat = pl.BlockSpec((nchunk, HEAD, LANE), lambda i, hp: (i, 0, hp))
    return pl.pallas_call(
        functools.partial(_wkv_intra_kernel, chunk=chunk, nchunk=nchunk),
        grid=(n // rows, N_HEADS // 2),
        in_specs=[tok, tok, tok, tok, tok,
                  pl.BlockSpec((nchunk, 1, LANE), lambda i, hp: (i, 0, hp))],
        out_specs=[tok, tok, mat, mat],
        out_shape=[jax.ShapeDtypeStruct((n, D_A), BF16),
                   jax.ShapeDtypeStruct((n, D_A), F32),
                   jax.ShapeDtypeStruct((n // chunk, HEAD, D_A), BF16),
                   jax.ShapeDtypeStruct((n // chunk, HEAD, D_A), F32)],
        compiler_params=_params(("parallel", "parallel")),
        name="wkv_intra",
    )(rt, at, bt, kt, v, ge)


def _wkv_scan_kernel(rc_ref, yc_ref, m2_ref, n2_ref, s0_ref, y_ref, st_ref, h_scr, *, chunk, nchunk):
    j = pl.program_id(1)

    @pl.when(j == 0)
    def _():
        for h in range(N_HEADS):
            h_scr[:, h * HEAD:(h + 1) * HEAD] = s0_ref[0, h].T

    c = chunk

    def chunk_body(ci, carry):
        off = pl.multiple_of(ci * c, c)
        h_all = h_scr[...]
        hb = h_all.astype(BF16)
        rc = rc_ref[pl.ds(off, c), :]
        m2 = m2_ref[ci]
        ch0 = lax.broadcasted_iota(jnp.int32, (1, LANE), 1) < HEAD
        zero = jnp.zeros((HEAD, LANE), BF16)
        ys, mh = [], []
        for hp in range(N_HEADS // 2):
            sl = slice(hp * LANE, (hp + 1) * LANE)
            hpair = hb[:, sl]
            hbd = jnp.concatenate([jnp.where(ch0, hpair, zero), jnp.where(ch0, zero, hpair)], axis=0)
            both = _dot(jnp.concatenate([rc[:, sl], m2[:, sl]], axis=0), hbd)
            ys.append(both[:c])
            mh.append(both[c:])
        y_ref[pl.ds(off, c), :] = (jnp.concatenate(ys, axis=1) + yc_ref[pl.ds(off, c), :]).astype(BF16)
        h_scr[...] = h_all + jnp.concatenate(mh, axis=1) + n2_ref[ci]
        return carry

    lax.fori_loop(0, nchunk, chunk_body, 0)

    @pl.when(j == pl.num_programs(1) - 1)
    def _():
        for h in range(N_HEADS):
            st_ref[0, h] = h_scr[:, h * HEAD:(h + 1) * HEAD].T


def _wkv_scan(rc, yc, m2, n2, s0, *, n_streams, t_len, chunk, nchunk):
    n = rc.shape[0]
    rows = chunk * nchunk
    nj = t_len // rows
    tok = pl.BlockSpec((rows, D_A), lambda b, j: (b * nj + j, 0))
    mat = pl.BlockSpec((nchunk, HEAD, D_A), lambda b, j: (b * nj + j, 0, 0))
    st = pl.BlockSpec((1, N_HEADS, HEAD, HEAD), lambda b, j: (b, 0, 0, 0))
    return pl.pallas_call(
        functools.partial(_wkv_scan_kernel, chunk=chunk, nchunk=nchunk),
        grid=(n_streams, nj),
        in_specs=[tok, tok, mat, mat, st],
        out_specs=[tok, st],
        out_shape=[jax.ShapeDtypeStruct((n, D_A), BF16),
                   jax.ShapeDtypeStruct((n_streams, N_HEADS, HEAD, HEAD), F32)],
        scratch_shapes=[pltpu.VMEM((HEAD, D_A), F32)],
        compiler_params=_params(("parallel", "arbitrary")),
        name="wkv_scan",
    )(rc, yc, m2, n2, s0)


CONV_ROWS = 128


def _conv_kernel(u_ref, hown_ref, hinit_ref, cw_ref, cb_ref, lg_ref, lb_ref, o_ref, ext_scr, sh_scr, z_scr):
    tm = u_ref.shape[0]
    n_lt = D_B // LANE
    first = pl.program_id(1) == 0

    @pl.when(first)
    def _():
        for l in range(n_lt):
            ext_scr[l, 0:HALO, :] = hinit_ref[0, :, l * LANE:(l + 1) * LANE]

    @pl.when(jnp.logical_not(first))
    def _():
        for l in range(n_lt):
            ext_scr[l, 0:HALO, :] = hown_ref[:, l * LANE:(l + 1) * LANE]

    for l in range(n_lt):
        ext_scr[l, HALO:HALO + tm, :] = u_ref[:, l * LANE:(l + 1) * LANE]
    span = sh_scr.shape[1]
    lead = HALO - (CONV_W - 1)

    def lane_tile(l, carry):
        for s in range(1, SUBLANE):
            sh_scr[s - 1] = ext_scr[l, s:s + span, :]
        acc = jnp.broadcast_to(cb_ref[l], (tm, LANE))
        for j in range(CONV_W):
            s, q = (lead + j) % SUBLANE, (lead + j) // SUBLANE
            r0 = q * SUBLANE
            win = ext_scr[l, r0:r0 + tm, :] if s == 0 else sh_scr[s - 1, r0:r0 + tm, :]
            acc = acc + cw_ref[l, j:j + 1, :] * win
        z_scr[l] = acc
        return carry

    lax.fori_loop(0, n_lt, lane_tile, 0)
    rows = min(CONV_ROWS, tm)

    def norm(i, carry):
        r0 = pl.multiple_of(i * rows, rows)
        z = jnp.concatenate([z_scr[l, pl.ds(r0, rows), :] for l in range(n_lt)], axis=1)
        mu = jnp.mean(z, axis=-1, keepdims=True)
        d = z - mu
        var = jnp.mean(d * d, axis=-1, keepdims=True)
        zn = d * lax.rsqrt(var + LN_EPS) * lg_ref[...] + lb_ref[...]
        o_ref[pl.ds(r0, rows), :] = (zn * jax.nn.sigmoid(zn)).astype(BF16)
        return carry

    lax.fori_loop(0, tm // rows, norm, 0)


def _conv(u, hinit, cw, cb, lg, lb, *, n_streams, t_len, tm):
    n = u.shape[0]
    tps = t_len // tm
    if tps > 1:
        blocks_per_tile = tm // HALO
        own_map = lambda b, i: (jnp.maximum((b * tps + i) * blocks_per_tile - 1, 0), 0)
    else:
        own_map = lambda b, i: (0, 0)
    return pl.pallas_call(
        _conv_kernel,
        grid=(n_streams, tps),
        in_specs=[
            pl.BlockSpec((tm, D_B), lambda b, i: (b * tps + i, 0)),
            pl.BlockSpec((HALO, D_B), own_map),
            pl.BlockSpec((1, HALO, D_B), lambda b, i: (b, 0, 0)),
            _const_spec((D_B // LANE, HALO, LANE)),
            _const_spec((D_B // LANE, 1, LANE)),
            _const_spec((1, D_B)),
            _const_spec((1, D_B)),
        ],
        out_specs=pl.BlockSpec((tm, D_B), lambda b, i: (b * tps + i, 0)),
        out_shape=jax.ShapeDtypeStruct((n, D_B), BF16),
        scratch_shapes=[pltpu.VMEM((D_B // LANE, HALO + tm, LANE), F32),
                        pltpu.VMEM((SUBLANE - 1, HALO + tm - SUBLANE, LANE), F32),
                        pltpu.VMEM((D_B // LANE, tm, LANE), F32)],
        compiler_params=_params(("parallel", "arbitrary")),
        name="conv",
    )(u, u, hinit, cw, cb, lg, lb)


def _merge_kernel(x_ref, y_ref, bonus_ref, g_ref, ob_ref, n1_ref, wg_ref, bg_ref,
                  lng_ref, lnb_ref, woa_ref, wob_ref, o_ref):
    h = _rms(x_ref[...], n1_ref[...]).astype(BF16)
    y = y_ref[...].astype(F32)
    d = y - _head_sum(y) * (1.0 / HEAD)
    var = _head_sum(d * d) * (1.0 / HEAD)
    yn = d * lax.rsqrt(var + GN_EPS) * lng_ref[...] + lnb_ref[...]
    oa = ((yn + bonus_ref[...].astype(F32)) * g_ref[...].astype(F32)).astype(BF16)
    for j in range(2):
        cols = slice(j * D_A, (j + 1) * D_A)
        gcols = slice(D_MODEL + j * D_A, D_MODEL + (j + 1) * D_A)
        ga = jax.nn.sigmoid(_dot(h, wg_ref[:, cols]) + bg_ref[:, cols])
        gb = jax.nn.sigmoid(_dot(h, wg_ref[:, gcols]) + bg_ref[:, gcols])
        o_ref[:, cols] = (ga * _dot(oa, woa_ref[:, cols])
                          + gb * _dot(ob_ref[...], wob_ref[:, cols])).astype(BF16)


def _merge(x, y, bonus, g, ob, n1, wg, bg, lng, lnb, woa, wob):
    n = x.shape[0]
    tm = min(TM_WIDE, n)
    tok_a = pl.BlockSpec((tm, D_A), lambda i: (i, 0))
    tok_m = pl.BlockSpec((tm, D_MODEL), lambda i: (i, 0))
    return pl.pallas_call(
        _merge_kernel,
        grid=(n // tm,),
        in_specs=[tok_m, tok_a, tok_a, tok_a, tok_a,
                  _const_spec((1, D_MODEL)),
                  _const_spec((D_MODEL, 2 * D_MODEL)),
                  _const_spec((1, 2 * D_MODEL)),
                  _const_spec((1, D_A)),
                  _const_spec((1, D_A)),
                  _const_spec((D_A, D_MODEL)),
                  _const_spec((D_B, D_MODEL))],
        out_specs=tok_m,
        out_shape=jax.ShapeDtypeStruct((n, D_MODEL), BF16),
        compiler_params=_params(("parallel",)),
        name="merge",
    )(x, y, bonus, g, ob, n1, wg, bg, lng, lnb, woa, wob)


def _pack_bf16_pairs(x):
    w = x.shape[1] // 2
    bits = pltpu.bitcast(x.astype(BF16).astype(F32), jnp.uint32)
    return (bits[:, w:] & jnp.uint32(0xFFFF0000)) | (bits[:, :w] >> jnp.uint32(16))


def _unpack_bf16_pairs(u):
    lo = pltpu.bitcast(u << jnp.uint32(16), F32)
    hi = pltpu.bitcast(u & jnp.uint32(0xFFFF0000), F32)
    return jnp.concatenate([lo, hi], axis=1).astype(BF16)


def _outproj_kernel(m_ref, x_ref, wo_ref, n2_ref, wrh_ref, wrl_ref, brt_ref, x1_ref, h2_ref, lg_ref):
    x1 = x_ref[...] + _dot(m_ref[...], wo_ref[...])
    x1_ref[...] = x1
    h2 = _rms(x1, n2_ref[...])
    h2_ref[...] = _pack_bf16_pairs(h2)
    lg_ref[...] = _dot3(h2, wrh_ref[...], wrl_ref[...]) + brt_ref[...]


def _outproj(m, x, wo, n2, wrh, wrl, brt):
    n = x.shape[0]
    tm = min(TM_WIDE, n)
    tok_m = pl.BlockSpec((tm, D_MODEL), lambda i: (i, 0))
    return pl.pallas_call(
        _outproj_kernel,
        grid=(n // tm,),
        in_specs=[tok_m, tok_m,
                  _const_spec((D_MODEL, D_MODEL)),
                  _const_spec((1, D_MODEL)),
                  _const_spec((D_MODEL, ROUTE_W)),
                  _const_spec((D_MODEL, ROUTE_W)),
                  _const_spec((1, ROUTE_W))],
        out_specs=[tok_m, pl.BlockSpec((tm, D_MODEL // 2), lambda i: (i, 0)),
                   pl.BlockSpec((tm, ROUTE_W), lambda i: (i, 0))],
        out_shape=[jax.ShapeDtypeStruct((n, D_MODEL), F32),
                   jax.ShapeDtypeStruct((n, D_MODEL // 2), jnp.uint32),
                   jax.ShapeDtypeStruct((n, ROUTE_W), F32)],
        compiler_params=_params(("parallel",)),
        name="outproj",
    )(m, x, wo, n2, wrh, wrl, brt)


def _route_kernel(lg_ref, ri_ref, rw_ref, cnt_ref, carry_scr):
    i = pl.program_id(0)
    tm = lg_ref.shape[0]

    @pl.when(i == 0)
    def _():
        carry_scr[...] = jnp.zeros_like(carry_scr)

    lg = lg_ref[...]
    lane = lax.broadcasted_iota(jnp.int32, (tm, ROUTE_W), 1)
    neg = jnp.float32(-jnp.inf)
    big = jnp.int32(ROUTE_W)
    grp = jnp.where(lane < N_GROUPS, lg, neg)
    gmax = jnp.max(grp, axis=1, keepdims=True)
    g_sel = jnp.min(jnp.where(grp == gmax, lane, big), axis=1, keepdims=True)
    p_grp = 1.0 / jnp.sum(jnp.where(lane < N_GROUPS, jnp.exp(lg - gmax), 0.0), axis=1, keepdims=True)
    lo = N_GROUPS + g_sel * EXP_PER_GROUP
    le = jnp.where((lane >= lo) & (lane < lo + EXP_PER_GROUP), lg, neg)
    m1 = jnp.max(le, axis=1, keepdims=True)
    i1 = jnp.min(jnp.where(le == m1, lane, big), axis=1, keepdims=True)
    le2 = jnp.where(lane == i1, neg, le)
    m2 = jnp.max(le2, axis=1, keepdims=True)
    i2 = jnp.min(jnp.where(le2 == m2, lane, big), axis=1, keepdims=True)
    e2 = jnp.exp(m2 - m1)
    wa = p_grp / (1.0 + e2)
    wb = p_grp * e2 / (1.0 + e2)
    ea = i1 - N_GROUPS
    eb = i2 - N_GROUPS
    onehot = jnp.where((lane == ea) | (lane == eb), 1.0, 0.0)
    ri_ = lax.broadcasted_iota(jnp.int32, (tm, tm), 0)
    ci_ = lax.broadcasted_iota(jnp.int32, (tm, tm), 1)
    below = jnp.where(ci_ < ri_, 1.0, 0.0).astype(BF16)
    before = _dot(below, onehot.astype(BF16)) + carry_scr[...]
    rank_a = jnp.sum(jnp.where(lane == ea, before, 0.0), axis=1, keepdims=True).astype(jnp.int32)
    rank_b = jnp.sum(jnp.where(lane == eb, before, 0.0), axis=1, keepdims=True).astype(jnp.int32)
    carry_scr[...] = carry_scr[...] + jnp.sum(onehot, axis=0, keepdims=True)
    ri = jnp.where(lane == 0, ea, jnp.where(lane == 1, eb,
                   jnp.where(lane == 2, rank_a, jnp.where(lane == 3, rank_b, 0))))
    ri_ref[0] = ri.T[0:SUBLANE, :]
    rw_ref[...] = jnp.where(lane == 0, wa, jnp.where(lane == 1, wb, 0.0))
    cnt_ref[...] = jnp.broadcast_to(carry_scr[...], cnt_ref.shape)


def _route(lg):
    n = lg.shape[0]
    tok = pl.BlockSpec((TM, ROUTE_W), lambda i: (i, 0))
    return pl.pallas_call(
        _route_kernel,
        grid=(n // TM,),
        in_specs=[tok],
        out_specs=[pl.BlockSpec((1, SUBLANE, TM), lambda i: (i, 0, 0)), tok,
                   pl.BlockSpec((8, ROUTE_W), lambda i: (0, 0))],
        out_shape=[jax.ShapeDtypeStruct((n // TM, SUBLANE, TM), jnp.int32),
                   jax.ShapeDtypeStruct((n, ROUTE_W), F32),
                   jax.ShapeDtypeStruct((8, ROUTE_W), F32)],
        scratch_shapes=[pltpu.VMEM((1, ROUTE_W), F32)],
        compiler_params=_params(("arbitrary",)),
        name="route",
    )(lg)


def _dispatch_kernel(pos_ref, h_ref, xs_in_ref, xs_ref, sem):
    del xs_in_ref
    tm = h_ref.shape[0]

    def copy(r, slot):
        return pltpu.make_async_copy(h_ref.at[pl.ds(r, 1)],
                                     xs_ref.at[pl.ds(pos_ref[0, slot, r], 1)], sem)

    def start(r):
        copy(r, 0).start(priority=0)
        copy(r, 1).start(priority=1)

    def wait(r):
        copy(r, 0).wait()
        copy(r, 1).wait()

    for r in range(tm):
        start(r)
    for r in range(tm):
        wait(r)


def _dispatch(pos, h2, xs, *, tile0):
    n = h2.shape[0]
    return pl.pallas_call(
        _dispatch_kernel,
        grid=(n // TM,),
        in_specs=[
            pl.BlockSpec((1, 2, TM), lambda i: (i + tile0, 0, 0), memory_space=pltpu.SMEM),
            pl.BlockSpec((TM, D_MODEL // 2), lambda i: (i, 0)),
            pl.BlockSpec(memory_space=pl.ANY),
        ],
        out_specs=pl.BlockSpec(memory_space=pl.ANY),
        out_shape=jax.ShapeDtypeStruct(xs.shape, xs.dtype),
        scratch_shapes=[pltpu.SemaphoreType.DMA(())],
        input_output_aliases={2: 0},
        compiler_params=_params(("arbitrary",)),
        name="dispatch",
    )(pos, h2, xs)


def _experts_kernel(te_ref, tv_ref, xs_ref, wg_ref, wu_ref, wd_ref, y_ref, wg_scr, wu_scr, wd_scr):
    t = pl.program_id(0)
    new_expert = jnp.logical_or(t == 0, te_ref[t] != te_ref[jnp.maximum(t - 1, 0)])

    @pl.when(new_expert)
    def _():
        wg_scr[...] = wg_ref[0].astype(BF16)
        wu_scr[...] = wu_ref[0].astype(BF16)
        wd_scr[...] = wd_ref[0].astype(BF16)

    @pl.when(tv_ref[t] == 1)
    def _():
        x = _unpack_bf16_pairs(xs_ref[...])
        gt = _dot(x, wg_scr[...])
        up = _dot(x, wu_scr[...])
        hid = (gt * jax.nn.sigmoid(gt) * up).astype(BF16)
        y_ref[...] = _pack_bf16_pairs(_dot(hid, wd_scr[...]))

    @pl.when(tv_ref[t] == 0)
    def _():
        y_ref[...] = jnp.zeros_like(y_ref)


def _experts(te, tv, xs, wg, wu, wd):
    p = xs.shape[0]
    grid_spec = pltpu.PrefetchScalarGridSpec(
        num_scalar_prefetch=2,
        grid=(p // TMX,),
        in_specs=[
            pl.BlockSpec((TMX, D_MODEL // 2), lambda t, te, tv: (t, 0)),
            pl.BlockSpec((1, D_MODEL, D_EXPERT), lambda t, te, tv: (te[t], 0, 0)),
            pl.BlockSpec((1, D_MODEL, D_EXPERT), lambda t, te, tv: (te[t], 0, 0)),
            pl.BlockSpec((1, D_EXPERT, D_MODEL), lambda t, te, tv: (te[t], 0, 0)),
        ],
        out_specs=pl.BlockSpec((TMX, D_MODEL // 2), lambda t, te, tv: (t, 0)),
        scratch_shapes=[pltpu.VMEM((D_MODEL, D_EXPERT), BF16), pltpu.VMEM((D_MODEL, D_EXPERT), BF16),
                        pltpu.VMEM((D_EXPERT, D_MODEL), BF16)],
    )
    return pl.pallas_call(
        _experts_kernel,
        grid_spec=grid_spec,
        out_shape=jax.ShapeDtypeStruct((p, D_MODEL // 2), jnp.uint32),
        compiler_params=_params(("arbitrary",)),
        name="experts",
    )(te, tv, xs, wg, wu, wd)


def _combine_kernel(pos_ref, x1_ref, rw_ref, fg_ref, ys_ref, o_ref, ya_scr, yb_scr, sem):
    tm = x1_ref.shape[0]

    def copies(r):
        return (pltpu.make_async_copy(ys_ref.at[pl.ds(pos_ref[0, 0, r], 1)], ya_scr.at[pl.ds(r, 1)], sem),
                pltpu.make_async_copy(ys_ref.at[pl.ds(pos_ref[0, 1, r], 1)], yb_scr.at[pl.ds(r, 1)], sem))

    def start(r):
        ca, cb = copies(r)
        ca.start(priority=0)
        cb.start(priority=1)

    def wait(r):
        ca, cb = copies(r)
        ca.wait()
        cb.wait()

    for r in range(tm):
        start(r)
    for r in range(tm):
        wait(r)
    rw = rw_ref[...]
    ya = _unpack_bf16_pairs(ya_scr[...]).astype(F32)
    yb = _unpack_bf16_pairs(yb_scr[...]).astype(F32)
    x2 = x1_ref[...] + rw[:, 0:1] * ya + rw[:, 1:2] * yb
    o_ref[...] = _rms(x2, fg_ref[...])


def _combine(pos, x1, rw, fg, ys, *, tile0):
    n = x1.shape[0]
    tok_m = pl.BlockSpec((TM, D_MODEL), lambda i: (i, 0))
    return pl.pallas_call(
        _combine_kernel,
        grid=(n // TM,),
        in_specs=[
            pl.BlockSpec((1, 2, TM), lambda i: (i + tile0, 0, 0), memory_space=pltpu.SMEM),
            tok_m,
            pl.BlockSpec((TM, ROUTE_W), lambda i: (i + tile0, 0)),
            _const_spec((1, D_MODEL)),
            pl.BlockSpec(memory_space=pl.ANY),
        ],
        out_specs=tok_m,
        out_shape=jax.ShapeDtypeStruct((n, D_MODEL), F32),
        scratch_shapes=[pltpu.VMEM((TM, D_MODEL // 2), jnp.uint32),
                        pltpu.VMEM((TM, D_MODEL // 2), jnp.uint32),
                        pltpu.SemaphoreType.DMA(())],
        compiler_params=_params(("arbitrary",)),
        name="combine",
    )(pos, x1, rw, fg, ys)


def _row(v):
    return v.reshape(1, -1).astype(F32)


def _prepare_weights(norm1_g, w_in, b_conv_in, b_gate, mu_shift, w0, w_lora2, a0, a_lora2, g_lora2,
                     k_k, k_a, r_k, lnx_g, lnx_b, w_out_a, conv_w, conv_b, cln_g, cln_b, w_out_b, w_o,
                     norm2_g, w_rg, b_rg, w_re, b_re, w_gate_e, w_up_e, w_down_e, final_g):
    w = {}
    w["n1"] = _row(norm1_g)
    w["wr"] = _permute_cols(w_in[:, :SHIFT_W]).astype(BF16)
    w["wc"] = w_in[:, SHIFT_W:SHIFT_W + 2 * D_B].astype(BF16)
    w["wg"] = w_in[:, SHIFT_W + 2 * D_B:].astype(BF16)
    w["bc"] = _row(b_conv_in)
    w["bg"] = _row(b_gate)
    w["mu"] = _permute_cols(mu_shift).reshape(1, P_W)
    w["vp"] = jnp.concatenate([jnp.stack([w0, a0, k_k, k_a, r_k.reshape(-1)]),
                               jnp.zeros((3, D_A), F32)], axis=0)
    z = jnp.zeros((LORA_W, D_A), F32)
    w2 = jnp.concatenate([jnp.concatenate([w_lora2, z], axis=1),
                          jnp.concatenate([z, a_lora2], axis=1)], axis=0)
    w["w2h"], w["w2l"] = _split(w2)
    g2 = jnp.pad(g_lora2, ((0, 2 * LANE - LORA_G), (0, 0)))
    w["g2h"], w["g2l"] = _split(g2)
    w["lng"] = _row(lnx_g)
    w["lnb"] = _row(lnx_b)
    w["woa"] = w_out_a.astype(BF16)
    w["wob"] = w_out_b.astype(BF16)
    w["cw"] = jnp.pad(conv_w, ((0, HALO - CONV_W), (0, 0))).reshape(HALO, D_B // LANE, LANE).transpose(1, 0, 2)
    w["cb"] = conv_b.reshape(D_B // LANE, 1, LANE)
    w["clg"] = _row(cln_g)
    w["clb"] = _row(cln_b)
    w["wo"] = w_o.astype(BF16)
    w["n2"] = _row(norm2_g)
    pad_r = ROUTE_W - N_GROUPS - N_EXPERTS
    w["wrh"], w["wrl"] = _split(jnp.pad(jnp.concatenate([w_rg, w_re], axis=1), ((0, 0), (0, pad_r))))
    w["brt"] = jnp.pad(jnp.concatenate([b_rg, b_re]), (0, pad_r)).reshape(1, ROUTE_W)
    w["wge"] = w_gate_e
    w["wue"] = w_up_e
    w["wde"] = w_down_e
    w["fg"] = _row(final_g)
    return w


def _trunk(x, shift0, wkv0, conv0, w, *, chunk, nchunk, intra_chunks, conv_tm):
    b, t, _ = x.shape
    n = b * t
    xf = x.reshape(n, D_MODEL)
    shift_p = _permute_cols(shift0)
    tps = max(t // TM, 1)
    nb, step = (1, TM) if t >= TM else (TM // t, t)
    u, tails, rt, at, bt, kt, vv, ge, bonus, g = _inprep(
        xf, w["n1"], w["wr"], w["wc"], w["bc"], shift_p.reshape(-1, nb, P_W), w["mu"], w["vp"],
        w["w2h"], w["w2l"], w["g2h"], w["g2l"], chunk=chunk, nb=nb, step=step, tps=tps)
    rc, yc, m2, n2 = _wkv_intra(rt, at, bt, kt, vv, ge, chunk=chunk, nchunk=intra_chunks)
    y, wkv_new = _wkv_scan(rc, yc, m2, n2, wkv0, n_streams=b, t_len=t, chunk=chunk, nchunk=nchunk)
    hinit = jnp.pad(conv0, ((0, 0), (HALO - (CONV_W - 1), 0), (0, 0)))
    ob = _conv(u, hinit, w["cw"], w["cb"], w["clg"], w["clb"], n_streams=b, t_len=t, tm=conv_tm)
    merged = _merge(xf, y, bonus, g, ob, w["n1"], w["wg"], w["bg"], w["lng"], w["lnb"], w["woa"], w["wob"])
    x1, h2, lg = _outproj(merged, xf, w["wo"], w["n2"], w["wrh"], w["wrl"], w["brt"])
    new_shift = tails.reshape(b, -1, P_W)[:, -1, :SHIFT_W][:, _INV_PERM]
    ext_tail = jnp.concatenate([conv0, u.reshape(b, t, D_B)[:, -(CONV_W - 1):]], axis=1)[:, -(CONV_W - 1):]
    return x1, h2, lg, new_shift, wkv_new, ext_tail


def kernel(x_prompt, x_sample, state_shift, state_wkv, cache_conv, norm1_g, w_in, b_conv_in, b_gate,
           mu_shift, w0, w_lora2, a0, a_lora2, g_lora2, k_k, k_a, r_k, lnx_g, lnx_b, w_out_a, conv_w,
           conv_b, cln_g, cln_b, w_out_b, w_o, norm2_g, w_rg, b_rg, w_re, b_re, w_gate_e, w_up_e,
           w_down_e, final_g):
    per_layer = (norm1_g, w_in, b_conv_in, b_gate, mu_shift, w0, w_lora2, a0, a_lora2, g_lora2, k_k, k_a,
                 r_k, lnx_g, lnx_b, w_out_a, conv_w, conv_b, cln_g, cln_b, w_out_b, w_o, norm2_g, w_rg,
                 b_rg, w_re, b_re, w_gate_e, w_up_e, w_down_e)
    assert norm1_g.shape[0] == 1, "single layer only"
    w = _prepare_weights(*[p[0] for p in per_layer], final_g)
    bp, tp, _ = x_prompt.shape
    bs, ts, _ = x_sample.shape
    zero_shift = jnp.zeros((bp, SHIFT_W), F32)
    zero_wkv = jnp.zeros((bp, N_HEADS, HEAD, HEAD), F32)
    zero_conv = jnp.zeros((bp, CONV_W - 1, D_B), F32)
    x1p, h2p, lgp, shift_p, wkv_p, conv_p = _trunk(
        x_prompt, zero_shift, zero_wkv, zero_conv, w, chunk=64, nchunk=min(8, tp // 64),
        intra_chunks=min(16, tp // 64), conv_tm=TM)
    x1s, h2s, lgs, shift_s, wkv_s, conv_s = _trunk(
        x_sample, state_shift[0], state_wkv[0], cache_conv[0], w, chunk=ts, nchunk=1,
        intra_chunks=TM // ts, conv_tm=ts)

    n_p, n_s = bp * tp, bs * ts
    n_all = n_p + n_s
    ri, rw, cnt = _route(jnp.concatenate([lgp, lgs], axis=0))
    counts = cnt[0, :N_EXPERTS].astype(jnp.int32)
    tiles_per = (counts + TMX - 1) // TMX
    tile_end = jnp.cumsum(tiles_per)
    offs = (tile_end - tiles_per) * TMX
    n_tiles = (2 * n_all + N_EXPERTS * (TMX - 1) + TMX - 1) // TMX
    t_idx = jnp.arange(n_tiles, dtype=jnp.int32)
    te = jnp.sum(t_idx[:, None] >= tile_end[None, :], axis=1).astype(jnp.int32)
    tv = (t_idx < tile_end[-1]).astype(jnp.int32)
    e_ids = jnp.arange(N_EXPERTS, dtype=jnp.int32)
    last_e = jnp.max(jnp.where(tiles_per > 0, e_ids, 0))
    te = jnp.where(tv == 1, jnp.minimum(te, N_EXPERTS - 1), last_e)
    base = jnp.sum(jnp.where(ri[:, 0:2, :, None] == e_ids, offs, 0), axis=-1)
    pos = base + ri[:, 2:4, :]
    xs = jnp.zeros((n_tiles * TMX, D_MODEL // 2), jnp.uint32)
    xs = _dispatch(pos, h2p, xs, tile0=0)
    xs = _dispatch(pos, h2s, xs, tile0=n_p // TM)
    ys = _experts(te, tv, xs, w["wge"], w["wue"], w["wde"])
    yp = _combine(pos, x1p, rw, w["fg"], ys, tile0=0)
    ysm = _combine(pos, x1s, rw, w["fg"], ys, tile0=n_p // TM)
    return (yp.reshape(bp, tp, D_MODEL), ysm.reshape(bs, ts, D_MODEL),
            shift_p[None], wkv_p[None], conv_p[None],
            shift_s[None], wkv_s[None], conv_s[None])
```

```python
import functools
import math

import numpy as np
import jax
import jax.numpy as jnp
from jax import lax
from jax.experimental import pallas as pl
from jax.experimental.pallas import tpu as pltpu

F32 = jnp.float32
BF16 = jnp.bfloat16

D_MODEL = 2048
D_A = 1024
HEAD = 64
N_HEADS = D_A // HEAD
LORA_W = 64
LORA_A = 64
LORA_G = 160
SHIFT_W = 3 * D_A + LORA_W + LORA_A + LORA_G
D_B = 1024
CONV_W = 31
N_GROUPS = 4
EXP_PER_GROUP = 8
N_EXPERTS = N_GROUPS * EXP_PER_GROUP
D_EXPERT = 512
RMS_EPS = 1e-6
LN_EPS = 1e-5
GN_EPS = 64e-5

LANE = 128
SUBLANE = 8
P_W = 27 * LANE
LORA_IN0 = 3 * D_A
XG0 = LORA_IN0 + LANE
HALO = 32
ROUTE_W = LANE
TM = 256
TM_WIDE = 512
TMX = 256
ZERO_ROWS = 256
VMEM_LIMIT = 56 * 1024 * 1024

_PERM_RANGES = ((0, D_A), (D_A + LORA_W, 2 * D_A + LORA_W), (2 * D_A + LORA_W, 3 * D_A + LORA_W),
                (D_A, D_A + LORA_W), (3 * D_A + LORA_W, SHIFT_W))
_PERM = np.concatenate([np.arange(a, b) for a, b in _PERM_RANGES])
_INV_PERM = np.argsort(_PERM)
assert len(_PERM) == SHIFT_W and (np.sort(_PERM) == np.arange(SHIFT_W)).all()


def _permute_cols(x):
    pad = jnp.zeros(x.shape[:-1] + (P_W - SHIFT_W,), x.dtype)
    return jnp.concatenate([x[..., a:b] for a, b in _PERM_RANGES] + [pad], axis=-1)


def _dot(a, b):
    return jnp.dot(a, b, preferred_element_type=F32)


def _dot_nt(a, b):
    return lax.dot_general(a, b, (((1,), (1,)), ((), ())), preferred_element_type=F32)


def _dot_tn(a, b):
    return lax.dot_general(a, b, (((0,), (0,)), ((), ())), preferred_element_type=F32)


def _split(x):
    hi = x.astype(BF16)
    lo = (x - hi.astype(F32)).astype(BF16)
    return hi, lo


def _dot3(x, w_hi, w_lo):
    x_hi, x_lo = _split(x)
    return _dot(x_hi, w_hi) + (_dot(x_lo, w_hi) + _dot(x_hi, w_lo))


def _rms(x, g):
    return x * lax.rsqrt(jnp.mean(x * x, axis=-1, keepdims=True) + RMS_EPS) * g


def _head_sum(x):
    w = 2 * LANE
    r = lax.broadcasted_iota(jnp.int32, (w, w), 0) // HEAD
    c = lax.broadcasted_iota(jnp.int32, (w, w), 1) // HEAD
    ones = jnp.where(r == c, 1.0, 0.0).astype(BF16)
    outs = [_dot(x[:, j * w:(j + 1) * w].astype(BF16), ones) for j in range(x.shape[1] // w)]
    return jnp.concatenate(outs, axis=1)


def _const_spec(shape):
    nd = len(shape)
    return pl.BlockSpec(shape, lambda *_: (0,) * nd, pipeline_mode=pl.Buffered(1))


def _params(sem):
    return pltpu.CompilerParams(dimension_semantics=sem, vmem_limit_bytes=VMEM_LIMIT)


def _inprep_kernel(x_ref, n1_ref, wr_ref, wc_ref, bc_ref, bnd_ref, mu_ref, vp_ref, w2h_ref, w2l_ref,
                   g2h_ref, g2l_ref,
                   u_ref, tail_ref, rt_ref, at_ref, bt_ref, kt_ref, v_ref, ge_ref, bonus_ref, g_ref,
                   first_scr, last_scr, *, chunk, nb, step, tps):
    tm = x_ref.shape[0]
    if nb == 1:
        starts_stream = lax.rem(pl.program_id(0), tps) == 0

        @pl.when(starts_stream)
        def _():
            first_scr[...] = bnd_ref[0, 0:1, :]

        @pl.when(jnp.logical_not(starts_stream))
        def _():
            first_scr[...] = last_scr[...]

    h = _rms(x_ref[...], n1_ref[...]).astype(BF16)
    p = _dot(h, wr_ref[...])
    for j in range(nb):
        tail_ref[0, j:j + 1, :] = p[(j + 1) * step - 1:(j + 1) * step, :]
    c = _dot(h, wc_ref[...]) + bc_ref[...]
    u_ref[...] = c[:, :D_B] * jax.nn.sigmoid(c[:, D_B:])

    row = lax.broadcasted_iota(jnp.int32, (tm, 1), 0)
    prev = pltpu.roll(p, 1, axis=0)
    if nb == 1:
        prev = jnp.where(row == 0, first_scr[...], prev)
        last_scr[...] = p[tm - 1:tm, :]
    else:
        for j in range(nb):
            prev = jnp.where(row == j * step, bnd_ref[0, j:j + 1, :], prev)
    xs = p + (prev - p) * mu_ref[...]
    r = xs[:, 0:D_A]
    k = xs[:, D_A:2 * D_A]
    v = xs[:, 2 * D_A:3 * D_A]
    lin = xs[:, LORA_IN0:LORA_IN0 + LANE]
    lane = lax.broadcasted_iota(jnp.int32, (1, LANE), 1)
    lin = jnp.where(lane < LORA_W, jnp.tanh(lin), lin)
    lo = _dot3(lin, w2h_ref[...], w2l_ref[...])
    w0 = vp_ref[0:1, :]
    a0 = vp_ref[1:2, :]
    k_k = vp_ref[2:3, :]
    k_a = vp_ref[3:4, :]
    r_k = vp_ref[4:5, :]
    zw = -(w0 + lo[:, :D_A])
    softplus = jnp.maximum(zw, 0.0) + jnp.log1p(jnp.exp(-jnp.abs(zw)))
    logw = -jnp.exp(-softplus - 0.5)
    a = jax.nn.sigmoid(a0 + lo[:, D_A:])
    g = _dot3(jax.nn.sigmoid(xs[:, XG0:XG0 + 2 * LANE]), g2h_ref[...], g2l_ref[...])
    kku = k * k_k
    kk = kku / jnp.maximum(jnp.sqrt(_head_sum(kku * kku)), 1e-12)
    k2 = k * (1.0 + (a - 1.0) * k_a)
    bonus_ref[...] = (_head_sum(r * k2 * r_k) * v).astype(BF16)
    g_ref[...] = g.astype(BF16)
    ri = lax.broadcasted_iota(jnp.int32, (tm, tm), 0)
    ci = lax.broadcasted_iota(jnp.int32, (tm, tm), 1)
    tri = jnp.where((ri // chunk == ci // chunk) & (ci <= ri), 1.0, 0.0).astype(BF16)
    lw_hi, lw_lo = _split(logw)
    cum = _dot(tri, lw_hi) + _dot(tri, lw_lo)
    eg = jnp.exp(cum)
    egi = jnp.exp(-cum)
    rt_ref[...] = (r * eg).astype(BF16)
    at_ref[...] = (-kk * jnp.exp(cum - logw)).astype(BF16)
    bt_ref[...] = (kk * a * egi).astype(BF16)
    kt_ref[...] = (k2 * egi).astype(BF16)
    v_ref[...] = v.astype(BF16)
    for cc in range(tm // chunk):
        ge_ref[cc] = eg[(cc + 1) * chunk - 1:(cc + 1) * chunk, :]


def _inprep(x, n1, wr, wc, bc, bnd, mu, vp, w2h, w2l, g2h, g2l, *, chunk, nb, step, tps):
    n = x.shape[0]
    tok = pl.BlockSpec((TM, D_A), lambda i: (i, 0))
    return pl.pallas_call(
        functools.partial(_inprep_kernel, chunk=chunk, nb=nb, step=step, tps=tps),
        grid=(n // TM,),
        in_specs=[
            pl.BlockSpec((TM, D_MODEL), lambda i: (i, 0)),
            _const_spec((1, D_MODEL)),
            _const_spec((D_MODEL, P_W)),
            _const_spec((D_MODEL, 2 * D_B)),
            _const_spec((1, 2 * D_B)),
            pl.BlockSpec((1, nb, P_W), lambda i: (i // tps, 0, 0)),
            _const_spec((1, P_W)),
            _const_spec((8, D_A)),
            _const_spec((LANE, 2 * D_A)),
            _const_spec((LANE, 2 * D_A)),
            _const_spec((2 * LANE, D_A)),
            _const_spec((2 * LANE, D_A)),
        ],
        out_specs=[tok,
                   pl.BlockSpec((1, nb, P_W), lambda i: (i, 0, 0)),
                   tok, tok, tok, tok, tok,
                   pl.BlockSpec((TM // chunk, 1, D_A), lambda i: (i, 0, 0)),
                   tok, tok],
        out_shape=[jax.ShapeDtypeStruct((n, D_B), F32),
                   jax.ShapeDtypeStruct((n // TM, nb, P_W), F32)]
        + [jax.ShapeDtypeStruct((n, D_A), BF16)] * 5
        + [jax.ShapeDtypeStruct((n // chunk, 1, D_A), F32),
           jax.ShapeDtypeStruct((n, D_A), BF16),
           jax.ShapeDtypeStruct((n, D_A), BF16)],
        scratch_shapes=[pltpu.VMEM((1, P_W), F32), pltpu.VMEM((1, P_W), F32)],
        compiler_params=_params(("arbitrary",)),
        name="inprep",
    )(x, n1, wr, wc, bc, bnd, mu, vp, w2h, w2l, g2h, g2l)


def _wkv_intra_kernel(rt_ref, at_ref, bt_ref, kt_ref, v_ref, ge_ref, rc_ref, yc_ref, m2_ref, n2_ref,
                      *, chunk, nchunk):
    c = chunk
    w2 = 2 * c
    row = lax.broadcasted_iota(jnp.int32, (c, w2), 0)
    col = lax.broadcasted_iota(jnp.int32, (c, w2), 1)
    colc = jnp.where(col >= c, col - c, col)
    strict = colc < row
    incl = colc <= row
    eye_c = jnp.where(colc == row, 1.0, 0.0).astype(F32)
    tok0 = lax.broadcasted_iota(jnp.int32, (1, w2), 1) < c
    ch0 = lax.broadcasted_iota(jnp.int32, (1, LANE), 1) < HEAD
    rk = lax.broadcasted_iota(jnp.int32, (HEAD, LANE), 0)
    ck = lax.broadcasted_iota(jnp.int32, (HEAD, LANE), 1)
    eye_k = jnp.where(jnp.where(ck >= HEAD, ck - HEAD, ck) == rk, 1.0, 0.0).astype(F32)
    n_sq = int(math.log2(c)) - 1

    def bd(x, first):
        zero = jnp.zeros_like(x)
        return jnp.concatenate([jnp.where(first, x, zero), jnp.where(first, zero, x)], axis=0)

    def head_blocks(x):
        return [jnp.where(ch0, x[:HEAD, j * LANE:(j + 1) * LANE], x[HEAD:, j * LANE:(j + 1) * LANE])
                for j in range(x.shape[1] // LANE)]

    rng = range(nchunk)
    rows = [slice(ci * c, (ci + 1) * c) for ci in rng]
    at = [at_ref[r, :] for r in rows]
    rt = [rt_ref[r, :] for r in rows]
    bt = [bt_ref[r, :] for r in rows]
    kt = [kt_ref[r, :] for r in rows]
    vv = [v_ref[r, :] for r in rows]
    m = [_dot_nt(jnp.concatenate([at[i], rt[i]], axis=0),
                 jnp.concatenate([bd(bt[i], ch0), bd(kt[i], ch0)], axis=0)) for i in rng]
    a_ab = [jnp.where(strict, x[:c, :w2], 0.0) for x in m]
    a_rb = [jnp.where(incl, x[c:, :w2], 0.0).astype(BF16) for x in m]
    akv = [_dot(jnp.concatenate([jnp.where(strict, x[:c, w2:], 0.0), jnp.where(incl, x[c:, w2:], 0.0)],
                                axis=0).astype(BF16), bd(vv[i], ch0)) for i, x in enumerate(m)]
    kv = [head_blocks(_dot_tn(kt[i], vv[i]))[0] for i in rng]
    t = [eye_c + x for x in a_ab]
    pw = [x.astype(BF16) for x in a_ab]
    pw = [_dot(x, bd(x, tok0)).astype(BF16) for x in pw]
    for _ in range(n_sq - 1):
        st = [_dot(jnp.concatenate([t[i].astype(BF16), pw[i]], axis=0), bd(pw[i], tok0)) for i in rng]
        t = [t[i] + st[i][:c] for i in rng]
        pw = [x[c:].astype(BF16) for x in st]
    t = [t[i] + _dot(t[i].astype(BF16), bd(pw[i], tok0)) for i in rng]
    twb = [_dot(t[i].astype(BF16),
                jnp.concatenate([bd(at[i], ch0), bd(akv[i][:c].astype(BF16), ch0)], axis=1)).astype(BF16)
           for i in rng]
    rb = [_dot(a_rb[i], jnp.concatenate([bd(twb[i][:, :LANE], ch0), bd(twb[i][:, LANE:], ch0)], axis=1))
          for i in rng]
    xb = [head_blocks(_dot_tn(bt[i], twb[i])) for i in rng]
    for i in rng:
        rc_ref[rows[i], :] = (rt[i].astype(F32) + rb[i][:, :LANE]).astype(BF16)
        yc_ref[rows[i], :] = rb[i][:, LANE:] + akv[i][c:]
        g_full = jnp.broadcast_to(ge_ref[i], (LANE, LANE)).T
        g_col = jnp.where(ch0, g_full[:HEAD], g_full[HEAD:])
        m2_ref[i] = (g_col * (eye_k + xb[i][0]) - eye_k).astype(BF16)
        n2_ref[i] = g_col * (xb[i][1] + kv[i])


def _wkv_intra(rt, at, bt, kt, v, ge, *, chunk, nchunk):
    n = rt.shape[0]
    rows = chunk * nchunk
    tok = pl.BlockSpec((rows, LANE), lambda i, hp: (i, hp))
    mat = pl.BlockSpec((nchunk, HEAD, LANE), lambda i, hp: (i, 0, hp))
    return pl.pallas_call(
        functools.partial(_wkv_intra_kernel, chunk=chunk, nchunk=nchunk),
        grid=(n // rows, N_HEADS // 2),
        in_specs=[tok, tok, tok, tok, tok,
                  pl.BlockSpec((nchunk, 1, LANE), lambda i, hp: (i, 0, hp))],
        out_specs=[tok, tok, mat, mat],
        out_shape=[jax.ShapeDtypeStruct((n, D_A), BF16),
                   jax.ShapeDtypeStruct((n, D_A), F32),
                   jax.ShapeDtypeStruct((n // chunk, HEAD, D_A), BF16),
                   jax.ShapeDtypeStruct((n // chunk, HEAD, D_A), F32)],
        compiler_params=_params(("parallel", "parallel")),
        name="wkv_intra",
    )(rt, at, bt, kt, v, ge)


def _wkv_scan_kernel(rc_ref, yc_ref, m2_ref, n2_ref, s0_ref, y_ref, st_ref, h_scr, *, chunk, nchunk):
    j = pl.program_id(1)

    @pl.when(j == 0)
    def _():
        for h in range(N_HEADS):
            h_scr[:, h * HEAD:(h + 1) * HEAD] = s0_ref[0, h].T

    c = chunk

    def chunk_body(ci, carry):
        off = pl.multiple_of(ci * c, c)
        h_all = h_scr[...]
        hb = h_all.astype(BF16)
        rc = rc_ref[pl.ds(off, c), :]
        m2 = m2_ref[ci]
        ch0 = lax.broadcasted_iota(jnp.int32, (1, LANE), 1) < HEAD
        zero = jnp.zeros((HEAD, LANE), BF16)
        ys, mh = [], []
        for hp in range(N_HEADS // 2):
            sl = slice(hp * LANE, (hp + 1) * LANE)
            hpair = hb[:, sl]
            hbd = jnp.concatenate([jnp.where(ch0, hpair, zero), jnp.where(ch0, zero, hpair)], axis=0)
            both = _dot(jnp.concatenate([rc[:, sl], m2[:, sl]], axis=0), hbd)
            ys.append(both[:c])
            mh.append(both[c:])
        y_ref[pl.ds(off, c), :] = (jnp.concatenate(ys, axis=1) + yc_ref[pl.ds(off, c), :]).astype(BF16)
        h_scr[...] = h_all + jnp.concatenate(mh, axis=1) + n2_ref[ci]
        return carry

    lax.fori_loop(0, nchunk, chunk_body, 0)

    @pl.when(j == pl.num_programs(1) - 1)
    def _():
        for h in range(N_HEADS):
            st_ref[0, h] = h_scr[:, h * HEAD:(h + 1) * HEAD].T


def _wkv_scan(rc, yc, m2, n2, s0, *, n_streams, t_len, chunk, nchunk):
    n = rc.shape[0]
    rows = chunk * nchunk
    nj = t_len // rows
    tok = pl.BlockSpec((rows, D_A), lambda b, j: (b * nj + j, 0))
    mat = pl.BlockSpec((nchunk, HEAD, D_A), lambda b, j: (b * nj + j, 0, 0))
    st = pl.BlockSpec((1, N_HEADS, HEAD, HEAD), lambda b, j: (b, 0, 0, 0))
    return pl.pallas_call(
        functools.partial(_wkv_scan_kernel, chunk=chunk, nchunk=nchunk),
        grid=(n_streams, nj),
        in_specs=[tok, tok, mat, mat, st],
        out_specs=[tok, st],
        out_shape=[jax.ShapeDtypeStruct((n, D_A), BF16),
                   jax.ShapeDtypeStruct((n_streams, N_HEADS, HEAD, HEAD), F32)],
        scratch_shapes=[pltpu.VMEM((HEAD, D_A), F32)],
        compiler_params=_params(("parallel", "arbitrary")),
        name="wkv_scan",
    )(rc, yc, m2, n2, s0)


CONV_ROWS = 128


def _conv_kernel(u_ref, hown_ref, hinit_ref, cw_ref, cb_ref, lg_ref, lb_ref, o_ref, ext_scr, sh_scr, z_scr):
    tm = u_ref.shape[0]
    n_lt = D_B // LANE
    first = pl.program_id(1) == 0

    @pl.when(first)
    def _():
        for l in range(n_lt):
            ext_scr[l, 0:HALO, :] = hinit_ref[0, :, l * LANE:(l + 1) * LANE]

    @pl.when(jnp.logical_not(first))
    def _():
        for l in range(n_lt):
            ext_scr[l, 0:HALO, :] = hown_ref[:, l * LANE:(l + 1) * LANE]

    for l in range(n_lt):
        ext_scr[l, HALO:HALO + tm, :] = u_ref[:, l * LANE:(l + 1) * LANE]
    span = sh_scr.shape[1]
    lead = HALO - (CONV_W - 1)

    def lane_tile(l, carry):
        for s in range(1, SUBLANE):
            sh_scr[s - 1] = ext_scr[l, s:s + span, :]
        acc = jnp.broadcast_to(cb_ref[l], (tm, LANE))
        for j in range(CONV_W):
            s, q = (lead + j) % SUBLANE, (lead + j) // SUBLANE
            r0 = q * SUBLANE
            win = ext_scr[l, r0:r0 + tm, :] if s == 0 else sh_scr[s - 1, r0:r0 + tm, :]
            acc = acc + cw_ref[l, j:j + 1, :] * win
        z_scr[l] = acc
        return carry

    lax.fori_loop(0, n_lt, lane_tile, 0)
    rows = min(CONV_ROWS, tm)

    def norm(i, carry):
        r0 = pl.multiple_of(i * rows, rows)
        z = jnp.concatenate([z_scr[l, pl.ds(r0, rows), :] for l in range(n_lt)], axis=1)
        mu = jnp.mean(z, axis=-1, keepdims=True)
        d = z - mu
        var = jnp.mean(d * d, axis=-1, keepdims=True)
        zn = d * lax.rsqrt(var + LN_EPS) * lg_ref[...] + lb_ref[...]
        o_ref[pl.ds(r0, rows), :] = (zn * jax.nn.sigmoid(zn)).astype(BF16)
        return carry

    lax.fori_loop(0, tm // rows, norm, 0)


def _conv(u, hinit, cw, cb, lg, lb, *, n_streams, t_len, tm):
    n = u.shape[0]
    tps = t_len // tm
    if tps > 1:
        blocks_per_tile = tm // HALO
        own_map = lambda b, i: (jnp.maximum((b * tps + i) * blocks_per_tile - 1, 0), 0)
    else:
        own_map = lambda b, i: (0, 0)
    return pl.pallas_call(
        _conv_kernel,
        grid=(n_streams, tps),
        in_specs=[
            pl.BlockSpec((tm, D_B), lambda b, i: (b * tps + i, 0)),
            pl.BlockSpec((HALO, D_B), own_map),
            pl.BlockSpec((1, HALO, D_B), lambda b, i: (b, 0, 0)),
            _const_spec((D_B // LANE, HALO, LANE)),
            _const_spec((D_B // LANE, 1, LANE)),
            _const_spec((1, D_B)),
            _const_spec((1, D_B)),
        ],
        out_specs=pl.BlockSpec((tm, D_B), lambda b, i: (b * tps + i, 0)),
        out_shape=jax.ShapeDtypeStruct((n, D_B), BF16),
        scratch_shapes=[pltpu.VMEM((D_B // LANE, HALO + tm, LANE), F32),
                        pltpu.VMEM((SUBLANE - 1, HALO + tm - SUBLANE, LANE), F32),
                        pltpu.VMEM((D_B // LANE, tm, LANE), F32)],
        compiler_params=_params(("parallel", "arbitrary")),
        name="conv",
    )(u, u, hinit, cw, cb, lg, lb)


def _merge_kernel(x_ref, y_ref, bonus_ref, g_ref, ob_ref, n1_ref, wg_ref, bg_ref,
                  lng_ref, lnb_ref, woa_ref, wob_ref, o_ref):
    h = _rms(x_ref[...], n1_ref[...]).astype(BF16)
    y = y_ref[...].astype(F32)
    d = y - _head_sum(y) * (1.0 / HEAD)
    var = _head_sum(d * d) * (1.0 / HEAD)
    yn = d * lax.rsqrt(var + GN_EPS) * lng_ref[...] + lnb_ref[...]
    oa = ((yn + bonus_ref[...].astype(F32)) * g_ref[...].astype(F32)).astype(BF16)
    for j in range(2):
        cols = slice(j * D_A, (j + 1) * D_A)
        gcols = slice(D_MODEL + j * D_A, D_MODEL + (j + 1) * D_A)
        ga = jax.nn.sigmoid(_dot(h, wg_ref[:, cols]) + bg_ref[:, cols])
        gb = jax.nn.sigmoid(_dot(h, wg_ref[:, gcols]) + bg_ref[:, gcols])
        o_ref[:, cols] = (ga * _dot(oa, woa_ref[:, cols])
                          + gb * _dot(ob_ref[...], wob_ref[:, cols])).astype(BF16)


def _merge(x, y, bonus, g, ob, n1, wg, bg, lng, lnb, woa, wob):
    n = x.shape[0]
    tm = min(TM_WIDE, n)
    tok_a = pl.BlockSpec((tm, D_A), lambda i: (i, 0))
    tok_m = pl.BlockSpec((tm, D_MODEL), lambda i: (i, 0))
    return pl.pallas_call(
        _merge_kernel,
        grid=(n // tm,),
        in_specs=[tok_m, tok_a, tok_a, tok_a, tok_a,
                  _const_spec((1, D_MODEL)),
                  _const_spec((D_MODEL, 2 * D_MODEL)),
                  _const_spec((1, 2 * D_MODEL)),
                  _const_spec((1, D_A)),
                  _const_spec((1, D_A)),
                  _const_spec((D_A, D_MODEL)),
                  _const_spec((D_B, D_MODEL))],
        out_specs=tok_m,
        out_shape=jax.ShapeDtypeStruct((n, D_MODEL), BF16),
        compiler_params=_params(("parallel",)),
        name="merge",
    )(x, y, bonus, g, ob, n1, wg, bg, lng, lnb, woa, wob)


def _pack_bf16_pairs(x):
    w = x.shape[1] // 2
    bits = pltpu.bitcast(x.astype(BF16).astype(F32), jnp.uint32)
    return (bits[:, w:] & jnp.uint32(0xFFFF0000)) | (bits[:, :w] >> jnp.uint32(16))


def _unpack_bf16_pairs(u):
    lo = pltpu.bitcast(u << jnp.uint32(16), F32)
    hi = pltpu.bitcast(u & jnp.uint32(0xFFFF0000), F32)
    return jnp.concatenate([lo, hi], axis=1).astype(BF16)


def _outproj_kernel(m_ref, x_ref, wo_ref, n2_ref, wrh_ref, wrl_ref, brt_ref, x1_ref, h2_ref, lg_ref):
    x1 = x_ref[...] + _dot(m_ref[...], wo_ref[...])
    x1_ref[...] = x1
    h2 = _rms(x1, n2_ref[...])
    h2_ref[...] = _pack_bf16_pairs(h2)
    lg_ref[...] = _dot3(h2, wrh_ref[...], wrl_ref[...]) + brt_ref[...]


def _outproj(m, x, wo, n2, wrh, wrl, brt):
    n = x.shape[0]
    tm = min(TM_WIDE, n)
    tok_m = pl.BlockSpec((tm, D_MODEL), lambda i: (i, 0))
    return pl.pallas_call(
        _outproj_kernel,
        grid=(n // tm,),
        in_specs=[tok_m, tok_m,
                  _const_spec((D_MODEL, D_MODEL)),
                  _const_spec((1, D_MODEL)),
                  _const_spec((D_MODEL, ROUTE_W)),
                  _const_spec((D_MODEL, ROUTE_W)),
                  _const_spec((1, ROUTE_W))],
        out_specs=[tok_m, pl.BlockSpec((tm, D_MODEL // 2), lambda i: (i, 0)),
                   pl.BlockSpec((tm, ROUTE_W), lambda i: (i, 0))],
        out_shape=[jax.ShapeDtypeStruct((n, D_MODEL), F32),
                   jax.ShapeDtypeStruct((n, D_MODEL // 2), jnp.uint32),
                   jax.ShapeDtypeStruct((n, ROUTE_W), F32)],
        compiler_params=_params(("parallel",)),
        name="outproj",
    )(m, x, wo, n2, wrh, wrl, brt)


def _route_kernel(lg_ref, ri_ref, rw_ref, cnt_ref, xs_ref, carry_scr, zero_scr, sem, *, per_step):
    i = pl.program_id(0)
    tm = lg_ref.shape[0]

    @pl.when(i == 0)
    def _():
        carry_scr[...] = jnp.zeros_like(carry_scr)
        zero_scr[...] = jnp.zeros_like(zero_scr)

    zr = zero_scr.shape[0]

    def zero_copy(q):
        start = pl.multiple_of(i * per_step + q * zr, zr)
        return pltpu.make_async_copy(zero_scr, xs_ref.at[pl.ds(start, zr)], sem)

    for q in range(per_step // zr):
        zero_copy(q).start()

    lg = lg_ref[...]
    lane = lax.broadcasted_iota(jnp.int32, (tm, ROUTE_W), 1)
    neg = jnp.float32(-jnp.inf)
    big = jnp.int32(ROUTE_W)
    grp = jnp.where(lane < N_GROUPS, lg, neg)
    gmax = jnp.max(grp, axis=1, keepdims=True)
    g_sel = jnp.min(jnp.where(grp == gmax, lane, big), axis=1, keepdims=True)
    p_grp = 1.0 / jnp.sum(jnp.where(lane < N_GROUPS, jnp.exp(lg - gmax), 0.0), axis=1, keepdims=True)
    lo = N_GROUPS + g_sel * EXP_PER_GROUP
    le = jnp.where((lane >= lo) & (lane < lo + EXP_PER_GROUP), lg, neg)
    m1 = jnp.max(le, axis=1, keepdims=True)
    i1 = jnp.min(jnp.where(le == m1, lane, big), axis=1, keepdims=True)
    le2 = jnp.where(lane == i1, neg, le)
    m2 = jnp.max(le2, axis=1, keepdims=True)
    i2 = jnp.min(jnp.where(le2 == m2, lane, big), axis=1, keepdims=True)
    e2 = jnp.exp(m2 - m1)
    wa = p_grp / (1.0 + e2)
    wb = p_grp * e2 / (1.0 + e2)
    ea = i1 - N_GROUPS
    eb = i2 - N_GROUPS
    onehot = jnp.where((lane == ea) | (lane == eb), 1.0, 0.0)
    ri_ = lax.broadcasted_iota(jnp.int32, (tm, tm), 0)
    ci_ = lax.broadcasted_iota(jnp.int32, (tm, tm), 1)
    below = jnp.where(ci_ < ri_, 1.0, 0.0).astype(BF16)
    before = _dot(below, onehot.astype(BF16)) + carry_scr[...]
    rank_a = jnp.sum(jnp.where(lane == ea, before, 0.0), axis=1, keepdims=True).astype(jnp.int32)
    rank_b = jnp.sum(jnp.where(lane == eb, before, 0.0), axis=1, keepdims=True).astype(jnp.int32)
    carry_scr[...] = carry_scr[...] + jnp.sum(onehot, axis=0, keepdims=True)
    ri = jnp.where(lane == 0, ea, jnp.where(lane == 1, eb,
                   jnp.where(lane == 2, rank_a, jnp.where(lane == 3, rank_b, 0))))
    ri_t = ri.T
    for q in range(tm // TM):
        ri_ref[q] = ri_t[0:SUBLANE, q * TM:(q + 1) * TM]
    rw_ref[...] = jnp.where(lane == 0, wa, jnp.where(lane == 1, wb, 0.0))
    cnt_ref[...] = jnp.broadcast_to(carry_scr[...], cnt_ref.shape)
    for q in range(per_step // zr):
        zero_copy(q).wait()


def _route(lg, sorted_rows):
    n = lg.shape[0]
    k = next(c for c in (4, 3, 2, 1) if (n // TM) % c == 0)
    steps = n // (k * TM)
    per_step = -(-sorted_rows // (steps * ZERO_ROWS)) * ZERO_ROWS
    tok = pl.BlockSpec((k * TM, ROUTE_W), lambda i: (i, 0))
    return pl.pallas_call(
        functools.partial(_route_kernel, per_step=per_step),
        grid=(steps,),
        in_specs=[tok],
        out_specs=[pl.BlockSpec((k, SUBLANE, TM), lambda i: (i, 0, 0)), tok,
                   pl.BlockSpec((8, ROUTE_W), lambda i: (0, 0)),
                   pl.BlockSpec(memory_space=pl.ANY)],
        out_shape=[jax.ShapeDtypeStruct((n // TM, SUBLANE, TM), jnp.int32),
                   jax.ShapeDtypeStruct((n, ROUTE_W), F32),
                   jax.ShapeDtypeStruct((8, ROUTE_W), F32),
                   jax.ShapeDtypeStruct((steps * per_step, D_MODEL // 2), jnp.uint32)],
        scratch_shapes=[pltpu.VMEM((1, ROUTE_W), F32),
                        pltpu.VMEM((ZERO_ROWS, D_MODEL // 2), jnp.uint32),
                        pltpu.SemaphoreType.DMA(())],
        compiler_params=_params(("arbitrary",)),
        name="route",
    )(lg)


def _dispatch_kernel(pos_ref, h_ref, xs_in_ref, xs_ref, sem):
    del xs_in_ref
    tm = h_ref.shape[0]

    def copy(r, slot):
        return pltpu.make_async_copy(h_ref.at[pl.ds(r, 1)],
                                     xs_ref.at[pl.ds(pos_ref[0, slot, r], 1)], sem)

    def start(r):
        copy(r, 0).start(priority=0)
        copy(r, 1).start(priority=1)

    def wait(r):
        copy(r, 0).wait()
        copy(r, 1).wait()

    for r in range(tm):
        start(r)
    for r in range(tm):
        wait(r)


def _dispatch(pos, h2, xs, *, tile0):
    n = h2.shape[0]
    return pl.pallas_call(
        _dispatch_kernel,
        grid=(n // TM,),
        in_specs=[
            pl.BlockSpec((1, 2, TM), lambda i: (i + tile0, 0, 0), memory_space=pltpu.SMEM),
            pl.BlockSpec((TM, D_MODEL // 2), lambda i: (i, 0)),
            pl.BlockSpec(memory_space=pl.ANY),
        ],
        out_specs=pl.BlockSpec(memory_space=pl.ANY),
        out_shape=jax.ShapeDtypeStruct(xs.shape, xs.dtype),
        scratch_shapes=[pltpu.SemaphoreType.DMA(())],
        input_output_aliases={2: 0},
        compiler_params=_params(("arbitrary",)),
        name="dispatch",
    )(pos, h2, xs)


def _experts_kernel(te_ref, tv_ref, xs_ref, wg_ref, wu_ref, wd_ref, y_ref, wg_scr, wu_scr, wd_scr):
    t = pl.program_id(0)
    new_expert = jnp.logical_or(t == 0, te_ref[t] != te_ref[jnp.maximum(t - 1, 0)])

    @pl.when(new_expert)
    def _():
        wg_scr[...] = wg_ref[0].astype(BF16)
        wu_scr[...] = wu_ref[0].astype(BF16)
        wd_scr[...] = wd_ref[0].astype(BF16)

    @pl.when(tv_ref[t] == 1)
    def _():
        x = _unpack_bf16_pairs(xs_ref[...])
        gt = _dot(x, wg_scr[...])
        up = _dot(x, wu_scr[...])
        hid = (gt * jax.nn.sigmoid(gt) * up).astype(BF16)
        y_ref[...] = _pack_bf16_pairs(_dot(hid, wd_scr[...]))

    @pl.when(tv_ref[t] == 0)
    def _():
        y_ref[...] = jnp.zeros_like(y_ref)


def _experts(te, tv, xs, wg, wu, wd):
    p = te.shape[0] * TMX
    grid_spec = pltpu.PrefetchScalarGridSpec(
        num_scalar_prefetch=2,
        grid=(p // TMX,),
        in_specs=[
            pl.BlockSpec((TMX, D_MODEL // 2), lambda t, te, tv: (t, 0)),
            pl.BlockSpec((1, D_MODEL, D_EXPERT), lambda t, te, tv: (te[t], 0, 0)),
            pl.BlockSpec((1, D_MODEL, D_EXPERT), lambda t, te, tv: (te[t], 0, 0)),
            pl.BlockSpec((1, D_EXPERT, D_MODEL), lambda t, te, tv: (te[t], 0, 0)),
        ],
        out_specs=pl.BlockSpec((TMX, D_MODEL // 2), lambda t, te, tv: (t, 0)),
        scratch_shapes=[pltpu.VMEM((D_MODEL, D_EXPERT), BF16), pltpu.VMEM((D_MODEL, D_EXPERT), BF16),
                        pltpu.VMEM((D_EXPERT, D_MODEL), BF16)],
    )
    return pl.pallas_call(
        _experts_kernel,
        grid_spec=grid_spec,
        out_shape=jax.ShapeDtypeStruct((p, D_MODEL // 2), jnp.uint32),
        compiler_params=_params(("arbitrary",)),
        name="experts",
    )(te, tv, xs, wg, wu, wd)


def _combine_kernel(pos_ref, x1_ref, rw_ref, fg_ref, ys_ref, o_ref, ya_scr, yb_scr, sem):
    tm = x1_ref.shape[0]

    def copies(r):
        return (pltpu.make_async_copy(ys_ref.at[pl.ds(pos_ref[0, 0, r], 1)], ya_scr.at[pl.ds(r, 1)], sem),
                pltpu.make_async_copy(ys_ref.at[pl.ds(pos_ref[0, 1, r], 1)], yb_scr.at[pl.ds(r, 1)], sem))

    def start(r):
        ca, cb = copies(r)
        ca.start(priority=0)
        cb.start(priority=1)

    def wait(r):
        ca, cb = copies(r)
        ca.wait()
        cb.wait()

    for r in range(tm):
        start(r)
    for r in range(tm):
        wait(r)
    rw = rw_ref[...]
    ya = _unpack_bf16_pairs(ya_scr[...]).astype(F32)
    yb = _unpack_bf16_pairs(yb_scr[...]).astype(F32)
    x2 = x1_ref[...] + rw[:, 0:1] * ya + rw[:, 1:2] * yb
    o_ref[...] = _rms(x2, fg_ref[...])


def _combine(pos, x1, rw, fg, ys, *, tile0):
    n = x1.shape[0]
    tok_m = pl.BlockSpec((TM, D_MODEL), lambda i: (i, 0))
    return pl.pallas_call(
        _combine_kernel,
        grid=(n // TM,),
        in_specs=[
            pl.BlockSpec((1, 2, TM), lambda i: (i + tile0, 0, 0), memory_space=pltpu.SMEM),
            tok_m,
            pl.BlockSpec((TM, ROUTE_W), lambda i: (i + tile0, 0)),
            _const_spec((1, D_MODEL)),
            pl.BlockSpec(memory_space=pl.ANY),
        ],
        out_specs=tok_m,
        out_shape=jax.ShapeDtypeStruct((n, D_MODEL), F32),
        scratch_shapes=[pltpu.VMEM((TM, D_MODEL // 2), jnp.uint32),
                        pltpu.VMEM((TM, D_MODEL // 2), jnp.uint32),
                        pltpu.SemaphoreType.DMA(())],
        compiler_params=_params(("arbitrary",)),
        name="combine",
    )(pos, x1, rw, fg, ys)


def _row(v):
    return v.reshape(1, -1).astype(F32)


def _prepare_weights(norm1_g, w_in, b_conv_in, b_gate, mu_shift, w0, w_lora2, a0, a_lora2, g_lora2,
                     k_k, k_a, r_k, lnx_g, lnx_b, w_out_a, conv_w, conv_b, cln_g, cln_b, w_out_b, w_o,
                     norm2_g, w_rg, b_rg, w_re, b_re, w_gate_e, w_up_e, w_down_e, final_g):
    w = {}
    w["n1"] = _row(norm1_g)
    w["wr"] = _permute_cols(w_in[:, :SHIFT_W]).astype(BF16)
    w["wc"] = w_in[:, SHIFT_W:SHIFT_W + 2 * D_B].astype(BF16)
    w["wg"] = w_in[:, SHIFT_W + 2 * D_B:].astype(BF16)
    w["bc"] = _row(b_conv_in)
    w["bg"] = _row(b_gate)
    w["mu"] = _permute_cols(mu_shift).reshape(1, P_W)
    w["vp"] = jnp.concatenate([jnp.stack([w0, a0, k_k, k_a, r_k.reshape(-1)]),
                               jnp.zeros((3, D_A), F32)], axis=0)
    z = jnp.zeros((LORA_W, D_A), F32)
    w2 = jnp.concatenate([jnp.concatenate([w_lora2, z], axis=1),
                          jnp.concatenate([z, a_lora2], axis=1)], axis=0)
    w["w2h"], w["w2l"] = _split(w2)
    g2 = jnp.pad(g_lora2, ((0, 2 * LANE - LORA_G), (0, 0)))
    w["g2h"], w["g2l"] = _split(g2)
    w["lng"] = _row(lnx_g)
    w["lnb"] = _row(lnx_b)
    w["woa"] = w_out_a.astype(BF16)
    w["wob"] = w_out_b.astype(BF16)
    w["cw"] = jnp.pad(conv_w, ((0, HALO - CONV_W), (0, 0))).reshape(HALO, D_B // LANE, LANE).transpose(1, 0, 2)
    w["cb"] = conv_b.reshape(D_B // LANE, 1, LANE)
    w["clg"] = _row(cln_g)
    w["clb"] = _row(cln_b)
    w["wo"] = w_o.astype(BF16)
    w["n2"] = _row(norm2_g)
    pad_r = ROUTE_W - N_GROUPS - N_EXPERTS
    w["wrh"], w["wrl"] = _split(jnp.pad(jnp.concatenate([w_rg, w_re], axis=1), ((0, 0), (0, pad_r))))
    w["brt"] = jnp.pad(jnp.concatenate([b_rg, b_re]), (0, pad_r)).reshape(1, ROUTE_W)
    w["wge"] = w_gate_e
    w["wue"] = w_up_e
    w["wde"] = w_down_e
    w["fg"] = _row(final_g)
    return w


def _trunk(x, shift0, wkv0, conv0, w, *, chunk, nchunk, intra_chunks, conv_tm):
    b, t, _ = x.shape
    n = b * t
    xf = x.reshape(n, D_MODEL)
    shift_p = _permute_cols(shift0)
    tps = max(t // TM, 1)
    nb, step = (1, TM) if t >= TM else (TM // t, t)
    u, tails, rt, at, bt, kt, vv, ge, bonus, g = _inprep(
        xf, w["n1"], w["wr"], w["wc"], w["bc"], shift_p.reshape(-1, nb, P_W), w["mu"], w["vp"],
        w["w2h"], w["w2l"], w["g2h"], w["g2l"], chunk=chunk, nb=nb, step=step, tps=tps)
    rc, yc, m2, n2 = _wkv_intra(rt, at, bt, kt, vv, ge, chunk=chunk, nchunk=intra_chunks)
    y, wkv_new = _wkv_scan(rc, yc, m2, n2, wkv0, n_streams=b, t_len=t, chunk=chunk, nchunk=nchunk)
    hinit = jnp.pad(conv0, ((0, 0), (HALO - (CONV_W - 1), 0), (0, 0)))
    ob = _conv(u, hinit, w["cw"], w["cb"], w["clg"], w["clb"], n_streams=b, t_len=t, tm=conv_tm)
    merged = _merge(xf, y, bonus, g, ob, w["n1"], w["wg"], w["bg"], w["lng"], w["lnb"], w["woa"], w["wob"])
    x1, h2, lg = _outproj(merged, xf, w["wo"], w["n2"], w["wrh"], w["wrl"], w["brt"])
    new_shift = tails.reshape(b, -1, P_W)[:, -1, :SHIFT_W][:, _INV_PERM]
    ext_tail = jnp.concatenate([conv0, u.reshape(b, t, D_B)[:, -(CONV_W - 1):]], axis=1)[:, -(CONV_W - 1):]
    return x1, h2, lg, new_shift, wkv_new, ext_tail


def kernel(x_prompt, x_sample, state_shift, state_wkv, cache_conv, norm1_g, w_in, b_conv_in, b_gate,
           mu_shift, w0, w_lora2, a0, a_lora2, g_lora2, k_k, k_a, r_k, lnx_g, lnx_b, w_out_a, conv_w,
           conv_b, cln_g, cln_b, w_out_b, w_o, norm2_g, w_rg, b_rg, w_re, b_re, w_gate_e, w_up_e,
           w_down_e, final_g):
    per_layer = (norm1_g, w_in, b_conv_in, b_gate, mu_shift, w0, w_lora2, a0, a_lora2, g_lora2, k_k, k_a,
                 r_k, lnx_g, lnx_b, w_out_a, conv_w, conv_b, cln_g, cln_b, w_out_b, w_o, norm2_g, w_rg,
                 b_rg, w_re, b_re, w_gate_e, w_up_e, w_down_e)
    assert norm1_g.shape[0] == 1, "single layer only"
    w = _prepare_weights(*[p[0] for p in per_layer], final_g)
    bp, tp, _ = x_prompt.shape
    bs, ts, _ = x_sample.shape
    zero_shift = jnp.zeros((bp, SHIFT_W), F32)
    zero_wkv = jnp.zeros((bp, N_HEADS, HEAD, HEAD), F32)
    zero_conv = jnp.zeros((bp, CONV_W - 1, D_B), F32)
    x1p, h2p, lgp, shift_p, wkv_p, conv_p = _trunk(
        x_prompt, zero_shift, zero_wkv, zero_conv, w, chunk=64, nchunk=min(8, tp // 64),
        intra_chunks=min(16, tp // 64), conv_tm=TM)
    x1s, h2s, lgs, shift_s, wkv_s, conv_s = _trunk(
        x_sample, state_shift[0], state_wkv[0], cache_conv[0], w, chunk=ts, nchunk=1,
        intra_chunks=TM // ts, conv_tm=ts)

    n_p, n_s = bp * tp, bs * ts
    n_all = n_p + n_s
    n_tiles = (2 * n_all + N_EXPERTS * (TMX - 1) + TMX - 1) // TMX
    ri, rw, cnt, xs = _route(jnp.concatenate([lgp, lgs], axis=0), n_tiles * TMX)
    counts = cnt[0, :N_EXPERTS].astype(jnp.int32)
    tiles_per = (counts + TMX - 1) // TMX
    tile_end = jnp.cumsum(tiles_per)
    offs = (tile_end - tiles_per) * TMX
    t_idx = jnp.arange(n_tiles, dtype=jnp.int32)
    te = jnp.sum(t_idx[:, None] >= tile_end[None, :], axis=1).astype(jnp.int32)
    tv = (t_idx < tile_end[-1]).astype(jnp.int32)
    e_ids = jnp.arange(N_EXPERTS, dtype=jnp.int32)
    last_e = jnp.max(jnp.where(tiles_per > 0, e_ids, 0))
    te = jnp.where(tv == 1, jnp.minimum(te, N_EXPERTS - 1), last_e)
    base = jnp.sum(jnp.where(ri[:, 0:2, :, None] == e_ids, offs, 0), axis=-1)
    pos = base + ri[:, 2:4, :]
    xs = _dispatch(pos, h2p, xs, tile0=0)
    xs = _dispatch(pos, h2s, xs, tile0=n_p // TM)
    ys = _experts(te, tv, xs, w["wge"], w["wue"], w["wde"])
    yp = _combine(pos, x1p, rw, w["fg"], ys, tile0=0)
    ysm = _combine(pos, x1s, rw, w["fg"], ys, tile0=n_p // TM)
    return (yp.reshape(bp, tp, D_MODEL), ysm.reshape(bs, ts, D_MODEL),
            shift_p[None], wkv_p[None], conv_p[None],
            shift_s[None], wkv_s[None], conv_s[None])
```

```python
import functools
import math

import numpy as np
import jax
import jax.numpy as jnp
from jax import lax
from jax.experimental import pallas as pl
from jax.experimental.pallas import tpu as pltpu

F32 = jnp.float32
BF16 = jnp.bfloat16

D_MODEL = 2048
D_A = 1024
HEAD = 64
N_HEADS = D_A // HEAD
LORA_W = 64
LORA_A = 64
LORA_G = 160
SHIFT_W = 3 * D_A + LORA_W + LORA_A + LORA_G
D_B = 1024
CONV_W = 31
N_GROUPS = 4
EXP_PER_GROUP = 8
N_EXPERTS = N_GROUPS * EXP_PER_GROUP
D_EXPERT = 512
RMS_EPS = 1e-6
LN_EPS = 1e-5
GN_EPS = 64e-5

LANE = 128
SUBLANE = 8
P_W = 27 * LANE
LORA_IN0 = 3 * D_A
XG0 = LORA_IN0 + LANE
HALO = 32
ROUTE_W = LANE
TM = 256
TM_WIDE = 512
TMX = 256
ZERO_ROWS = 256
VMEM_LIMIT = 56 * 1024 * 1024

_PERM_RANGES = ((0, D_A), (D_A + LORA_W, 2 * D_A + LORA_W), (2 * D_A + LORA_W, 3 * D_A + LORA_W),
                (D_A, D_A + LORA_W), (3 * D_A + LORA_W, SHIFT_W))
_PERM = np.concatenate([np.arange(a, b) for a, b in _PERM_RANGES])
_INV_PERM = np.argsort(_PERM)
assert len(_PERM) == SHIFT_W and (np.sort(_PERM) == np.arange(SHIFT_W)).all()


def _permute_cols(x):
    pad = jnp.zeros(x.shape[:-1] + (P_W - SHIFT_W,), x.dtype)
    return jnp.concatenate([x[..., a:b] for a, b in _PERM_RANGES] + [pad], axis=-1)


def _dot(a, b):
    return jnp.dot(a, b, preferred_element_type=F32)


def _dot_nt(a, b):
    return lax.dot_general(a, b, (((1,), (1,)), ((), ())), preferred_element_type=F32)


def _dot_tn(a, b):
    return lax.dot_general(a, b, (((0,), (0,)), ((), ())), preferred_element_type=F32)


def _split(x):
    hi = x.astype(BF16)
    lo = (x - hi.astype(F32)).astype(BF16)
    return hi, lo


def _dot3(x, w_hi, w_lo):
    x_hi, x_lo = _split(x)
    return _dot(x_hi, w_hi) + (_dot(x_lo, w_hi) + _dot(x_hi, w_lo))


def _sigmoid(x):
    return 0.5 * jnp.tanh(0.5 * x) + 0.5


def _rms(x, g):
    return x * lax.rsqrt(jnp.mean(x * x, axis=-1, keepdims=True) + RMS_EPS) * g


def _head_sum(x):
    w = 2 * LANE
    r = lax.broadcasted_iota(jnp.int32, (w, w), 0) // HEAD
    c = lax.broadcasted_iota(jnp.int32, (w, w), 1) // HEAD
    ones = jnp.where(r == c, 1.0, 0.0).astype(BF16)
    outs = [_dot(x[:, j * w:(j + 1) * w].astype(BF16), ones) for j in range(x.shape[1] // w)]
    return jnp.concatenate(outs, axis=1)


def _const_spec(shape):
    nd = len(shape)
    return pl.BlockSpec(shape, lambda *_: (0,) * nd, pipeline_mode=pl.Buffered(1))


def _params(sem):
    return pltpu.CompilerParams(dimension_semantics=sem, vmem_limit_bytes=VMEM_LIMIT)


def _inprep_kernel(x_ref, n1_ref, wr_ref, wc_ref, bc_ref, bnd_ref, mu_ref, vp_ref, w2h_ref, w2l_ref,
                   g2h_ref, g2l_ref,
                   u_ref, tail_ref, rt_ref, at_ref, bt_ref, kt_ref, v_ref, ge_ref, bonus_ref, g_ref,
                   first_scr, last_scr, *, chunk, nb, step, tps):
    tm = x_ref.shape[0]
    if nb == 1:
        starts_stream = lax.rem(pl.program_id(0), tps) == 0

        @pl.when(starts_stream)
        def _():
            first_scr[...] = bnd_ref[0, 0:1, :]

        @pl.when(jnp.logical_not(starts_stream))
        def _():
            first_scr[...] = last_scr[...]

    h = _rms(x_ref[...], n1_ref[...]).astype(BF16)
    p = _dot(h, wr_ref[...])
    for j in range(nb):
        tail_ref[0, j:j + 1, :] = p[(j + 1) * step - 1:(j + 1) * step, :]
    c = _dot(h, wc_ref[...]) + bc_ref[...]
    u_ref[...] = c[:, :D_B] * _sigmoid(c[:, D_B:])

    row = lax.broadcasted_iota(jnp.int32, (tm, 1), 0)
    prev = pltpu.roll(p, 1, axis=0)
    if nb == 1:
        prev = jnp.where(row == 0, first_scr[...], prev)
        last_scr[...] = p[tm - 1:tm, :]
    else:
        for j in range(nb):
            prev = jnp.where(row == j * step, bnd_ref[0, j:j + 1, :], prev)
    xs = p + (prev - p) * mu_ref[...]
    r = xs[:, 0:D_A]
    k = xs[:, D_A:2 * D_A]
    v = xs[:, 2 * D_A:3 * D_A]
    lin = xs[:, LORA_IN0:LORA_IN0 + LANE]
    lane = lax.broadcasted_iota(jnp.int32, (1, LANE), 1)
    lin = jnp.where(lane < LORA_W, jnp.tanh(lin), lin)
    lo = _dot3(lin, w2h_ref[...], w2l_ref[...])
    w0 = vp_ref[0:1, :]
    a0 = vp_ref[1:2, :]
    k_k = vp_ref[2:3, :]
    k_a = vp_ref[3:4, :]
    r_k = vp_ref[4:5, :]
    logw = -math.exp(-0.5) * _sigmoid(w0 + lo[:, :D_A])
    a = _sigmoid(a0 + lo[:, D_A:])
    g = _dot3(_sigmoid(xs[:, XG0:XG0 + 2 * LANE]), g2h_ref[...], g2l_ref[...])
    kku = k * k_k
    kk = kku / jnp.maximum(jnp.sqrt(_head_sum(kku * kku)), 1e-12)
    k2 = k * (1.0 + (a - 1.0) * k_a)
    bonus_ref[...] = (_head_sum(r * k2 * r_k) * v).astype(BF16)
    g_ref[...] = g.astype(BF16)
    ri = lax.broadcasted_iota(jnp.int32, (tm, tm), 0)
    ci = lax.broadcasted_iota(jnp.int32, (tm, tm), 1)
    tri = jnp.where((ri // chunk == ci // chunk) & (ci <= ri), 1.0, 0.0).astype(BF16)
    lw_hi, lw_lo = _split(logw)
    cum = _dot(tri, lw_hi) + _dot(tri, lw_lo)
    eg = jnp.exp(cum)
    egi = jnp.exp(-cum)
    rt_ref[...] = (r * eg).astype(BF16)
    at_ref[...] = (-kk * jnp.exp(cum - logw)).astype(BF16)
    bt_ref[...] = (kk * a * egi).astype(BF16)
    kt_ref[...] = (k2 * egi).astype(BF16)
    v_ref[...] = v.astype(BF16)
    for cc in range(tm // chunk):
        ge_ref[cc] = eg[(cc + 1) * chunk - 1:(cc + 1) * chunk, :]


def _inprep(x, n1, wr, wc, bc, bnd, mu, vp, w2h, w2l, g2h, g2l, *, chunk, nb, step, tps):
    n = x.shape[0]
    tok = pl.BlockSpec((TM, D_A), lambda i: (i, 0))
    return pl.pallas_call(
        functools.partial(_inprep_kernel, chunk=chunk, nb=nb, step=step, tps=tps),
        grid=(n // TM,),
        in_specs=[
            pl.BlockSpec((TM, D_MODEL), lambda i: (i, 0)),
            _const_spec((1, D_MODEL)),
            _const_spec((D_MODEL, P_W)),
            _const_spec((D_MODEL, 2 * D_B)),
            _const_spec((1, 2 * D_B)),
            pl.BlockSpec((1, nb, P_W), lambda i: (i // tps, 0, 0)),
            _const_spec((1, P_W)),
            _const_spec((8, D_A)),
            _const_spec((LANE, 2 * D_A)),
            _const_spec((LANE, 2 * D_A)),
            _const_spec((2 * LANE, D_A)),
            _const_spec((2 * LANE, D_A)),
        ],
        out_specs=[tok,
                   pl.BlockSpec((1, nb, P_W), lambda i: (i, 0, 0)),
                   tok, tok, tok, tok, tok,
                   pl.BlockSpec((TM // chunk, 1, D_A), lambda i: (i, 0, 0)),
                   tok, tok],
        out_shape=[jax.ShapeDtypeStruct((n, D_B), F32),
                   jax.ShapeDtypeStruct((n // TM, nb, P_W), F32)]
        + [jax.ShapeDtypeStruct((n, D_A), BF16)] * 5
        + [jax.ShapeDtypeStruct((n // chunk, 1, D_A), F32),
           jax.ShapeDtypeStruct((n, D_A), BF16),
           jax.ShapeDtypeStruct((n, D_A), BF16)],
        scratch_shapes=[pltpu.VMEM((1, P_W), F32), pltpu.VMEM((1, P_W), F32)],
        compiler_params=_params(("arbitrary",)),
        name="inprep",
    )(x, n1, wr, wc, bc, bnd, mu, vp, w2h, w2l, g2h, g2l)


def _wkv_intra_kernel(rt_ref, at_ref, bt_ref, kt_ref, v_ref, ge_ref, rc_ref, yc_ref, m2_ref, n2_ref,
                      *, chunk, nchunk):
    c = chunk
    w2 = 2 * c
    row = lax.broadcasted_iota(jnp.int32, (c, w2), 0)
    col = lax.broadcasted_iota(jnp.int32, (c, w2), 1)
    colc = jnp.where(col >= c, col - c, col)
    strict = colc < row
    incl = colc <= row
    eye_c = jnp.where(colc == row, 1.0, 0.0).astype(F32)
    tok0 = lax.broadcasted_iota(jnp.int32, (1, w2), 1) < c
    ch0 = lax.broadcasted_iota(jnp.int32, (1, LANE), 1) < HEAD
    rk = lax.broadcasted_iota(jnp.int32, (HEAD, LANE), 0)
    ck = lax.broadcasted_iota(jnp.int32, (HEAD, LANE), 1)
    eye_k = jnp.where(jnp.where(ck >= HEAD, ck - HEAD, ck) == rk, 1.0, 0.0).astype(F32)
    n_sq = int(math.log2(c)) - 1

    def bd(x, first):
        zero = jnp.zeros_like(x)
        return jnp.concatenate([jnp.where(first, x, zero), jnp.where(first, zero, x)], axis=0)

    def head_blocks(x):
        return [jnp.where(ch0, x[:HEAD, j * LANE:(j + 1) * LANE], x[HEAD:, j * LANE:(j + 1) * LANE])
                for j in range(x.shape[1] // LANE)]

    rng = range(nchunk)
    rows = [slice(ci * c, (ci + 1) * c) for ci in rng]
    at = [at_ref[r, :] for r in rows]
    rt = [rt_ref[r, :] for r in rows]
    bt = [bt_ref[r, :] for r in rows]
    kt = [kt_ref[r, :] for r in rows]
    vv = [v_ref[r, :] for r in rows]
    m = [_dot_nt(jnp.concatenate([at[i], rt[i]], axis=0),
                 jnp.concatenate([bd(bt[i], ch0), bd(kt[i], ch0)], axis=0)) for i in rng]
    a_ab = [jnp.where(strict, x[:c, :w2], 0.0) for x in m]
    a_rb = [jnp.where(incl, x[c:, :w2], 0.0).astype(BF16) for x in m]
    akv = [_dot(jnp.concatenate([jnp.where(strict, x[:c, w2:], 0.0), jnp.where(incl, x[c:, w2:], 0.0)],
                                axis=0).astype(BF16), bd(vv[i], ch0)) for i, x in enumerate(m)]
    kv = [head_blocks(_dot_tn(kt[i], vv[i]))[0] for i in rng]
    t = [eye_c + x for x in a_ab]
    pw = [x.astype(BF16) for x in a_ab]
    pw = [_dot(x, bd(x, tok0)).astype(BF16) for x in pw]
    for _ in range(n_sq - 1):
        st = [_dot(jnp.concatenate([t[i].astype(BF16), pw[i]], axis=0), bd(pw[i], tok0)) for i in rng]
        t = [t[i] + st[i][:c] for i in rng]
        pw = [x[c:].astype(BF16) for x in st]
    t = [t[i] + _dot(t[i].astype(BF16), bd(pw[i], tok0)) for i in rng]
    twb = [_dot(t[i].astype(BF16),
                jnp.concatenate([bd(at[i], ch0), bd(akv[i][:c].astype(BF16), ch0)], axis=1)).astype(BF16)
           for i in rng]
    rb = [_dot(a_rb[i], jnp.concatenate([bd(twb[i][:, :LANE], ch0), bd(twb[i][:, LANE:], ch0)], axis=1))
          for i in rng]
    xb = [head_blocks(_dot_tn(bt[i], twb[i])) for i in rng]
    for i in rng:
        rc_ref[rows[i], :] = (rt[i].astype(F32) + rb[i][:, :LANE]).astype(BF16)
        yc_ref[rows[i], :] = rb[i][:, LANE:] + akv[i][c:]
        g_full = jnp.broadcast_to(ge_ref[i], (LANE, LANE)).T
        g_col = jnp.where(ch0, g_full[:HEAD], g_full[HEAD:])
        m2_ref[i] = (g_col * (eye_k + xb[i][0]) - eye_k).astype(BF16)
        n2_ref[i] = g_col * (xb[i][1] + kv[i])


def _wkv_intra(rt, at, bt, kt, v, ge, *, chunk, nchunk):
    n = rt.shape[0]
    rows = chunk * nchunk
    tok = pl.BlockSpec((rows, LANE), lambda i, hp: (i, hp))
    mat = pl.BlockSpec((nchunk, HEAD, LANE), lambda i, hp: (i, 0, hp))
    return pl.pallas_call(
        functools.partial(_wkv_intra_kernel, chunk=chunk, nchunk=nchunk),
        grid=(n // rows, N_HEADS // 2),
        in_specs=[tok, tok, tok, tok, tok,
                  pl.BlockSpec((nchunk, 1, LANE), lambda i, hp: (i, 0, hp))],
        out_specs=[tok, tok, mat, mat],
        out_shape=[jax.ShapeDtypeStruct((n, D_A), BF16),
                   jax.ShapeDtypeStruct((n, D_A), F32),
                   jax.ShapeDtypeStruct((n // chunk, HEAD, D_A), BF16),
                   jax.ShapeDtypeStruct((n // chunk, HEAD, D_A), F32)],
        compiler_params=_params(("parallel", "parallel")),
        name="wkv_intra",
    )(rt, at, bt, kt, v, ge)


def _wkv_scan_kernel(rc_ref, yc_ref, m2_ref, n2_ref, s0_ref, y_ref, st_ref, h_scr, *, chunk, nchunk):
    j = pl.program_id(1)

    @pl.when(j == 0)
    def _():
        for h in range(N_HEADS):
            h_scr[:, h * HEAD:(h + 1) * HEAD] = s0_ref[0, h].T

    c = chunk

    def chunk_body(ci, carry):
        off = pl.multiple_of(ci * c, c)
        h_all = h_scr[...]
        hb = h_all.astype(BF16)
        rc = rc_ref[pl.ds(off, c), :]
        m2 = m2_ref[ci]
        ch0 = lax.broadcasted_iota(jnp.int32, (1, LANE), 1) < HEAD
        zero = jnp.zeros((HEAD, LANE), BF16)
        ys, mh = [], []
        for hp in range(N_HEADS // 2):
            sl = slice(hp * LANE, (hp + 1) * LANE)
            hpair = hb[:, sl]
            hbd = jnp.concatenate([jnp.where(ch0, hpair, zero), jnp.where(ch0, zero, hpair)], axis=0)
            both = _dot(jnp.concatenate([rc[:, sl], m2[:, sl]], axis=0), hbd)
            ys.append(both[:c])
            mh.append(both[c:])
        y_ref[pl.ds(off, c), :] = (jnp.concatenate(ys, axis=1) + yc_ref[pl.ds(off, c), :]).astype(BF16)
        h_scr[...] = h_all + jnp.concatenate(mh, axis=1) + n2_ref[ci]
        return carry

    lax.fori_loop(0, nchunk, chunk_body, 0)

    @pl.when(j == pl.num_programs(1) - 1)
    def _():
        for h in range(N_HEADS):
            st_ref[0, h] = h_scr[:, h * HEAD:(h + 1) * HEAD].T


def _wkv_scan(rc, yc, m2, n2, s0, *, n_streams, t_len, chunk, nchunk):
    n = rc.shape[0]
    rows = chunk * nchunk
    nj = t_len // rows
    tok = pl.BlockSpec((rows, D_A), lambda b, j: (b * nj + j, 0))
    mat = pl.BlockSpec((nchunk, HEAD, D_A), lambda b, j: (b * nj + j, 0, 0))
    st = pl.BlockSpec((1, N_HEADS, HEAD, HEAD), lambda b, j: (b, 0, 0, 0))
    return pl.pallas_call(
        functools.partial(_wkv_scan_kernel, chunk=chunk, nchunk=nchunk),
        grid=(n_streams, nj),
        in_specs=[tok, tok, mat, mat, st],
        out_specs=[tok, st],
        out_shape=[jax.ShapeDtypeStruct((n, D_A), BF16),
                   jax.ShapeDtypeStruct((n_streams, N_HEADS, HEAD, HEAD), F32)],
        scratch_shapes=[pltpu.VMEM((HEAD, D_A), F32)],
        compiler_params=_params(("parallel", "arbitrary")),
        name="wkv_scan",
    )(rc, yc, m2, n2, s0)


CONV_ROWS = 128


def _conv_kernel(u_ref, hown_ref, hinit_ref, cw_ref, cb_ref, lg_ref, lb_ref, o_ref, ext_scr, sh_scr, z_scr):
    tm = u_ref.shape[0]
    n_lt = D_B // LANE
    first = pl.program_id(1) == 0

    @pl.when(first)
    def _():
        for l in range(n_lt):
            ext_scr[l, 0:HALO, :] = hinit_ref[0, :, l * LANE:(l + 1) * LANE]

    @pl.when(jnp.logical_not(first))
    def _():
        for l in range(n_lt):
            ext_scr[l, 0:HALO, :] = hown_ref[:, l * LANE:(l + 1) * LANE]

    for l in range(n_lt):
        ext_scr[l, HALO:HALO + tm, :] = u_ref[:, l * LANE:(l + 1) * LANE]
    span = sh_scr.shape[1]
    lead = HALO - (CONV_W - 1)

    def lane_tile(l, carry):
        for s in range(1, SUBLANE):
            sh_scr[s - 1] = ext_scr[l, s:s + span, :]
        acc = jnp.broadcast_to(cb_ref[l], (tm, LANE))
        for j in range(CONV_W):
            s, q = (lead + j) % SUBLANE, (lead + j) // SUBLANE
            r0 = q * SUBLANE
            win = ext_scr[l, r0:r0 + tm, :] if s == 0 else sh_scr[s - 1, r0:r0 + tm, :]
            acc = acc + cw_ref[l, j:j + 1, :] * win
        z_scr[l] = acc
        return carry

    lax.fori_loop(0, n_lt, lane_tile, 0)
    rows = min(CONV_ROWS, tm)

    def norm(i, carry):
        r0 = pl.multiple_of(i * rows, rows)
        z = jnp.concatenate([z_scr[l, pl.ds(r0, rows), :] for l in range(n_lt)], axis=1)
        mu = jnp.mean(z, axis=-1, keepdims=True)
        d = z - mu
        var = jnp.mean(d * d, axis=-1, keepdims=True)
        zn = d * lax.rsqrt(var + LN_EPS) * lg_ref[...] + lb_ref[...]
        o_ref[pl.ds(r0, rows), :] = (zn * _sigmoid(zn)).astype(BF16)
        return carry

    lax.fori_loop(0, tm // rows, norm, 0)


def _conv(u, hinit, cw, cb, lg, lb, *, n_streams, t_len, tm):
    n = u.shape[0]
    tps = t_len // tm
    if tps > 1:
        blocks_per_tile = tm // HALO
        own_map = lambda b, i: (jnp.maximum((b * tps + i) * blocks_per_tile - 1, 0), 0)
    else:
        own_map = lambda b, i: (0, 0)
    return pl.pallas_call(
        _conv_kernel,
        grid=(n_streams, tps),
        in_specs=[
            pl.BlockSpec((tm, D_B), lambda b, i: (b * tps + i, 0)),
            pl.BlockSpec((HALO, D_B), own_map),
            pl.BlockSpec((1, HALO, D_B), lambda b, i: (b, 0, 0)),
            _const_spec((D_B // LANE, HALO, LANE)),
            _const_spec((D_B // LANE, 1, LANE)),
            _const_spec((1, D_B)),
            _const_spec((1, D_B)),
        ],
        out_specs=pl.BlockSpec((tm, D_B), lambda b, i: (b * tps + i, 0)),
        out_shape=jax.ShapeDtypeStruct((n, D_B), BF16),
        scratch_shapes=[pltpu.VMEM((D_B // LANE, HALO + tm, LANE), F32),
                        pltpu.VMEM((SUBLANE - 1, HALO + tm - SUBLANE, LANE), F32),
                        pltpu.VMEM((D_B // LANE, tm, LANE), F32)],
        compiler_params=_params(("parallel", "arbitrary")),
        name="conv",
    )(u, u, hinit, cw, cb, lg, lb)


def _merge_kernel(x_ref, y_ref, bonus_ref, g_ref, ob_ref, n1_ref, wg_ref, bg_ref,
                  lng_ref, lnb_ref, woa_ref, wob_ref, o_ref):
    h = _rms(x_ref[...], n1_ref[...]).astype(BF16)
    y = y_ref[...].astype(F32)
    d = y - _head_sum(y) * (1.0 / HEAD)
    var = _head_sum(d * d) * (1.0 / HEAD)
    yn = d * lax.rsqrt(var + GN_EPS) * lng_ref[...] + lnb_ref[...]
    oa = ((yn + bonus_ref[...].astype(F32)) * g_ref[...].astype(F32)).astype(BF16)
    for j in range(2):
        cols = slice(j * D_A, (j + 1) * D_A)
        gcols = slice(D_MODEL + j * D_A, D_MODEL + (j + 1) * D_A)
        ga = _sigmoid(_dot(h, wg_ref[:, cols]) + bg_ref[:, cols])
        gb = _sigmoid(_dot(h, wg_ref[:, gcols]) + bg_ref[:, gcols])
        o_ref[:, cols] = (ga * _dot(oa, woa_ref[:, cols])
                          + gb * _dot(ob_ref[...], wob_ref[:, cols])).astype(BF16)


def _merge(x, y, bonus, g, ob, n1, wg, bg, lng, lnb, woa, wob):
    n = x.shape[0]
    tm = min(TM_WIDE, n)
    tok_a = pl.BlockSpec((tm, D_A), lambda i: (i, 0))
    tok_m = pl.BlockSpec((tm, D_MODEL), lambda i: (i, 0))
    return pl.pallas_call(
        _merge_kernel,
        grid=(n // tm,),
        in_specs=[tok_m, tok_a, tok_a, tok_a, tok_a,
                  _const_spec((1, D_MODEL)),
                  _const_spec((D_MODEL, 2 * D_MODEL)),
                  _const_spec((1, 2 * D_MODEL)),
                  _const_spec((1, D_A)),
                  _const_spec((1, D_A)),
                  _const_spec((D_A, D_MODEL)),
                  _const_spec((D_B, D_MODEL))],
        out_specs=tok_m,
        out_shape=jax.ShapeDtypeStruct((n, D_MODEL), BF16),
        compiler_params=_params(("parallel",)),
        name="merge",
    )(x, y, bonus, g, ob, n1, wg, bg, lng, lnb, woa, wob)


def _outproj_kernel(m_ref, x_ref, wo_ref, n2_ref, wrh_ref, wrl_ref, brt_ref, x1_ref, h2_ref, lg_ref):
    x1 = x_ref[...] + _dot(m_ref[...], wo_ref[...])
    x1_ref[...] = x1
    h2 = _rms(x1, n2_ref[...])
    h2_ref[...] = h2
    lg_ref[...] = _dot3(h2, wrh_ref[...], wrl_ref[...]) + brt_ref[...]


def _outproj(m, x, wo, n2, wrh, wrl, brt):
    n = x.shape[0]
    tm = min(TM_WIDE, n)
    tok_m = pl.BlockSpec((tm, D_MODEL), lambda i: (i, 0))
    return pl.pallas_call(
        _outproj_kernel,
        grid=(n // tm,),
        in_specs=[tok_m, tok_m,
                  _const_spec((D_MODEL, D_MODEL)),
                  _const_spec((1, D_MODEL)),
                  _const_spec((D_MODEL, ROUTE_W)),
                  _const_spec((D_MODEL, ROUTE_W)),
                  _const_spec((1, ROUTE_W))],
        out_specs=[tok_m, tok_m,
                   pl.BlockSpec((tm, ROUTE_W), lambda i: (i, 0))],
        out_shape=[jax.ShapeDtypeStruct((n, D_MODEL), F32),
                   jax.ShapeDtypeStruct((n, D_MODEL), F32),
                   jax.ShapeDtypeStruct((n, ROUTE_W), F32)],
        compiler_params=_params(("parallel",)),
        name="outproj",
    )(m, x, wo, n2, wrh, wrl, brt)


def _route_kernel(lg_ref, ri_ref, rw_ref, cnt_ref, xs_ref, carry_scr, zero_scr, sem, *, per_step):
    i = pl.program_id(0)
    tm = lg_ref.shape[0]

    @pl.when(i == 0)
    def _():
        carry_scr[...] = jnp.zeros_like(carry_scr)
        zero_scr[...] = jnp.zeros_like(zero_scr)

    zr = zero_scr.shape[0]

    def zero_copy(q):
        start = pl.multiple_of(i * per_step + q * zr, zr)
        return pltpu.make_async_copy(zero_scr, xs_ref.at[pl.ds(start, zr)], sem)

    for q in range(per_step // zr):
        zero_copy(q).start()

    lg = lg_ref[...]
    lane = lax.broadcasted_iota(jnp.int32, (tm, ROUTE_W), 1)
    neg = jnp.float32(-jnp.inf)
    big = jnp.int32(ROUTE_W)
    grp = jnp.where(lane < N_GROUPS, lg, neg)
    gmax = jnp.max(grp, axis=1, keepdims=True)
    g_sel = jnp.min(jnp.where(grp == gmax, lane, big), axis=1, keepdims=True)
    p_grp = 1.0 / jnp.sum(jnp.where(lane < N_GROUPS, jnp.exp(lg - gmax), 0.0), axis=1, keepdims=True)
    lo = N_GROUPS + g_sel * EXP_PER_GROUP
    le = jnp.where((lane >= lo) & (lane < lo + EXP_PER_GROUP), lg, neg)
    m1 = jnp.max(le, axis=1, keepdims=True)
    i1 = jnp.min(jnp.where(le == m1, lane, big), axis=1, keepdims=True)
    le2 = jnp.where(lane == i1, neg, le)
    m2 = jnp.max(le2, axis=1, keepdims=True)
    i2 = jnp.min(jnp.where(le2 == m2, lane, big), axis=1, keepdims=True)
    e2 = jnp.exp(m2 - m1)
    wa = p_grp / (1.0 + e2)
    wb = p_grp * e2 / (1.0 + e2)
    ea = i1 - N_GROUPS
    eb = i2 - N_GROUPS
    onehot = jnp.where((lane == ea) | (lane == eb), 1.0, 0.0)
    ri_ = lax.broadcasted_iota(jnp.int32, (tm, tm), 0)
    ci_ = lax.broadcasted_iota(jnp.int32, (tm, tm), 1)
    below = jnp.where(ci_ < ri_, 1.0, 0.0).astype(BF16)
    before = _dot(below, onehot.astype(BF16)) + carry_scr[...]
    rank_a = jnp.sum(jnp.where(lane == ea, before, 0.0), axis=1, keepdims=True).astype(jnp.int32)
    rank_b = jnp.sum(jnp.where(lane == eb, before, 0.0), axis=1, keepdims=True).astype(jnp.int32)
    carry_scr[...] = carry_scr[...] + jnp.sum(onehot, axis=0, keepdims=True)
    ri = jnp.where(lane == 0, ea, jnp.where(lane == 1, eb,
                   jnp.where(lane == 2, rank_a, jnp.where(lane == 3, rank_b, 0))))
    ri_t = ri.T
    for q in range(tm // TM):
        ri_ref[q] = ri_t[0:SUBLANE, q * TM:(q + 1) * TM]
    rw_ref[...] = jnp.where(lane == 0, wa, jnp.where(lane == 1, wb, 0.0))
    cnt_ref[...] = jnp.broadcast_to(carry_scr[...], cnt_ref.shape)
    for q in range(per_step // zr):
        zero_copy(q).wait()


def _route(lg, sorted_rows):
    n = lg.shape[0]
    k = next(c for c in (4, 3, 2, 1) if (n // TM) % c == 0)
    steps = n // (k * TM)
    per_step = -(-sorted_rows // (steps * ZERO_ROWS)) * ZERO_ROWS
    tok = pl.BlockSpec((k * TM, ROUTE_W), lambda i: (i, 0))
    return pl.pallas_call(
        functools.partial(_route_kernel, per_step=per_step),
        grid=(steps,),
        in_specs=[tok],
        out_specs=[pl.BlockSpec((k, SUBLANE, TM), lambda i: (i, 0, 0)), tok,
                   pl.BlockSpec((8, ROUTE_W), lambda i: (0, 0)),
                   pl.BlockSpec(memory_space=pl.ANY)],
        out_shape=[jax.ShapeDtypeStruct((n // TM, SUBLANE, TM), jnp.int32),
                   jax.ShapeDtypeStruct((n, ROUTE_W), F32),
                   jax.ShapeDtypeStruct((8, ROUTE_W), F32),
                   jax.ShapeDtypeStruct((steps * per_step, D_MODEL), F32)],
        scratch_shapes=[pltpu.VMEM((1, ROUTE_W), F32),
                        pltpu.VMEM((ZERO_ROWS, D_MODEL), F32),
                        pltpu.SemaphoreType.DMA(())],
        compiler_params=_params(("arbitrary",)),
        name="route",
    )(lg)


def _dispatch_kernel(pos_ref, h_ref, xs_in_ref, xs_ref, sem):
    del xs_in_ref
    tm = h_ref.shape[0]

    def copy(r, slot):
        return pltpu.make_async_copy(h_ref.at[pl.ds(r, 1)],
                                     xs_ref.at[pl.ds(pos_ref[0, slot, r], 1)], sem)

    def start(r):
        copy(r, 0).start(priority=0)
        copy(r, 1).start(priority=1)

    def wait(r):
        copy(r, 0).wait()
        copy(r, 1).wait()

    for r in range(tm):
        start(r)
    for r in range(tm):
        wait(r)


def _dispatch(pos, h2, xs, *, tile0):
    n = h2.shape[0]
    return pl.pallas_call(
        _dispatch_kernel,
        grid=(n // TM,),
        in_specs=[
            pl.BlockSpec((1, 2, TM), lambda i: (i + tile0, 0, 0), memory_space=pltpu.SMEM),
            pl.BlockSpec((TM, D_MODEL), lambda i: (i, 0)),
            pl.BlockSpec(memory_space=pl.ANY),
        ],
        out_specs=pl.BlockSpec(memory_space=pl.ANY),
        out_shape=jax.ShapeDtypeStruct(xs.shape, xs.dtype),
        scratch_shapes=[pltpu.SemaphoreType.DMA(())],
        input_output_aliases={2: 0},
        compiler_params=_params(("arbitrary",)),
        name="dispatch",
    )(pos, h2, xs)


def _experts_kernel(te_ref, tv_ref, xs_ref, wg_ref, wu_ref, wd_ref, y_ref, wg_scr, wu_scr, wd_scr):
    t = pl.program_id(0)
    new_expert = jnp.logical_or(t == 0, te_ref[t] != te_ref[jnp.maximum(t - 1, 0)])

    @pl.when(new_expert)
    def _():
        wg_scr[...] = wg_ref[0].astype(BF16)
        wu_scr[...] = wu_ref[0].astype(BF16)
        wd_scr[...] = wd_ref[0].astype(BF16)

    @pl.when(tv_ref[t] == 1)
    def _():
        x = xs_ref[...].astype(BF16)
        gt = _dot(x, wg_scr[...])
        up = _dot(x, wu_scr[...])
        hid = (gt * _sigmoid(gt) * up).astype(BF16)
        y_ref[...] = _dot(hid, wd_scr[...])

    @pl.when(tv_ref[t] == 0)
    def _():
        y_ref[...] = jnp.zeros_like(y_ref)


def _experts(te, tv, xs, wg, wu, wd):
    p = te.shape[0] * TMX
    grid_spec = pltpu.PrefetchScalarGridSpec(
        num_scalar_prefetch=2,
        grid=(p // TMX,),
        in_specs=[
            pl.BlockSpec((TMX, D_MODEL), lambda t, te, tv: (t, 0)),
            pl.BlockSpec((1, D_MODEL, D_EXPERT), lambda t, te, tv: (te[t], 0, 0)),
            pl.BlockSpec((1, D_MODEL, D_EXPERT), lambda t, te, tv: (te[t], 0, 0)),
            pl.BlockSpec((1, D_EXPERT, D_MODEL), lambda t, te, tv: (te[t], 0, 0)),
        ],
        out_specs=pl.BlockSpec((TMX, D_MODEL), lambda t, te, tv: (t, 0)),
        scratch_shapes=[pltpu.VMEM((D_MODEL, D_EXPERT), BF16), pltpu.VMEM((D_MODEL, D_EXPERT), BF16),
                        pltpu.VMEM((D_EXPERT, D_MODEL), BF16)],
    )
    return pl.pallas_call(
        _experts_kernel,
        grid_spec=grid_spec,
        out_shape=jax.ShapeDtypeStruct((p, D_MODEL), F32),
        compiler_params=_params(("arbitrary",)),
        name="experts",
    )(te, tv, xs, wg, wu, wd)


def _combine_kernel(pos_ref, x1_ref, rw_ref, fg_ref, ys_ref, o_ref, ya_scr, yb_scr, sem):
    tm = x1_ref.shape[0]

    def copies(r):
        return (pltpu.make_async_copy(ys_ref.at[pl.ds(pos_ref[0, 0, r], 1)], ya_scr.at[pl.ds(r, 1)], sem),
                pltpu.make_async_copy(ys_ref.at[pl.ds(pos_ref[0, 1, r], 1)], yb_scr.at[pl.ds(r, 1)], sem))

    def start(r):
        ca, cb = copies(r)
        ca.start(priority=0)
        cb.start(priority=1)

    def wait(r):
        ca, cb = copies(r)
        ca.wait()
        cb.wait()

    for r in range(tm):
        start(r)
    for r in range(tm):
        wait(r)
    rw = rw_ref[...]
    x2 = x1_ref[...] + rw[:, 0:1] * ya_scr[...] + rw[:, 1:2] * yb_scr[...]
    o_ref[...] = _rms(x2, fg_ref[...])


def _combine(pos, x1, rw, fg, ys, *, tile0):
    n = x1.shape[0]
    tok_m = pl.BlockSpec((TM, D_MODEL), lambda i: (i, 0))
    return pl.pallas_call(
        _combine_kernel,
        grid=(n // TM,),
        in_specs=[
            pl.BlockSpec((1, 2, TM), lambda i: (i + tile0, 0, 0), memory_space=pltpu.SMEM),
            tok_m,
            pl.BlockSpec((TM, ROUTE_W), lambda i: (i + tile0, 0)),
            _const_spec((1, D_MODEL)),
            pl.BlockSpec(memory_space=pl.ANY),
        ],
        out_specs=tok_m,
        out_shape=jax.ShapeDtypeStruct((n, D_MODEL), F32),
        scratch_shapes=[pltpu.VMEM((TM, D_MODEL), F32), pltpu.VMEM((TM, D_MODEL), F32),
                        pltpu.SemaphoreType.DMA(())],
        compiler_params=_params(("arbitrary",)),
        name="combine",
    )(pos, x1, rw, fg, ys)


def _row(v):
    return v.reshape(1, -1).astype(F32)


def _prepare_weights(norm1_g, w_in, b_conv_in, b_gate, mu_shift, w0, w_lora2, a0, a_lora2, g_lora2,
                     k_k, k_a, r_k, lnx_g, lnx_b, w_out_a, conv_w, conv_b, cln_g, cln_b, w_out_b, w_o,
                     norm2_g, w_rg, b_rg, w_re, b_re, w_gate_e, w_up_e, w_down_e, final_g):
    w = {}
    w["n1"] = _row(norm1_g)
    w["wr"] = _permute_cols(w_in[:, :SHIFT_W]).astype(BF16)
    w["wc"] = w_in[:, SHIFT_W:SHIFT_W + 2 * D_B].astype(BF16)
    w["wg"] = w_in[:, SHIFT_W + 2 * D_B:].astype(BF16)
    w["bc"] = _row(b_conv_in)
    w["bg"] = _row(b_gate)
    w["mu"] = _permute_cols(mu_shift).reshape(1, P_W)
    w["vp"] = jnp.concatenate([jnp.stack([w0, a0, k_k, k_a, r_k.reshape(-1)]),
                               jnp.zeros((3, D_A), F32)], axis=0)
    z = jnp.zeros((LORA_W, D_A), F32)
    w2 = jnp.concatenate([jnp.concatenate([w_lora2, z], axis=1),
                          jnp.concatenate([z, a_lora2], axis=1)], axis=0)
    w["w2h"], w["w2l"] = _split(w2)
    g2 = jnp.pad(g_lora2, ((0, 2 * LANE - LORA_G), (0, 0)))
    w["g2h"], w["g2l"] = _split(g2)
    w["lng"] = _row(lnx_g)
    w["lnb"] = _row(lnx_b)
    w["woa"] = w_out_a.astype(BF16)
    w["wob"] = w_out_b.astype(BF16)
    w["cw"] = jnp.pad(conv_w, ((0, HALO - CONV_W), (0, 0))).reshape(HALO, D_B // LANE, LANE).transpose(1, 0, 2)
    w["cb"] = conv_b.reshape(D_B // LANE, 1, LANE)
    w["clg"] = _row(cln_g)
    w["clb"] = _row(cln_b)
    w["wo"] = w_o.astype(BF16)
    w["n2"] = _row(norm2_g)
    pad_r = ROUTE_W - N_GROUPS - N_EXPERTS
    w["wrh"], w["wrl"] = _split(jnp.pad(jnp.concatenate([w_rg, w_re], axis=1), ((0, 0), (0, pad_r))))
    w["brt"] = jnp.pad(jnp.concatenate([b_rg, b_re]), (0, pad_r)).reshape(1, ROUTE_W)
    w["wge"] = w_gate_e
    w["wue"] = w_up_e
    w["wde"] = w_down_e
    w["fg"] = _row(final_g)
    return w


def _trunk(x, shift0, wkv0, conv0, w, *, chunk, nchunk, intra_chunks, conv_tm):
    b, t, _ = x.shape
    n = b * t
    xf = x.reshape(n, D_MODEL)
    shift_p = _permute_cols(shift0)
    tps = max(t // TM, 1)
    nb, step = (1, TM) if t >= TM else (TM // t, t)
    u, tails, rt, at, bt, kt, vv, ge, bonus, g = _inprep(
        xf, w["n1"], w["wr"], w["wc"], w["bc"], shift_p.reshape(-1, nb, P_W), w["mu"], w["vp"],
        w["w2h"], w["w2l"], w["g2h"], w["g2l"], chunk=chunk, nb=nb, step=step, tps=tps)
    rc, yc, m2, n2 = _wkv_intra(rt, at, bt, kt, vv, ge, chunk=chunk, nchunk=intra_chunks)
    y, wkv_new = _wkv_scan(rc, yc, m2, n2, wkv0, n_streams=b, t_len=t, chunk=chunk, nchunk=nchunk)
    hinit = jnp.pad(conv0, ((0, 0), (HALO - (CONV_W - 1), 0), (0, 0)))
    ob = _conv(u, hinit, w["cw"], w["cb"], w["clg"], w["clb"], n_streams=b, t_len=t, tm=conv_tm)
    merged = _merge(xf, y, bonus, g, ob, w["n1"], w["wg"], w["bg"], w["lng"], w["lnb"], w["woa"], w["wob"])
    x1, h2, lg = _outproj(merged, xf, w["wo"], w["n2"], w["wrh"], w["wrl"], w["brt"])
    new_shift = tails.reshape(b, -1, P_W)[:, -1, :SHIFT_W][:, _INV_PERM]
    ext_tail = jnp.concatenate([conv0, u.reshape(b, t, D_B)[:, -(CONV_W - 1):]], axis=1)[:, -(CONV_W - 1):]
    return x1, h2, lg, new_shift, wkv_new, ext_tail


def kernel(x_prompt, x_sample, state_shift, state_wkv, cache_conv, norm1_g, w_in, b_conv_in, b_gate,
           mu_shift, w0, w_lora2, a0, a_lora2, g_lora2, k_k, k_a, r_k, lnx_g, lnx_b, w_out_a, conv_w,
           conv_b, cln_g, cln_b, w_out_b, w_o, norm2_g, w_rg, b_rg, w_re, b_re, w_gate_e, w_up_e,
           w_down_e, final_g):
    per_layer = (norm1_g, w_in, b_conv_in, b_gate, mu_shift, w0, w_lora2, a0, a_lora2, g_lora2, k_k, k_a,
                 r_k, lnx_g, lnx_b, w_out_a, conv_w, conv_b, cln_g, cln_b, w_out_b, w_o, norm2_g, w_rg,
                 b_rg, w_re, b_re, w_gate_e, w_up_e, w_down_e)
    assert norm1_g.shape[0] == 1, "single layer only"
    w = _prepare_weights(*[p[0] for p in per_layer], final_g)
    bp, tp, _ = x_prompt.shape
    bs, ts, _ = x_sample.shape
    zero_shift = jnp.zeros((bp, SHIFT_W), F32)
    zero_wkv = jnp.zeros((bp, N_HEADS, HEAD, HEAD), F32)
    zero_conv = jnp.zeros((bp, CONV_W - 1, D_B), F32)
    x1p, h2p, lgp, shift_p, wkv_p, conv_p = _trunk(
        x_prompt, zero_shift, zero_wkv, zero_conv, w, chunk=64, nchunk=min(8, tp // 64),
        intra_chunks=min(16, tp // 64), conv_tm=TM)
    x1s, h2s, lgs, shift_s, wkv_s, conv_s = _trunk(
        x_sample, state_shift[0], state_wkv[0], cache_conv[0], w, chunk=ts, nchunk=1,
        intra_chunks=TM // ts, conv_tm=ts)

    n_p, n_s = bp * tp, bs * ts
    n_all = n_p + n_s
    n_tiles = (2 * n_all + N_EXPERTS * (TMX - 1) + TMX - 1) // TMX
    ri, rw, cnt, xs = _route(jnp.concatenate([lgp, lgs], axis=0), n_tiles * TMX)
    counts = cnt[0, :N_EXPERTS].astype(jnp.int32)
    tiles_per = (counts + TMX - 1) // TMX
    tile_end = jnp.cumsum(tiles_per)
    offs = (tile_end - tiles_per) * TMX
    t_idx = jnp.arange(n_tiles, dtype=jnp.int32)
    te = jnp.sum(t_idx[:, None] >= tile_end[None, :], axis=1).astype(jnp.int32)
    tv = (t_idx < tile_end[-1]).astype(jnp.int32)
    e_ids = jnp.arange(N_EXPERTS, dtype=jnp.int32)
    last_e = jnp.max(jnp.where(tiles_per > 0, e_ids, 0))
    te = jnp.where(tv == 1, jnp.minimum(te, N_EXPERTS - 1), last_e)
    base = jnp.sum(jnp.where(ri[:, 0:2, :, None] == e_ids, offs, 0), axis=-1)
    pos = base + ri[:, 2:4, :]
    xs = _dispatch(pos, h2p, xs, tile0=0)
    xs = _dispatch(pos, h2s, xs, tile0=n_p // TM)
    ys = _experts(te, tv, xs, w["wge"], w["wue"], w["wde"])
    yp = _combine(pos, x1p, rw, w["fg"], ys, tile0=0)
    ysm = _combine(pos, x1s, rw, w["fg"], ys, tile0=n_p // TM)
    return (yp.reshape(bp, tp, D_MODEL), ysm.reshape(bs, ts, D_MODEL),
            shift_p[None], wkv_p[None], conv_p[None],
            shift_s[None], wkv_s[None], conv_s[None])
```

```python
import functools
import math

import numpy as np
import jax
import jax.numpy as jnp
from jax import lax
from jax.experimental import pallas as pl
from jax.experimental.pallas import tpu as pltpu

F32 = jnp.float32
BF16 = jnp.bfloat16

D_MODEL = 2048
D_A = 1024
HEAD = 64
N_HEADS = D_A // HEAD
LORA_W = 64
LORA_A = 64
LORA_G = 160
SHIFT_W = 3 * D_A + LORA_W + LORA_A + LORA_G
D_B = 1024
CONV_W = 31
N_GROUPS = 4
EXP_PER_GROUP = 8
N_EXPERTS = N_GROUPS * EXP_PER_GROUP
D_EXPERT = 512
RMS_EPS = 1e-6
LN_EPS = 1e-5
GN_EPS = 64e-5

LANE = 128
SUBLANE = 8
P_W = 27 * LANE
LORA_IN0 = 3 * D_A
XG0 = LORA_IN0 + LANE
HALO = 32
ROUTE_W = LANE
TM = 256
TM_WIDE = 512
TMX = 256
ZERO_ROWS = 256
VMEM_LIMIT = 56 * 1024 * 1024

_PERM_RANGES = ((0, D_A), (D_A + LORA_W, 2 * D_A + LORA_W), (2 * D_A + LORA_W, 3 * D_A + LORA_W),
                (D_A, D_A + LORA_W), (3 * D_A + LORA_W, SHIFT_W))
_PERM = np.concatenate([np.arange(a, b) for a, b in _PERM_RANGES])
_INV_PERM = np.argsort(_PERM)
assert len(_PERM) == SHIFT_W and (np.sort(_PERM) == np.arange(SHIFT_W)).all()


def _permute_cols(x):
    pad = jnp.zeros(x.shape[:-1] + (P_W - SHIFT_W,), x.dtype)
    return jnp.concatenate([x[..., a:b] for a, b in _PERM_RANGES] + [pad], axis=-1)


def _dot(a, b):
    return jnp.dot(a, b, preferred_element_type=F32)


def _dot_nt(a, b):
    return lax.dot_general(a, b, (((1,), (1,)), ((), ())), preferred_element_type=F32)


def _dot_tn(a, b):
    return lax.dot_general(a, b, (((0,), (0,)), ((), ())), preferred_element_type=F32)


def _split(x):
    hi = x.astype(BF16)
    lo = (x - hi.astype(F32)).astype(BF16)
    return hi, lo


def _dot3(x, w_hi, w_lo):
    x_hi, x_lo = _split(x)
    return _dot(x_hi, w_hi) + (_dot(x_lo, w_hi) + _dot(x_hi, w_lo))


def _sigmoid(x):
    return 0.5 * jnp.tanh(0.5 * x) + 0.5


def _rms(x, g):
    return x * lax.rsqrt(jnp.mean(x * x, axis=-1, keepdims=True) + RMS_EPS) * g


def _head_sum(x):
    w = 2 * LANE
    r = lax.broadcasted_iota(jnp.int32, (w, w), 0) // HEAD
    c = lax.broadcasted_iota(jnp.int32, (w, w), 1) // HEAD
    ones = jnp.where(r == c, 1.0, 0.0).astype(BF16)
    outs = [_dot(x[:, j * w:(j + 1) * w].astype(BF16), ones) for j in range(x.shape[1] // w)]
    return jnp.concatenate(outs, axis=1)


def _const_spec(shape):
    nd = len(shape)
    return pl.BlockSpec(shape, lambda *_: (0,) * nd, pipeline_mode=pl.Buffered(1))


def _params(sem):
    return pltpu.CompilerParams(dimension_semantics=sem, vmem_limit_bytes=VMEM_LIMIT)


def _inprep_kernel(x_ref, n1_ref, wr_ref, wc_ref, bc_ref, bnd_ref, mu_ref, vp_ref, w2h_ref, w2l_ref,
                   g2h_ref, g2l_ref,
                   u_ref, tail_ref, rt_ref, at_ref, bt_ref, kt_ref, v_ref, ge_ref, bonus_ref, g_ref,
                   first_scr, last_scr, *, chunk, nb, step, tps):
    tm = x_ref.shape[0]
    if nb == 1:
        starts_stream = lax.rem(pl.program_id(0), tps) == 0

        @pl.when(starts_stream)
        def _():
            first_scr[...] = bnd_ref[0, 0:1, :]

        @pl.when(jnp.logical_not(starts_stream))
        def _():
            first_scr[...] = last_scr[...]

    h = _rms(x_ref[...], n1_ref[...]).astype(BF16)
    p = _dot(h, wr_ref[...])
    for j in range(nb):
        tail_ref[0, j:j + 1, :] = p[(j + 1) * step - 1:(j + 1) * step, :]
    c = _dot(h, wc_ref[...]) + bc_ref[...]
    u_ref[...] = c[:, :D_B] * _sigmoid(c[:, D_B:])

    row = lax.broadcasted_iota(jnp.int32, (tm, 1), 0)
    prev = pltpu.roll(p, 1, axis=0)
    if nb == 1:
        prev = jnp.where(row == 0, first_scr[...], prev)
        last_scr[...] = p[tm - 1:tm, :]
    else:
        for j in range(nb):
            prev = jnp.where(row == j * step, bnd_ref[0, j:j + 1, :], prev)
    xs = p + (prev - p) * mu_ref[...]
    r = xs[:, 0:D_A]
    k = xs[:, D_A:2 * D_A]
    v = xs[:, 2 * D_A:3 * D_A]
    lin = xs[:, LORA_IN0:LORA_IN0 + LANE]
    lane = lax.broadcasted_iota(jnp.int32, (1, LANE), 1)
    lin = jnp.where(lane < LORA_W, jnp.tanh(lin), lin)
    lo = _dot3(lin, w2h_ref[...], w2l_ref[...])
    w0 = vp_ref[0:1, :]
    a0 = vp_ref[1:2, :]
    k_k = vp_ref[2:3, :]
    k_a = vp_ref[3:4, :]
    r_k = vp_ref[4:5, :]
    logw = -math.exp(-0.5) * _sigmoid(w0 + lo[:, :D_A])
    a = _sigmoid(a0 + lo[:, D_A:])
    g = _dot3(_sigmoid(xs[:, XG0:XG0 + 2 * LANE]), g2h_ref[...], g2l_ref[...])
    kku = k * k_k
    kk = kku / jnp.maximum(jnp.sqrt(_head_sum(kku * kku)), 1e-12)
    k2 = k * (1.0 + (a - 1.0) * k_a)
    bonus_ref[...] = (_head_sum(r * k2 * r_k) * v).astype(BF16)
    g_ref[...] = g.astype(BF16)
    ri = lax.broadcasted_iota(jnp.int32, (tm, tm), 0)
    ci = lax.broadcasted_iota(jnp.int32, (tm, tm), 1)
    tri = jnp.where((ri // chunk == ci // chunk) & (ci <= ri), 1.0, 0.0).astype(BF16)
    lw_hi, lw_lo = _split(logw)
    cum = _dot(tri, lw_hi) + _dot(tri, lw_lo)
    eg = jnp.exp(cum)
    egi = jnp.exp(-cum)
    rt_ref[...] = (r * eg).astype(BF16)
    at_ref[...] = (-kk * jnp.exp(cum - logw)).astype(BF16)
    bt_ref[...] = (kk * a * egi).astype(BF16)
    kt_ref[...] = (k2 * egi).astype(BF16)
    v_ref[...] = v.astype(BF16)
    for cc in range(tm // chunk):
        ge_ref[cc] = eg[(cc + 1) * chunk - 1:(cc + 1) * chunk, :]


def _inprep(x, n1, wr, wc, bc, bnd, mu, vp, w2h, w2l, g2h, g2l, *, chunk, nb, step, tps):
    n = x.shape[0]
    tok = pl.BlockSpec((TM, D_A), lambda i: (i, 0))
    return pl.pallas_call(
        functools.partial(_inprep_kernel, chunk=chunk, nb=nb, step=step, tps=tps),
        grid=(n // TM,),
        in_specs=[
            pl.BlockSpec((TM, D_MODEL), lambda i: (i, 0)),
            _const_spec((1, D_MODEL)),
            _const_spec((D_MODEL, P_W)),
            _const_spec((D_MODEL, 2 * D_B)),
            _const_spec((1, 2 * D_B)),
            pl.BlockSpec((1, nb, P_W), lambda i: (i // tps, 0, 0)),
            _const_spec((1, P_W)),
            _const_spec((8, D_A)),
            _const_spec((LANE, 2 * D_A)),
            _const_spec((LANE, 2 * D_A)),
            _const_spec((2 * LANE, D_A)),
            _const_spec((2 * LANE, D_A)),
        ],
        out_specs=[tok,
                   pl.BlockSpec((1, nb, P_W), lambda i: (i, 0, 0)),
                   tok, tok, tok, tok, tok,
                   pl.BlockSpec((TM // chunk, 1, D_A), lambda i: (i, 0, 0)),
                   tok, tok],
        out_shape=[jax.ShapeDtypeStruct((n, D_B), F32),
                   jax.ShapeDtypeStruct((n // TM, nb, P_W), F32)]
        + [jax.ShapeDtypeStruct((n, D_A), BF16)] * 5
        + [jax.ShapeDtypeStruct((n // chunk, 1, D_A), F32),
           jax.ShapeDtypeStruct((n, D_A), BF16),
           jax.ShapeDtypeStruct((n, D_A), BF16)],
        scratch_shapes=[pltpu.VMEM((1, P_W), F32), pltpu.VMEM((1, P_W), F32)],
        compiler_params=_params(("arbitrary",)),
        name="inprep",
    )(x, n1, wr, wc, bc, bnd, mu, vp, w2h, w2l, g2h, g2l)


def _wkv_intra_kernel(rt_ref, at_ref, bt_ref, kt_ref, v_ref, ge_ref, rc_ref, yc_ref, m2_ref, n2_ref,
                      *, chunk, nchunk):
    c = chunk
    w2 = 2 * c
    row = lax.broadcasted_iota(jnp.int32, (c, w2), 0)
    col = lax.broadcasted_iota(jnp.int32, (c, w2), 1)
    colc = jnp.where(col >= c, col - c, col)
    strict = colc < row
    incl = colc <= row
    eye_c = jnp.where(colc == row, 1.0, 0.0).astype(F32)
    tok0 = lax.broadcasted_iota(jnp.int32, (1, w2), 1) < c
    ch0 = lax.broadcasted_iota(jnp.int32, (1, LANE), 1) < HEAD
    rk = lax.broadcasted_iota(jnp.int32, (HEAD, LANE), 0)
    ck = lax.broadcasted_iota(jnp.int32, (HEAD, LANE), 1)
    eye_k = jnp.where(jnp.where(ck >= HEAD, ck - HEAD, ck) == rk, 1.0, 0.0).astype(F32)
    n_sq = int(math.log2(c)) - 1

    def bd(x, first):
        zero = jnp.zeros_like(x)
        return jnp.concatenate([jnp.where(first, x, zero), jnp.where(first, zero, x)], axis=0)

    def head_blocks(x):
        return [jnp.where(ch0, x[:HEAD, j * LANE:(j + 1) * LANE], x[HEAD:, j * LANE:(j + 1) * LANE])
                for j in range(x.shape[1] // LANE)]

    rng = range(nchunk)
    rows = [slice(ci * c, (ci + 1) * c) for ci in rng]
    at = [at_ref[r, :] for r in rows]
    rt = [rt_ref[r, :] for r in rows]
    bt = [bt_ref[r, :] for r in rows]
    kt = [kt_ref[r, :] for r in rows]
    vv = [v_ref[r, :] for r in rows]
    m = [_dot_nt(jnp.concatenate([at[i], rt[i]], axis=0),
                 jnp.concatenate([bd(bt[i], ch0), bd(kt[i], ch0)], axis=0)) for i in rng]
    a_ab = [jnp.where(strict, x[:c, :w2], 0.0) for x in m]
    a_rb = [jnp.where(incl, x[c:, :w2], 0.0).astype(BF16) for x in m]
    akv = [_dot(jnp.concatenate([jnp.where(strict, x[:c, w2:], 0.0), jnp.where(incl, x[c:, w2:], 0.0)],
                                axis=0).astype(BF16), bd(vv[i], ch0)) for i, x in enumerate(m)]
    kv = [head_blocks(_dot_tn(kt[i], vv[i]))[0] for i in rng]
    t = [eye_c + x for x in a_ab]
    pw = [x.astype(BF16) for x in a_ab]
    pw = [_dot(x, bd(x, tok0)).astype(BF16) for x in pw]
    for _ in range(n_sq - 1):
        st = [_dot(jnp.concatenate([t[i].astype(BF16), pw[i]], axis=0), bd(pw[i], tok0)) for i in rng]
        t = [t[i] + st[i][:c] for i in rng]
        pw = [x[c:].astype(BF16) for x in st]
    t = [t[i] + _dot(t[i].astype(BF16), bd(pw[i], tok0)) for i in rng]
    twb = [_dot(t[i].astype(BF16),
                jnp.concatenate([bd(at[i], ch0), bd(akv[i][:c].astype(BF16), ch0)], axis=1)).astype(BF16)
           for i in rng]
    rb = [_dot(a_rb[i], jnp.concatenate([bd(twb[i][:, :LANE], ch0), bd(twb[i][:, LANE:], ch0)], axis=1))
          for i in rng]
    xb = [head_blocks(_dot_tn(bt[i], twb[i])) for i in rng]
    for i in rng:
        rc_ref[rows[i], :] = (rt[i].astype(F32) + rb[i][:, :LANE]).astype(BF16)
        yc_ref[rows[i], :] = rb[i][:, LANE:] + akv[i][c:]
        g_full = jnp.broadcast_to(ge_ref[i], (LANE, LANE)).T
        g_col = jnp.where(ch0, g_full[:HEAD], g_full[HEAD:])
        m2_ref[i] = (g_col * (eye_k + xb[i][0]) - eye_k).astype(BF16)
        n2_ref[i] = g_col * (xb[i][1] + kv[i])


def _wkv_intra(rt, at, bt, kt, v, ge, *, chunk, nchunk):
    n = rt.shape[0]
    rows = chunk * nchunk
    tok = pl.BlockSpec((rows, LANE), lambda i, hp: (i, hp))
    mat = pl.BlockSpec((nchunk, HEAD, LANE), lambda i, hp: (i, 0, hp))
    return pl.pallas_call(
        functools.partial(_wkv_intra_kernel, chunk=chunk, nchunk=nchunk),
        grid=(n // rows, N_HEADS // 2),
        in_specs=[tok, tok, tok, tok, tok,
                  pl.BlockSpec((nchunk, 1, LANE), lambda i, hp: (i, 0, hp))],
        out_specs=[tok, tok, mat, mat],
        out_shape=[jax.ShapeDtypeStruct((n, D_A), BF16),
                   jax.ShapeDtypeStruct((n, D_A), F32),
                   jax.ShapeDtypeStruct((n // chunk, HEAD, D_A), BF16),
                   jax.ShapeDtypeStruct((n // chunk, HEAD, D_A), F32)],
        compiler_params=_params(("parallel", "parallel")),
        name="wkv_intra",
    )(rt, at, bt, kt, v, ge)


def _wkv_scan_kernel(rc_ref, yc_ref, m2_ref, n2_ref, s0_ref, y_ref, st_ref, h_scr, *, chunk, nchunk):
    j = pl.program_id(1)

    @pl.when(j == 0)
    def _():
        for h in range(N_HEADS):
            h_scr[:, h * HEAD:(h + 1) * HEAD] = s0_ref[0, h].T

    c = chunk

    def chunk_body(ci, carry):
        off = pl.multiple_of(ci * c, c)
        h_all = h_scr[...]
        hb = h_all.astype(BF16)
        rc = rc_ref[pl.ds(off, c), :]
        m2 = m2_ref[ci]
        ch0 = lax.broadcasted_iota(jnp.int32, (1, LANE), 1) < HEAD
        zero = jnp.zeros((HEAD, LANE), BF16)
        ys, mh = [], []
        for hp in range(N_HEADS // 2):
            sl = slice(hp * LANE, (hp + 1) * LANE)
            hpair = hb[:, sl]
            hbd = jnp.concatenate([jnp.where(ch0, hpair, zero), jnp.where(ch0, zero, hpair)], axis=0)
            both = _dot(jnp.concatenate([rc[:, sl], m2[:, sl]], axis=0), hbd)
            ys.append(both[:c])
            mh.append(both[c:])
        y_ref[pl.ds(off, c), :] = (jnp.concatenate(ys, axis=1) + yc_ref[pl.ds(off, c), :]).astype(BF16)
        h_scr[...] = h_all + jnp.concatenate(mh, axis=1) + n2_ref[ci]
        return carry

    lax.fori_loop(0, nchunk, chunk_body, 0)

    @pl.when(j == pl.num_programs(1) - 1)
    def _():
        for h in range(N_HEADS):
            st_ref[0, h] = h_scr[:, h * HEAD:(h + 1) * HEAD].T


def _wkv_scan(rc, yc, m2, n2, s0, *, n_streams, t_len, chunk, nchunk):
    n = rc.shape[0]
    rows = chunk * nchunk
    nj = t_len // rows
    tok = pl.BlockSpec((rows, D_A), lambda b, j: (b * nj + j, 0))
    mat = pl.BlockSpec((nchunk, HEAD, D_A), lambda b, j: (b * nj + j, 0, 0))
    st = pl.BlockSpec((1, N_HEADS, HEAD, HEAD), lambda b, j: (b, 0, 0, 0))
    return pl.pallas_call(
        functools.partial(_wkv_scan_kernel, chunk=chunk, nchunk=nchunk),
        grid=(n_streams, nj),
        in_specs=[tok, tok, mat, mat, st],
        out_specs=[tok, st],
        out_shape=[jax.ShapeDtypeStruct((n, D_A), BF16),
                   jax.ShapeDtypeStruct((n_streams, N_HEADS, HEAD, HEAD), F32)],
        scratch_shapes=[pltpu.VMEM((HEAD, D_A), F32)],
        compiler_params=_params(("parallel", "arbitrary")),
        name="wkv_scan",
    )(rc, yc, m2, n2, s0)


CONV_ROWS = 128


def _conv_kernel(u_ref, hown_ref, hinit_ref, cw_ref, cb_ref, lg_ref, lb_ref, o_ref, ext_scr, sh_scr, z_scr):
    tm = u_ref.shape[0]
    n_lt = D_B // LANE
    first = pl.program_id(1) == 0

    @pl.when(first)
    def _():
        for l in range(n_lt):
            ext_scr[l, 0:HALO, :] = hinit_ref[0, :, l * LANE:(l + 1) * LANE]

    @pl.when(jnp.logical_not(first))
    def _():
        for l in range(n_lt):
            ext_scr[l, 0:HALO, :] = hown_ref[:, l * LANE:(l + 1) * LANE]

    for l in range(n_lt):
        ext_scr[l, HALO:HALO + tm, :] = u_ref[:, l * LANE:(l + 1) * LANE]
    span = sh_scr.shape[1]
    lead = HALO - (CONV_W - 1)

    def lane_tile(l, carry):
        for s in range(1, SUBLANE):
            sh_scr[s - 1] = ext_scr[l, s:s + span, :]
        acc = jnp.broadcast_to(cb_ref[l], (tm, LANE))
        for j in range(CONV_W):
            s, q = (lead + j) % SUBLANE, (lead + j) // SUBLANE
            r0 = q * SUBLANE
            win = ext_scr[l, r0:r0 + tm, :] if s == 0 else sh_scr[s - 1, r0:r0 + tm, :]
            acc = acc + cw_ref[l, j:j + 1, :] * win
        z_scr[l] = acc
        return carry

    lax.fori_loop(0, n_lt, lane_tile, 0)
    rows = min(CONV_ROWS, tm)

    def norm(i, carry):
        r0 = pl.multiple_of(i * rows, rows)
        z = jnp.concatenate([z_scr[l, pl.ds(r0, rows), :] for l in range(n_lt)], axis=1)
        mu = jnp.mean(z, axis=-1, keepdims=True)
        d = z - mu
        var = jnp.mean(d * d, axis=-1, keepdims=True)
        zn = d * lax.rsqrt(var + LN_EPS) * lg_ref[...] + lb_ref[...]
        o_ref[pl.ds(r0, rows), :] = (zn * _sigmoid(zn)).astype(BF16)
        return carry

    lax.fori_loop(0, tm // rows, norm, 0)


def _conv(u, hinit, cw, cb, lg, lb, *, n_streams, t_len, tm):
    n = u.shape[0]
    tps = t_len // tm
    if tps > 1:
        blocks_per_tile = tm // HALO
        own_map = lambda b, i: (jnp.maximum((b * tps + i) * blocks_per_tile - 1, 0), 0)
    else:
        own_map = lambda b, i: (0, 0)
    return pl.pallas_call(
        _conv_kernel,
        grid=(n_streams, tps),
        in_specs=[
            pl.BlockSpec((tm, D_B), lambda b, i: (b * tps + i, 0)),
            pl.BlockSpec((HALO, D_B), own_map),
            pl.BlockSpec((1, HALO, D_B), lambda b, i: (b, 0, 0)),
            _const_spec((D_B // LANE, HALO, LANE)),
            _const_spec((D_B // LANE, 1, LANE)),
            _const_spec((1, D_B)),
            _const_spec((1, D_B)),
        ],
        out_specs=pl.BlockSpec((tm, D_B), lambda b, i: (b * tps + i, 0)),
        out_shape=jax.ShapeDtypeStruct((n, D_B), BF16),
        scratch_shapes=[pltpu.VMEM((D_B // LANE, HALO + tm, LANE), F32),
                        pltpu.VMEM((SUBLANE - 1, HALO + tm - SUBLANE, LANE), F32),
                        pltpu.VMEM((D_B // LANE, tm, LANE), F32)],
        compiler_params=_params(("parallel", "arbitrary")),
        name="conv",
    )(u, u, hinit, cw, cb, lg, lb)


def _merge_kernel(x_ref, y_ref, bonus_ref, g_ref, ob_ref, n1_ref, wg_ref, bg_ref,
                  lng_ref, lnb_ref, woa_ref, wob_ref, o_ref):
    h = _rms(x_ref[...], n1_ref[...]).astype(BF16)
    y = y_ref[...].astype(F32)
    d = y - _head_sum(y) * (1.0 / HEAD)
    var = _head_sum(d * d) * (1.0 / HEAD)
    yn = d * lax.rsqrt(var + GN_EPS) * lng_ref[...] + lnb_ref[...]
    oa = ((yn + bonus_ref[...].astype(F32)) * g_ref[...].astype(F32)).astype(BF16)
    for j in range(2):
        cols = slice(j * D_A, (j + 1) * D_A)
        gcols = slice(D_MODEL + j * D_A, D_MODEL + (j + 1) * D_A)
        ga = _sigmoid(_dot(h, wg_ref[:, cols]) + bg_ref[:, cols])
        gb = _sigmoid(_dot(h, wg_ref[:, gcols]) + bg_ref[:, gcols])
        o_ref[:, cols] = (ga * _dot(oa, woa_ref[:, cols])
                          + gb * _dot(ob_ref[...], wob_ref[:, cols])).astype(BF16)


def _merge(x, y, bonus, g, ob, n1, wg, bg, lng, lnb, woa, wob):
    n = x.shape[0]
    tm = min(TM_WIDE, n)
    tok_a = pl.BlockSpec((tm, D_A), lambda i: (i, 0))
    tok_m = pl.BlockSpec((tm, D_MODEL), lambda i: (i, 0))
    return pl.pallas_call(
        _merge_kernel,
        grid=(n // tm,),
        in_specs=[tok_m, tok_a, tok_a, tok_a, tok_a,
                  _const_spec((1, D_MODEL)),
                  _const_spec((D_MODEL, 2 * D_MODEL)),
                  _const_spec((1, 2 * D_MODEL)),
                  _const_spec((1, D_A)),
                  _const_spec((1, D_A)),
                  _const_spec((D_A, D_MODEL)),
                  _const_spec((D_B, D_MODEL))],
        out_specs=tok_m,
        out_shape=jax.ShapeDtypeStruct((n, D_MODEL), BF16),
        compiler_params=_params(("parallel",)),
        name="merge",
    )(x, y, bonus, g, ob, n1, wg, bg, lng, lnb, woa, wob)


def _outproj_kernel(m_ref, x_ref, wo_ref, n2_ref, wrh_ref, wrl_ref, brt_ref, x1_ref, h2_ref, lg_ref):
    x1 = x_ref[...] + _dot(m_ref[...], wo_ref[...])
    x1_ref[...] = x1
    h2 = _rms(x1, n2_ref[...])
    h2_ref[...] = h2
    lg_ref[...] = _dot3(h2, wrh_ref[...], wrl_ref[...]) + brt_ref[...]


def _outproj(m, x, wo, n2, wrh, wrl, brt):
    n = x.shape[0]
    tm = min(TM_WIDE, n)
    tok_m = pl.BlockSpec((tm, D_MODEL), lambda i: (i, 0))
    return pl.pallas_call(
        _outproj_kernel,
        grid=(n // tm,),
        in_specs=[tok_m, tok_m,
                  _const_spec((D_MODEL, D_MODEL)),
                  _const_spec((1, D_MODEL)),
                  _const_spec((D_MODEL, ROUTE_W)),
                  _const_spec((D_MODEL, ROUTE_W)),
                  _const_spec((1, ROUTE_W))],
        out_specs=[tok_m, tok_m,
                   pl.BlockSpec((tm, ROUTE_W), lambda i: (i, 0))],
        out_shape=[jax.ShapeDtypeStruct((n, D_MODEL), F32),
                   jax.ShapeDtypeStruct((n, D_MODEL), F32),
                   jax.ShapeDtypeStruct((n, ROUTE_W), F32)],
        compiler_params=_params(("parallel",)),
        name="outproj",
    )(m, x, wo, n2, wrh, wrl, brt)


def _route_kernel(lg_ref, ri_ref, rw_ref, cnt_ref, xs_ref, carry_scr, zero_scr, sem, *, per_step):
    i = pl.program_id(0)
    tm = lg_ref.shape[0]

    @pl.when(i == 0)
    def _():
        carry_scr[...] = jnp.zeros_like(carry_scr)
        zero_scr[...] = jnp.zeros_like(zero_scr)

    zr = zero_scr.shape[0]

    def zero_copy(q):
        start = pl.multiple_of(i * per_step + q * zr, zr)
        return pltpu.make_async_copy(zero_scr, xs_ref.at[pl.ds(start, zr)], sem)

    for q in range(per_step // zr):
        zero_copy(q).start()

    lg = lg_ref[...]
    lane = lax.broadcasted_iota(jnp.int32, (tm, ROUTE_W), 1)
    neg = jnp.float32(-jnp.inf)
    big = jnp.int32(ROUTE_W)
    grp = jnp.where(lane < N_GROUPS, lg, neg)
    gmax = jnp.max(grp, axis=1, keepdims=True)
    g_sel = jnp.min(jnp.where(grp == gmax, lane, big), axis=1, keepdims=True)
    p_grp = 1.0 / jnp.sum(jnp.where(lane < N_GROUPS, jnp.exp(lg - gmax), 0.0), axis=1, keepdims=True)
    lo = N_GROUPS + g_sel * EXP_PER_GROUP
    le = jnp.where((lane >= lo) & (lane < lo + EXP_PER_GROUP), lg, neg)
    m1 = jnp.max(le, axis=1, keepdims=True)
    i1 = jnp.min(jnp.where(le == m1, lane, big), axis=1, keepdims=True)
    le2 = jnp.where(lane == i1, neg, le)
    m2 = jnp.max(le2, axis=1, keepdims=True)
    i2 = jnp.min(jnp.where(le2 == m2, lane, big), axis=1, keepdims=True)
    e2 = jnp.exp(m2 - m1)
    wa = p_grp / (1.0 + e2)
    wb = p_grp * e2 / (1.0 + e2)
    ea = i1 - N_GROUPS
    eb = i2 - N_GROUPS
    onehot = jnp.where((lane == ea) | (lane == eb), 1.0, 0.0)
    ri_ = lax.broadcasted_iota(jnp.int32, (tm, tm), 0)
    ci_ = lax.broadcasted_iota(jnp.int32, (tm, tm), 1)
    below = jnp.where(ci_ < ri_, 1.0, 0.0).astype(BF16)
    before = _dot(below, onehot.astype(BF16)) + carry_scr[...]
    rank_a = jnp.sum(jnp.where(lane == ea, before, 0.0), axis=1, keepdims=True).astype(jnp.int32)
    rank_b = jnp.sum(jnp.where(lane == eb, before, 0.0), axis=1, keepdims=True).astype(jnp.int32)
    carry_scr[...] = carry_scr[...] + jnp.sum(onehot, axis=0, keepdims=True)
    ri = jnp.where(lane == 0, ea, jnp.where(lane == 1, eb,
                   jnp.where(lane == 2, rank_a, jnp.where(lane == 3, rank_b, 0))))
    ri_t = ri.T
    for q in range(tm // TM):
        ri_ref[q] = ri_t[0:SUBLANE, q * TM:(q + 1) * TM]
    rw_ref[...] = jnp.where(lane == 0, wa, jnp.where(lane == 1, wb, 0.0))
    cnt_ref[...] = jnp.broadcast_to(carry_scr[...], cnt_ref.shape)
    for q in range(per_step // zr):
        zero_copy(q).wait()


def _route(lg, sorted_rows):
    n = lg.shape[0]
    k = next(c for c in (4, 3, 2, 1) if (n // TM) % c == 0)
    steps = n // (k * TM)
    per_step = -(-sorted_rows // (steps * ZERO_ROWS)) * ZERO_ROWS
    tok = pl.BlockSpec((k * TM, ROUTE_W), lambda i: (i, 0))
    return pl.pallas_call(
        functools.partial(_route_kernel, per_step=per_step),
        grid=(steps,),
        in_specs=[tok],
        out_specs=[pl.BlockSpec((k, SUBLANE, TM), lambda i: (i, 0, 0)), tok,
                   pl.BlockSpec((8, ROUTE_W), lambda i: (0, 0)),
                   pl.BlockSpec(memory_space=pl.ANY)],
        out_shape=[jax.ShapeDtypeStruct((n // TM, SUBLANE, TM), jnp.int32),
                   jax.ShapeDtypeStruct((n, ROUTE_W), F32),
                   jax.ShapeDtypeStruct((8, ROUTE_W), F32),
                   jax.ShapeDtypeStruct((steps * per_step, D_MODEL), F32)],
        scratch_shapes=[pltpu.VMEM((1, ROUTE_W), F32),
                        pltpu.VMEM((ZERO_ROWS, D_MODEL), F32),
                        pltpu.SemaphoreType.DMA(())],
        compiler_params=_params(("arbitrary",)),
        name="route",
    )(lg)


def _dispatch_kernel(pos_ref, h_ref, xs_in_ref, xs_ref, sem):
    del xs_in_ref
    tm = h_ref.shape[0]

    def copy(r, slot):
        return pltpu.make_async_copy(h_ref.at[pl.ds(r, 1)],
                                     xs_ref.at[pl.ds(pos_ref[0, slot, r], 1)], sem)

    def start(r):
        copy(r, 0).start(priority=0)
        copy(r, 1).start(priority=1)

    def wait(r):
        copy(r, 0).wait()
        copy(r, 1).wait()

    for r in range(tm):
        start(r)
    for r in range(tm):
        wait(r)


def _dispatch(pos, h2, xs, *, tile0):
    n = h2.shape[0]
    return pl.pallas_call(
        _dispatch_kernel,
        grid=(n // TM,),
        in_specs=[
            pl.BlockSpec((1, 2, TM), lambda i: (i + tile0, 0, 0), memory_space=pltpu.SMEM),
            pl.BlockSpec((TM, D_MODEL), lambda i: (i, 0)),
            pl.BlockSpec(memory_space=pl.ANY),
        ],
        out_specs=pl.BlockSpec(memory_space=pl.ANY),
        out_shape=jax.ShapeDtypeStruct(xs.shape, xs.dtype),
        scratch_shapes=[pltpu.SemaphoreType.DMA(())],
        input_output_aliases={2: 0},
        compiler_params=_params(("arbitrary",)),
        name="dispatch",
    )(pos, h2, xs)


def _experts_kernel(te_ref, tv_ref, xs_ref, wg_ref, wu_ref, wd_ref, y_ref, wg_scr, wu_scr, wd_scr):
    t = pl.program_id(0)
    new_expert = jnp.logical_or(t == 0, te_ref[t] != te_ref[jnp.maximum(t - 1, 0)])

    @pl.when(new_expert)
    def _():
        wg_scr[...] = wg_ref[0].astype(BF16)
        wu_scr[...] = wu_ref[0].astype(BF16)
        wd_scr[...] = wd_ref[0].astype(BF16)

    @pl.when(tv_ref[t] == 1)
    def _():
        x = xs_ref[...].astype(BF16)
        gt = _dot(x, wg_scr[...])
        up = _dot(x, wu_scr[...])
        hid = (gt * _sigmoid(gt) * up).astype(BF16)
        y_ref[...] = _dot(hid, wd_scr[...])

    @pl.when(tv_ref[t] == 0)
    def _():
        y_ref[...] = jnp.zeros_like(y_ref)


def _experts(te, tv, xs, wg, wu, wd):
    p = te.shape[0] * TMX
    grid_spec = pltpu.PrefetchScalarGridSpec(
        num_scalar_prefetch=2,
        grid=(p // TMX,),
        in_specs=[
            pl.BlockSpec((TMX, D_MODEL), lambda t, te, tv: (t, 0)),
            pl.BlockSpec((1, D_MODEL, D_EXPERT), lambda t, te, tv: (te[t], 0, 0)),
            pl.BlockSpec((1, D_MODEL, D_EXPERT), lambda t, te, tv: (te[t], 0, 0)),
            pl.BlockSpec((1, D_EXPERT, D_MODEL), lambda t, te, tv: (te[t], 0, 0)),
        ],
        out_specs=pl.BlockSpec((TMX, D_MODEL), lambda t, te, tv: (t, 0)),
        scratch_shapes=[pltpu.VMEM((D_MODEL, D_EXPERT), BF16), pltpu.VMEM((D_MODEL, D_EXPERT), BF16),
                        pltpu.VMEM((D_EXPERT, D_MODEL), BF16)],
    )
    return pl.pallas_call(
        _experts_kernel,
        grid_spec=grid_spec,
        out_shape=jax.ShapeDtypeStruct((p, D_MODEL), F32),
        compiler_params=_params(("arbitrary",)),
        name="experts",
    )(te, tv, xs, wg, wu, wd)


def _combine_kernel(pos_ref, posn_ref, x1_ref, rw_ref, fg_ref, ys_ref, o_ref, ya_scr, yb_scr, sem,
                    *, single_tile):
    tm = x1_ref.shape[0]
    i = pl.program_id(0)
    last = pl.num_programs(0) - 1

    def copies(p_ref, slot, r):
        return (pltpu.make_async_copy(ys_ref.at[pl.ds(p_ref[0, 0, r], 1)], ya_scr.at[slot, pl.ds(r, 1)],
                                      sem.at[slot]),
                pltpu.make_async_copy(ys_ref.at[pl.ds(p_ref[0, 1, r], 1)], yb_scr.at[slot, pl.ds(r, 1)],
                                      sem.at[slot]))

    def start_tile(p_ref, slot):
        for r in range(tm):
            ca, cb = copies(p_ref, slot, r)
            ca.start(priority=0)
            cb.start(priority=1)

    def wait_tile(p_ref, slot):
        for r in range(tm):
            ca, cb = copies(p_ref, slot, r)
            ca.wait()
            cb.wait()

    if single_tile:
        start_tile(pos_ref, 0)
        wait_tile(pos_ref, 0)
        rw = rw_ref[...]
        x2 = x1_ref[...] + rw[:, 0:1] * ya_scr[0] + rw[:, 1:2] * yb_scr[0]
        o_ref[...] = _rms(x2, fg_ref[...])
        return

    @pl.when(i == 0)
    def _():
        start_tile(pos_ref, 0)

    for slot in range(2):
        @pl.when(lax.rem(i, 2) == slot)
        def _(slot=slot):
            @pl.when(i < last)
            def _():
                start_tile(posn_ref, 1 - slot)

            wait_tile(pos_ref, slot)
            rw = rw_ref[...]
            x2 = x1_ref[...] + rw[:, 0:1] * ya_scr[slot] + rw[:, 1:2] * yb_scr[slot]
            o_ref[...] = _rms(x2, fg_ref[...])


def _combine(pos, x1, rw, fg, ys, *, tile0):
    n = x1.shape[0]
    nt = n // TM
    tok_m = pl.BlockSpec((TM, D_MODEL), lambda i: (i, 0))
    return pl.pallas_call(
        functools.partial(_combine_kernel, single_tile=(nt == 1)),
        grid=(nt,),
        in_specs=[
            pl.BlockSpec((1, 2, TM), lambda i: (i + tile0, 0, 0), memory_space=pltpu.SMEM),
            pl.BlockSpec((1, 2, TM), lambda i: (jnp.minimum(i + 1, nt - 1) + tile0, 0, 0),
                         memory_space=pltpu.SMEM),
            tok_m,
            pl.BlockSpec((TM, ROUTE_W), lambda i: (i + tile0, 0)),
            _const_spec((1, D_MODEL)),
            pl.BlockSpec(memory_space=pl.ANY),
        ],
        out_specs=tok_m,
        out_shape=jax.ShapeDtypeStruct((n, D_MODEL), F32),
        scratch_shapes=[pltpu.VMEM((2, TM, D_MODEL), F32), pltpu.VMEM((2, TM, D_MODEL), F32),
                        pltpu.SemaphoreType.DMA((2,))],
        compiler_params=_params(("arbitrary",)),
        name="combine",
    )(pos, pos, x1, rw, fg, ys)


def _row(v):
    return v.reshape(1, -1).astype(F32)


def _prepare_weights(norm1_g, w_in, b_conv_in, b_gate, mu_shift, w0, w_lora2, a0, a_lora2, g_lora2,
                     k_k, k_a, r_k, lnx_g, lnx_b, w_out_a, conv_w, conv_b, cln_g, cln_b, w_out_b, w_o,
                     norm2_g, w_rg, b_rg, w_re, b_re, w_gate_e, w_up_e, w_down_e, final_g):
    w = {}
    w["n1"] = _row(norm1_g)
    w["wr"] = _permute_cols(w_in[:, :SHIFT_W]).astype(BF16)
    w["wc"] = w_in[:, SHIFT_W:SHIFT_W + 2 * D_B].astype(BF16)
    w["wg"] = w_in[:, SHIFT_W + 2 * D_B:].astype(BF16)
    w["bc"] = _row(b_conv_in)
    w["bg"] = _row(b_gate)
    w["mu"] = _permute_cols(mu_shift).reshape(1, P_W)
    w["vp"] = jnp.concatenate([jnp.stack([w0, a0, k_k, k_a, r_k.reshape(-1)]),
                               jnp.zeros((3, D_A), F32)], axis=0)
    z = jnp.zeros((LORA_W, D_A), F32)
    w2 = jnp.concatenate([jnp.concatenate([w_lora2, z], axis=1),
                          jnp.concatenate([z, a_lora2], axis=1)], axis=0)
    w["w2h"], w["w2l"] = _split(w2)
    g2 = jnp.pad(g_lora2, ((0, 2 * LANE - LORA_G), (0, 0)))
    w["g2h"], w["g2l"] = _split(g2)
    w["lng"] = _row(lnx_g)
    w["lnb"] = _row(lnx_b)
    w["woa"] = w_out_a.astype(BF16)
    w["wob"] = w_out_b.astype(BF16)
    w["cw"] = jnp.pad(conv_w, ((0, HALO - CONV_W), (0, 0))).reshape(HALO, D_B // LANE, LANE).transpose(1, 0, 2)
    w["cb"] = conv_b.reshape(D_B // LANE, 1, LANE)
    w["clg"] = _row(cln_g)
    w["clb"] = _row(cln_b)
    w["wo"] = w_o.astype(BF16)
    w["n2"] = _row(norm2_g)
    pad_r = ROUTE_W - N_GROUPS - N_EXPERTS
    w["wrh"], w["wrl"] = _split(jnp.pad(jnp.concatenate([w_rg, w_re], axis=1), ((0, 0), (0, pad_r))))
    w["brt"] = jnp.pad(jnp.concatenate([b_rg, b_re]), (0, pad_r)).reshape(1, ROUTE_W)
    w["wge"] = w_gate_e
    w["wue"] = w_up_e
    w["wde"] = w_down_e
    w["fg"] = _row(final_g)
    return w


def _trunk(x, shift0, wkv0, conv0, w, *, chunk, nchunk, intra_chunks, conv_tm):
    b, t, _ = x.shape
    n = b * t
    xf = x.reshape(n, D_MODEL)
    shift_p = _permute_cols(shift0)
    tps = max(t // TM, 1)
    nb, step = (1, TM) if t >= TM else (TM // t, t)
    u, tails, rt, at, bt, kt, vv, ge, bonus, g = _inprep(
        xf, w["n1"], w["wr"], w["wc"], w["bc"], shift_p.reshape(-1, nb, P_W), w["mu"], w["vp"],
        w["w2h"], w["w2l"], w["g2h"], w["g2l"], chunk=chunk, nb=nb, step=step, tps=tps)
    rc, yc, m2, n2 = _wkv_intra(rt, at, bt, kt, vv, ge, chunk=chunk, nchunk=intra_chunks)
    y, wkv_new = _wkv_scan(rc, yc, m2, n2, wkv0, n_streams=b, t_len=t, chunk=chunk, nchunk=nchunk)
    hinit = jnp.pad(conv0, ((0, 0), (HALO - (CONV_W - 1), 0), (0, 0)))
    ob = _conv(u, hinit, w["cw"], w["cb"], w["clg"], w["clb"], n_streams=b, t_len=t, tm=conv_tm)
    merged = _merge(xf, y, bonus, g, ob, w["n1"], w["wg"], w["bg"], w["lng"], w["lnb"], w["woa"], w["wob"])
    x1, h2, lg = _outproj(merged, xf, w["wo"], w["n2"], w["wrh"], w["wrl"], w["brt"])
    new_shift = tails.reshape(b, -1, P_W)[:, -1, :SHIFT_W][:, _INV_PERM]
    ext_tail = jnp.concatenate([conv0, u.reshape(b, t, D_B)[:, -(CONV_W - 1):]], axis=1)[:, -(CONV_W - 1):]
    return x1, h2, lg, new_shift, wkv_new, ext_tail


def kernel(x_prompt, x_sample, state_shift, state_wkv, cache_conv, norm1_g, w_in, b_conv_in, b_gate,
           mu_shift, w0, w_lora2, a0, a_lora2, g_lora2, k_k, k_a, r_k, lnx_g, lnx_b, w_out_a, conv_w,
           conv_b, cln_g, cln_b, w_out_b, w_o, norm2_g, w_rg, b_rg, w_re, b_re, w_gate_e, w_up_e,
           w_down_e, final_g):
    per_layer = (norm1_g, w_in, b_conv_in, b_gate, mu_shift, w0, w_lora2, a0, a_lora2, g_lora2, k_k, k_a,
                 r_k, lnx_g, lnx_b, w_out_a, conv_w, conv_b, cln_g, cln_b, w_out_b, w_o, norm2_g, w_rg,
                 b_rg, w_re, b_re, w_gate_e, w_up_e, w_down_e)
    assert norm1_g.shape[0] == 1, "single layer only"
    w = _prepare_weights(*[p[0] for p in per_layer], final_g)
    bp, tp, _ = x_prompt.shape
    bs, ts, _ = x_sample.shape
    zero_shift = jnp.zeros((bp, SHIFT_W), F32)
    zero_wkv = jnp.zeros((bp, N_HEADS, HEAD, HEAD), F32)
    zero_conv = jnp.zeros((bp, CONV_W - 1, D_B), F32)
    x1p, h2p, lgp, shift_p, wkv_p, conv_p = _trunk(
        x_prompt, zero_shift, zero_wkv, zero_conv, w, chunk=64, nchunk=min(8, tp // 64),
        intra_chunks=min(16, tp // 64), conv_tm=TM)
    x1s, h2s, lgs, shift_s, wkv_s, conv_s = _trunk(
        x_sample, state_shift[0], state_wkv[0], cache_conv[0], w, chunk=ts, nchunk=1,
        intra_chunks=TM // ts, conv_tm=ts)

    n_p, n_s = bp * tp, bs * ts
    n_all = n_p + n_s
    n_tiles = (2 * n_all + N_EXPERTS * (TMX - 1) + TMX - 1) // TMX
    ri, rw, cnt, xs = _route(jnp.concatenate([lgp, lgs], axis=0), n_tiles * TMX)
    counts = cnt[0, :N_EXPERTS].astype(jnp.int32)
    tiles_per = (counts + TMX - 1) // TMX
    tile_end = jnp.cumsum(tiles_per)
    offs = (tile_end - tiles_per) * TMX
    t_idx = jnp.arange(n_tiles, dtype=jnp.int32)
    te = jnp.sum(t_idx[:, None] >= tile_end[None, :], axis=1).astype(jnp.int32)
    tv = (t_idx < tile_end[-1]).astype(jnp.int32)
    e_ids = jnp.arange(N_EXPERTS, dtype=jnp.int32)
    last_e = jnp.max(jnp.where(tiles_per > 0, e_ids, 0))
    te = jnp.where(tv == 1, jnp.minimum(te, N_EXPERTS - 1), last_e)
    base = jnp.sum(jnp.where(ri[:, 0:2, :, None] == e_ids, offs, 0), axis=-1)
    pos = base + ri[:, 2:4, :]
    xs = _dispatch(pos, h2p, xs, tile0=0)
    xs = _dispatch(pos, h2s, xs, tile0=n_p // TM)
    ys = _experts(te, tv, xs, w["wge"], w["wue"], w["wde"])
    yp = _combine(pos, x1p, rw, w["fg"], ys, tile0=0)
    ysm = _combine(pos, x1s, rw, w["fg"], ys, tile0=n_p // TM)
    return (yp.reshape(bp, tp, D_MODEL), ysm.reshape(bs, ts, D_MODEL),
            shift_p[None], wkv_p[None], conv_p[None],
            shift_s[None], wkv_s[None], conv_s[None])
```
